```python
import math
import jax, jax.numpy as jnp
from jax import lax
import numpy as np

D_MODEL = 4096
BATCH = 4
SEQ = 4096
DEPTH = 2

RMS_EPS = 1e-6
Q_BLOCK = 128
N_BRANCH = 4
MIX_WIDTH = D_MODEL // 4

SSD_D_INNER = MIX_WIDTH
SSD_HEAD_DIM = 64
SSD_N_HEADS = SSD_D_INNER // SSD_HEAD_DIM
SSD_N_GROUPS = 4
SSD_D_STATE = 128
SSD_CONV = 4
SSD_CHUNK = 128
SSD_CONV_DIM = SSD_D_INNER + 2 * SSD_N_GROUPS * SSD_D_STATE

S5_WIDTH = MIX_WIDTH
S5_GROUP = 16
S5_N_GROUPS = S5_WIDTH // S5_GROUP
S5_STATE = 64

MLA_N_HEADS = 8
MLA_Q_RANK = 768
MLA_KV_RANK = 512
MLA_NOPE = 128
MLA_ROPE = 64
MLA_V = MIX_WIDTH // MLA_N_HEADS
MLA_QK = MLA_NOPE + MLA_ROPE
ROPE_THETA = 10000.0

DIFF_N_HEADS = 8
DIFF_HEAD_DIM = MIX_WIDTH // (2 * DIFF_N_HEADS)

REL_BUCKETS = 32
REL_MAX_DIST = 128

FFN_DENSE = 14336
N_EXPERTS = 8
TOP_K = 2
FFN_EXPERT = 3584
N_DENSE = (DEPTH + 1) // 2
N_MOE = DEPTH // 2

IN_SPLITS = (SSD_D_INNER, SSD_CONV_DIM, SSD_N_HEADS, S5_WIDTH, MLA_Q_RANK, MLA_KV_RANK, MLA_ROPE,
             MIX_WIDTH, MIX_WIDTH, MIX_WIDTH)
IN_COLS = sum(IN_SPLITS)

kernel_name = "hybrid_ssd_s5_mla_diffattn_moe_trunk"


def rmsnorm(x, g):
    xf = x.astype(jnp.float32)
    y = xf * lax.rsqrt(jnp.mean(xf * xf, axis=-1, keepdims=True) + RMS_EPS)
    return (y * g.astype(jnp.float32)).astype(x.dtype)


def causal_depthwise_conv(x, w, b):
    k = w.shape[0]
    y = lax.conv_general_dilated(x, w[:, None, :], window_strides=(1,), padding=[(k - 1, 0)],
                                 dimension_numbers=("NWC", "WIO", "NWC"),
                                 feature_group_count=x.shape[-1])
    return y + b


def segsum(a):
    t = a.shape[-1]
    idx = jnp.arange(t)
    xx = jnp.broadcast_to(a[..., :, None], a.shape + (t,))
    xx = jnp.where(idx[:, None] > idx[None, :], xx, 0)
    cs = jnp.cumsum(xx, axis=-2)
    return jnp.where(idx[:, None] >= idx[None, :], cs, -jnp.inf)


def ssd_chunked(xh, da, bh, ch):
    bsz, seq_len, nh, hp = xh.shape
    nc, q = seq_len // SSD_CHUNK, SSD_CHUNK
    x = xh.reshape(bsz, nc, q, nh, hp)
    b_ = bh.reshape(bsz, nc, q, nh, -1)
    c_ = ch.reshape(bsz, nc, q, nh, -1)
    a = da.reshape(bsz, nc, q, nh).transpose(0, 3, 1, 2).astype(jnp.float32)
    a_cs = jnp.cumsum(a, axis=-1)
    l_mat = jnp.exp(segsum(a))
    y_diag = jnp.einsum("bclhn,bcshn,bhcls,bcshp->bclhp", c_, b_, l_mat, x)
    decay_states = jnp.exp(a_cs[..., -1:] - a_cs)
    states = jnp.einsum("bclhn,bhcl,bclhp->bchpn", b_, decay_states, x)
    states = jnp.concatenate([jnp.zeros_like(states[:, :1]), states], axis=1)
    chunk_tot = jnp.pad(a_cs[..., -1], ((0, 0), (0, 0), (1, 0)))
    decay_chunk = jnp.exp(segsum(chunk_tot))
    new_states = jnp.einsum("bhzc,bchpn->bzhpn", decay_chunk, states)
    states_in = new_states[:, :-1]
    y_off = jnp.einsum("bclhn,bchpn,bhcl->bclhp", c_, states_in, jnp.exp(a_cs))
    return (y_diag + y_off).reshape(bsz, seq_len, nh, hp)


def mamba2_mixer(z, xbc, dt_raw, conv_w, conv_b, dt_bias, a_log, d_skip, norm_g):
    bsz, seq_len, _ = z.shape
    xbc = jax.nn.silu(causal_depthwise_conv(xbc, conv_w, conv_b))
    x_s, b_g, c_g = jnp.split(xbc, [SSD_D_INNER, SSD_D_INNER + SSD_N_GROUPS * SSD_D_STATE], axis=-1)
    x_s = x_s.reshape(bsz, seq_len, SSD_N_HEADS, SSD_HEAD_DIM)
    per_group = SSD_N_HEADS // SSD_N_GROUPS
    b_h = jnp.repeat(b_g.reshape(bsz, seq_len, SSD_N_GROUPS, SSD_D_STATE), per_group, axis=2)
    c_h = jnp.repeat(c_g.reshape(bsz, seq_len, SSD_N_GROUPS, SSD_D_STATE), per_group, axis=2)
    dt = jax.nn.softplus((dt_raw + dt_bias).astype(jnp.float32))
    a = -jnp.exp(a_log.astype(jnp.float32))
    y = ssd_chunked(x_s * dt[..., None].astype(x_s.dtype), dt * a, b_h, c_h)
    y = y + x_s * d_skip[:, None]
    y = y.reshape(bsz, seq_len, SSD_D_INNER) * jax.nn.silu(z)
    y = rmsnorm(y.reshape(bsz, seq_len, SSD_N_GROUPS, -1), norm_g.reshape(SSD_N_GROUPS, -1))
    return y.reshape(bsz, seq_len, SSD_D_INNER).astype(z.dtype)


def s5_mixer(u, a_re, a_im, b_re, b_im, c_re, c_im, log_dt, d_skip, w_glu, b_glu):
    f32 = jnp.float32
    bsz, seq_len, _ = u.shape
    dt = jnp.exp(log_dt.astype(f32))[:, None]
    ar, ai = a_re.astype(f32), a_im.astype(f32)
    mag = jnp.exp(ar * dt)
    lr, li = mag * jnp.cos(ai * dt), mag * jnp.sin(ai * dt)
    den = ar * ar + ai * ai
    fr = ((lr - 1.0) * ar + li * ai) / den
    fi = (li * ar - (lr - 1.0) * ai) / den
    br, bi = b_re.astype(f32), b_im.astype(f32)
    bbar_r = fr[..., None] * br - fi[..., None] * bi
    bbar_i = fr[..., None] * bi + fi[..., None] * br
    ug = u.reshape(bsz, seq_len, S5_N_GROUPS, S5_GROUP).astype(f32)
    xr = jnp.einsum("blgc,gnc->blgn", ug, bbar_r)
    xi = jnp.einsum("blgc,gnc->blgn", ug, bbar_i)
    lam_r = jnp.broadcast_to(lr, (1, seq_len) + lr.shape)
    lam_i = jnp.broadcast_to(li, (1, seq_len) + li.shape)

    def combine(e1, e2):
        a1r, a1i, b1r, b1i = e1
        a2r, a2i, b2r, b2i = e2
        return (a1r * a2r - a1i * a2i, a1r * a2i + a1i * a2r,
                a2r * b1r - a2i * b1i + b2r, a2r * b1i + a2i * b1r + b2i)

    _, _, sr, si = lax.associative_scan(combine, (lam_r, lam_i, xr, xi), axis=1)
    y = jnp.einsum("blgn,gcn->blgc", sr, c_re.astype(f32)) - jnp.einsum("blgn,gcn->blgc", si, c_im.astype(f32))
    y = y.reshape(bsz, seq_len, S5_WIDTH) + d_skip.astype(f32) * u.astype(f32)
    y = jax.nn.gelu(y)
    return (y * jax.nn.sigmoid(y @ w_glu.astype(f32) + b_glu.astype(f32))).astype(u.dtype)


def rope_tables(seq_len, dim):
    inv = 1.0 / (ROPE_THETA ** (jnp.arange(0, dim, 2, dtype=jnp.float32) / dim))
    ang = jnp.arange(seq_len, dtype=jnp.float32)[:, None] * inv[None, :]
    return jnp.cos(ang), jnp.sin(ang)


def apply_rope_tail(x, cos, sin):
    x_pass, x_rot = x[..., :-MLA_ROPE], x[..., -MLA_ROPE:]
    x1, x2 = jnp.split(x_rot, 2, axis=-1)
    c, s = cos[None, :, None, :], sin[None, :, None, :]
    rot = jnp.concatenate([x1 * c - x2 * s, x1 * s + x2 * c], axis=-1).astype(x.dtype)
    return jnp.concatenate([x_pass, rot], axis=-1)


def block_positions(i, seq_len):
    q_pos = i * Q_BLOCK + jnp.arange(Q_BLOCK)
    k_pos = jnp.arange(seq_len)
    return q_pos, k_pos, k_pos[None, :] <= q_pos[:, None]


def causal_block_sweep(block_fn, seq_len):
    out = lax.map(block_fn, jnp.arange(seq_len // Q_BLOCK))
    out = jnp.moveaxis(out, 0, 1)
    return out.reshape((out.shape[0], seq_len) + out.shape[3:])


def t5_bucket(n):
    max_exact = REL_BUCKETS // 2
    nf = jnp.maximum(n, 1).astype(jnp.float32)
    large = max_exact + (jnp.log(nf / max_exact) / math.log(REL_MAX_DIST / max_exact)
                         * (REL_BUCKETS - max_exact)).astype(jnp.int32)
    large = jnp.minimum(large, REL_BUCKETS - 1)
    return jnp.where(n < max_exact, n, large)


def mla_mixer(cq, ckv, k_pe, q_norm_g, w_q_up, kv_norm_g, w_kv_up, q_gain, k_gain, cos, sin):
    bsz, seq_len, _ = cq.shape
    q = (rmsnorm(cq, q_norm_g) @ w_q_up).reshape(bsz, seq_len, MLA_N_HEADS, MLA_QK)
    kv = (rmsnorm(ckv, kv_norm_g) @ w_kv_up).reshape(bsz, seq_len, MLA_N_HEADS, MLA_NOPE + MLA_V)
    k_nope, v = jnp.split(kv, [MLA_NOPE], axis=-1)
    k = jnp.concatenate([k_nope, jnp.broadcast_to(k_pe[:, :, None, :], (bsz, seq_len, MLA_N_HEADS, MLA_ROPE))], axis=-1)
    q = apply_rope_tail(rmsnorm(q, q_gain), cos, sin)
    k = apply_rope_tail(rmsnorm(k, k_gain), cos, sin)
    scale = MLA_QK ** -0.5

    def block(i):
        _, _, mask = block_positions(i, seq_len)
        qb = lax.dynamic_slice_in_dim(q, i * Q_BLOCK, Q_BLOCK, axis=1)
        s = jnp.einsum("bqhd,bkhd->bhqk", qb, k).astype(jnp.float32) * scale
        p = jax.nn.softmax(jnp.where(mask, s, -jnp.inf), axis=-1).astype(v.dtype)
        return jnp.einsum("bhqk,bkhd->bqhd", p, v)

    o = causal_block_sweep(block, seq_len)
    return o.reshape(bsz, seq_len, MIX_WIDTH)


def diff_mixer(q, k, v, q_gain, k_gain, lq1, lk1, lq2, lk2, subln_g, rel_bias, lambda_init):
    bsz, seq_len, _ = q.shape
    q = rmsnorm(q.reshape(bsz, seq_len, DIFF_N_HEADS, 2, DIFF_HEAD_DIM), q_gain)
    k = rmsnorm(k.reshape(bsz, seq_len, DIFF_N_HEADS, 2, DIFF_HEAD_DIM), k_gain)
    v = v.reshape(bsz, seq_len, DIFF_N_HEADS, 2 * DIFF_HEAD_DIM)
    f32 = jnp.float32
    lam = (jnp.exp(jnp.sum(lq1.astype(f32) * lk1.astype(f32)))
           - jnp.exp(jnp.sum(lq2.astype(f32) * lk2.astype(f32))) + lambda_init)
    scale = DIFF_HEAD_DIM ** -0.5

    def block(i):
        q_pos, k_pos, mask = block_positions(i, seq_len)
        buckets = t5_bucket(jnp.maximum(q_pos[:, None] - k_pos[None, :], 0))
        bias = jnp.transpose(rel_bias[buckets], (2, 0, 1)).astype(f32)
        qb = lax.dynamic_slice_in_dim(q, i * Q_BLOCK, Q_BLOCK, axis=1)
        s = jnp.einsum("bqhjd,bkhjd->bhjqk", qb, k).astype(f32) * scale + bias[None, :, None]
        p = jax.nn.softmax(jnp.where(mask, s, -jnp.inf), axis=-1)
        attn = (p[:, :, 0] - lam * p[:, :, 1]).astype(v.dtype)
        return jnp.einsum("bhqk,bkhd->bqhd", attn, v)

    o = causal_block_sweep(block, seq_len)
    o = rmsnorm(o, subln_g) * (1.0 - lambda_init)
    return o.reshape(bsz, seq_len, MIX_WIDTH).astype(q.dtype)


def swiglu(h, w_gate, w_up, w_down):
    return (jax.nn.silu(h @ w_gate) * (h @ w_up)) @ w_down


def moe_swiglu(h, w_router, b_router, w_gate, w_up, w_down):
    bsz, seq_len, d = h.shape
    t = h.reshape(bsz * seq_len, d)
    logits = (t @ w_router).astype(jnp.float32) + b_router.astype(jnp.float32)
    top_val, top_idx = lax.top_k(logits, TOP_K)
    top_w = jax.nn.softmax(top_val, axis=-1)
    combine = jnp.sum(jax.nn.one_hot(top_idx, N_EXPERTS, dtype=jnp.float32) * top_w[..., None], axis=1)
    out = jnp.zeros_like(t)
    for e in range(N_EXPERTS):
        out = out + combine[:, e:e + 1].astype(t.dtype) * swiglu(t, w_gate[e], w_up[e], w_down[e])
    return out.reshape(bsz, seq_len, d)


def setup_inputs(seed: int = 0) -> dict:
    key = jax.random.key(seed)
    keys = iter(jax.random.split(key, 64))
    f32 = jnp.float32

    def nrm(shape, scale):
        return scale * jax.random.normal(next(keys), shape, f32)

    def gain(shape):
        return 1.0 + nrm(shape, 0.02)

    def unif(shape, lo, hi):
        return jax.random.uniform(next(keys), shape, f32, lo, hi)

    nl = DEPTH
    s5_n = jnp.arange(S5_STATE, dtype=f32)
    inputs = {}
    inputs["x"] = nrm((BATCH, SEQ, D_MODEL), 1.0)
    inputs["norm_mix_g"] = gain((nl, D_MODEL))
    inputs["w_in"] = nrm((nl, D_MODEL, IN_COLS), D_MODEL ** -0.5)
    inputs["ssd_conv_w"] = nrm((nl, SSD_CONV, SSD_CONV_DIM), SSD_CONV ** -0.5)
    inputs["ssd_conv_b"] = nrm((nl, SSD_CONV_DIM), 0.02)
    dt0 = jnp.exp(unif((nl, SSD_N_HEADS), math.log(1e-3), math.log(1e-1)))
    inputs["ssd_dt_bias"] = dt0 + jnp.log(-jnp.expm1(-dt0))
    inputs["ssd_a_log"] = jnp.log(unif((nl, SSD_N_HEADS), 1.0, 16.0))
    inputs["ssd_d"] = 1.0 + nrm((nl, SSD_N_HEADS), 0.1)
    inputs["ssd_norm_g"] = gain((nl, SSD_D_INNER))
    inputs["s5_a_re"] = -0.5 + nrm((nl, S5_N_GROUPS, S5_STATE), 0.01)
    inputs["s5_a_im"] = math.pi * s5_n + nrm((nl, S5_N_GROUPS, S5_STATE), 0.01)
    inputs["s5_b_re"] = nrm((nl, S5_N_GROUPS, S5_STATE, S5_GROUP), (2 * S5_GROUP) ** -0.5)
    inputs["s5_b_im"] = nrm((nl, S5_N_GROUPS, S5_STATE, S5_GROUP), (2 * S5_GROUP) ** -0.5)
    inputs["s5_c_re"] = nrm((nl, S5_N_GROUPS, S5_GROUP, S5_STATE), S5_STATE ** -0.5)
    inputs["s5_c_im"] = nrm((nl, S5_N_GROUPS, S5_GROUP, S5_STATE), S5_STATE ** -0.5)
    inputs["s5_log_dt"] = unif((nl, S5_N_GROUPS), math.log(1e-3), math.log(1e-1))
    inputs["s5_d"] = nrm((nl, S5_WIDTH), 1.0)
    inputs["s5_w_glu"] = nrm((nl, S5_WIDTH, S5_WIDTH), S5_WIDTH ** -0.5)
    inputs["s5_b_glu"] = nrm((nl, S5_WIDTH), 0.02)
    inputs["mla_q_norm_g"] = gain((nl, MLA_Q_RANK))
    inputs["mla_w_q_up"] = nrm((nl, MLA_Q_RANK, MLA_N_HEADS * MLA_QK), MLA_Q_RANK ** -0.5)
    inputs["mla_kv_norm_g"] = gain((nl, MLA_KV_RANK))
    inputs["mla_w_kv_up"] = nrm((nl, MLA_KV_RANK, MLA_N_HEADS * (MLA_NOPE + MLA_V)), MLA_KV_RANK ** -0.5)
    inputs["mla_q_gain"] = gain((nl, MLA_QK))
    inputs["mla_k_gain"] = gain((nl, MLA_QK))
    inputs["diff_q_gain"] = gain((nl, DIFF_HEAD_DIM))
    inputs["diff_k_gain"] = gain((nl, DIFF_HEAD_DIM))
    inputs["diff_lq1"] = nrm((nl, DIFF_HEAD_DIM), 0.1)
    inputs["diff_lk1"] = nrm((nl, DIFF_HEAD_DIM), 0.1)
    inputs["diff_lq2"] = nrm((nl, DIFF_HEAD_DIM), 0.1)
    inputs["diff_lk2"] = nrm((nl, DIFF_HEAD_DIM), 0.1)
    inputs["diff_subln_g"] = gain((nl, 2 * DIFF_HEAD_DIM))
    inputs["rel_bias"] = nrm((REL_BUCKETS, DIFF_N_HEADS), 0.5)
    inputs["w_gate"] = nrm((nl, N_BRANCH, D_MODEL, D_MODEL), D_MODEL ** -0.5)
    inputs["b_gate"] = nrm((nl, N_BRANCH, D_MODEL), 0.02)
    inputs["w_branch"] = nrm((nl, N_BRANCH, MIX_WIDTH, D_MODEL), MIX_WIDTH ** -0.5)
    inputs["w_out"] = nrm((nl, D_MODEL, D_MODEL), D_MODEL ** -0.5)
    inputs["norm_ffn_g"] = gain((nl, D_MODEL))
    inputs["ffn_w_gate"] = nrm((N_DENSE, D_MODEL, FFN_DENSE), D_MODEL ** -0.5)
    inputs["ffn_w_up"] = nrm((N_DENSE, D_MODEL, FFN_DENSE), D_MODEL ** -0.5)
    inputs["ffn_w_down"] = nrm((N_DENSE, FFN_DENSE, D_MODEL), FFN_DENSE ** -0.5)
    inputs["moe_w_router"] = nrm((N_MOE, D_MODEL, N_EXPERTS), D_MODEL ** -0.5)
    inputs["moe_b_router"] = nrm((N_MOE, N_EXPERTS), 0.01)
    inputs["moe_w_gate"] = nrm((N_MOE, N_EXPERTS, D_MODEL, FFN_EXPERT), D_MODEL ** -0.5)
    inputs["moe_w_up"] = nrm((N_MOE, N_EXPERTS, D_MODEL, FFN_EXPERT), D_MODEL ** -0.5)
    inputs["moe_w_down"] = nrm((N_MOE, N_EXPERTS, FFN_EXPERT, D_MODEL), FFN_EXPERT ** -0.5)
    return inputs


def reference(x, norm_mix_g, w_in, ssd_conv_w, ssd_conv_b, ssd_dt_bias, ssd_a_log, ssd_d, ssd_norm_g,
              s5_a_re, s5_a_im, s5_b_re, s5_b_im, s5_c_re, s5_c_im, s5_log_dt, s5_d, s5_w_glu, s5_b_glu,
              mla_q_norm_g, mla_w_q_up, mla_kv_norm_g, mla_w_kv_up, mla_q_gain, mla_k_gain,
              diff_q_gain, diff_k_gain, diff_lq1, diff_lk1, diff_lq2, diff_lk2, diff_subln_g, rel_bias,
              w_gate, b_gate, w_branch, w_out, norm_ffn_g, ffn_w_gate, ffn_w_up, ffn_w_down,
              moe_w_router, moe_b_router, moe_w_gate, moe_w_up, moe_w_down):
    seq_len = x.shape[1]
    cos, sin = rope_tables(seq_len, MLA_ROPE)
    bounds, acc = [], 0
    for width in IN_SPLITS[:-1]:
        acc += width
        bounds.append(acc)

    for i in range(DEPTH):
        h = rmsnorm(x, norm_mix_g[i])
        proj = h @ w_in[i]
        z, xbc, dt_raw, u, cq, ckv, k_pe, dq, dk, dv = jnp.split(proj, bounds, axis=-1)
        y_a = mamba2_mixer(z, xbc, dt_raw, ssd_conv_w[i], ssd_conv_b[i], ssd_dt_bias[i],
                           ssd_a_log[i], ssd_d[i], ssd_norm_g[i])
        y_b = s5_mixer(u, s5_a_re[i], s5_a_im[i], s5_b_re[i], s5_b_im[i], s5_c_re[i], s5_c_im[i],
                       s5_log_dt[i], s5_d[i], s5_w_glu[i], s5_b_glu[i])
        y_c = mla_mixer(cq, ckv, k_pe, mla_q_norm_g[i], mla_w_q_up[i], mla_kv_norm_g[i],
                        mla_w_kv_up[i], mla_q_gain[i], mla_k_gain[i], cos, sin)
        lambda_init = 0.8 - 0.6 * math.exp(-0.3 * i)
        y_d = diff_mixer(dq, dk, dv, diff_q_gain[i], diff_k_gain[i], diff_lq1[i], diff_lk1[i],
                         diff_lq2[i], diff_lk2[i], diff_subln_g[i], rel_bias, lambda_init)
        merged = jnp.zeros_like(x)
        for j, y_j in enumerate((y_a, y_b, y_c, y_d)):
            gate = jax.nn.sigmoid(h @ w_gate[i, j] + b_gate[i, j])
            merged = merged + gate * (y_j @ w_branch[i, j])
        x = x + merged @ w_out[i]
        h2 = rmsnorm(x, norm_ffn_g[i])
        if i % 2 == 0:
            e = i // 2
            x = x + swiglu(h2, ffn_w_gate[e], ffn_w_up[e], ffn_w_down[e])
        else:
            e = i // 2
            x = x + moe_swiglu(h2, moe_w_router[e], moe_b_router[e], moe_w_gate[e], moe_w_up[e], moe_w_down[e])
    return x
```

```python
import functools
import math

import numpy as np
import jax
import jax.numpy as jnp
from jax import lax
from jax.experimental import pallas as pl
from jax.experimental.pallas import tpu as pltpu

F32 = jnp.float32
BF16 = jnp.bfloat16

V7X_VMEM_BYTES = 64 * 1024 * 1024
VMEM_LIMIT = V7X_VMEM_BYTES - 8 * 1024 * 1024
LANES = 128

RMS_EPS = 1e-6
SSD_HEAD_DIM = 64
SSD_N_GROUPS = 4
SSD_D_STATE = 128
SSD_CHUNK = 128
S5_GROUP = 16
S5_STATE = 64
S5_GROUPS_PER_BLOCK = 8
MLA_N_HEADS = 8
MLA_NOPE = 128
MLA_ROPE = 64
MLA_PAD = 256
ROPE_THETA = 10000.0
DIFF_N_HEADS = 8
REL_BUCKETS = 32
REL_MAX_DIST = 128
TOP_K = 2


def _cparams(sem):
    return pltpu.CompilerParams(dimension_semantics=sem, vmem_limit_bytes=VMEM_LIMIT)


def _tile(n, pref):
    t = min(n, pref)
    while n % t:
        t //= 2
    return t


def _split3(a):
    hi = a.astype(BF16)
    r1 = a - hi.astype(F32)
    mid = r1.astype(BF16)
    lo = (r1 - mid.astype(F32)).astype(BF16)
    return hi, mid, lo


def _dot(a, b):
    return jnp.dot(a, b, preferred_element_type=F32)


def _dot_exact_rhs01(a, sel):
    hi, mid, lo = _split3(a)
    return _dot(hi, sel) + _dot(mid, sel) + _dot(lo, sel)


def _dot_exact_lhs01(sel, b):
    hi, mid, lo = _split3(b)
    return _dot(sel, hi) + _dot(sel, mid) + _dot(sel, lo)


def _rmsnorm_kernel(x_ref, g_ref, o_ref):
    x = x_ref[...]
    ms = jnp.mean(x * x, axis=-1, keepdims=True)
    o_ref[...] = (x * lax.rsqrt(ms + RMS_EPS) * g_ref[...]).astype(o_ref.dtype)


def rmsnorm(x, g, tm=256):
    t, d = x.shape
    tm = _tile(t, tm)
    return pl.pallas_call(
        _rmsnorm_kernel,
        grid=(t // tm,),
        in_specs=[pl.BlockSpec((tm, d), lambda i: (i, 0)), pl.BlockSpec((1, d), lambda i: (0, 0))],
        out_specs=pl.BlockSpec((tm, d), lambda i: (i, 0)),
        out_shape=jax.ShapeDtypeStruct((t, d), BF16),
        compiler_params=_cparams(("parallel",)),
        name="rmsnorm",
    )(x, g.reshape(1, d))


def _rmsnorm_router_kernel(x_ref, g_ref, wr_ref, br_ref, o_ref, comb_ref, *, n_experts):
    x = x_ref[...]
    ms = jnp.mean(x * x, axis=-1, keepdims=True)
    h = x * lax.rsqrt(ms + RMS_EPS) * g_ref[...]
    o_ref[...] = h.astype(o_ref.dtype)
    logits = jnp.dot(h, wr_ref[...], preferred_element_type=F32, precision=lax.Precision.HIGHEST) + br_ref[...]
    lane = lax.broadcasted_iota(jnp.int32, logits.shape, 1)
    neg = jnp.float32(-jnp.inf)
    logits = jnp.where(lane < n_experts, logits, neg)
    m1 = jnp.max(logits, axis=-1, keepdims=True)
    i1 = jnp.min(jnp.where(logits == m1, lane, LANES), axis=-1, keepdims=True)
    rest = jnp.where(lane == i1, neg, logits)
    m2 = jnp.max(rest, axis=-1, keepdims=True)
    i2 = jnp.min(jnp.where(rest == m2, lane, LANES), axis=-1, keepdims=True)
    e2 = jnp.exp(m2 - m1)
    w1 = 1.0 / (1.0 + e2)
    w2 = e2 / (1.0 + e2)
    comb_ref[...] = jnp.where(lane == i1, w1, 0.0) + jnp.where(lane == i2, w2, 0.0)


def rmsnorm_router(x, g, w_router, b_router, tm=256):
    t, d = x.shape
    ne = w_router.shape[1]
    tm = _tile(t, tm)
    wr = jnp.zeros((d, LANES), F32).at[:, :ne].set(w_router)
    br = jnp.zeros((1, LANES), F32).at[0, :ne].set(b_router)
    return pl.pallas_call(
        functools.partial(_rmsnorm_router_kernel, n_experts=ne),
        grid=(t // tm,),
        in_specs=[pl.BlockSpec((tm, d), lambda i: (i, 0)), pl.BlockSpec((1, d), lambda i: (0, 0)),
                  pl.BlockSpec((d, LANES), lambda i: (0, 0)), pl.BlockSpec((1, LANES), lambda i: (0, 0))],
        out_specs=[pl.BlockSpec((tm, d), lambda i: (i, 0)), pl.BlockSpec((tm, LANES), lambda i: (i, 0))],
        out_shape=[jax.ShapeDtypeStruct((t, d), BF16), jax.ShapeDtypeStruct((t, LANES), F32)],
        compiler_params=_cparams(("parallel",)),
        name="rmsnorm_router",
    )(x, g.reshape(1, d), wr, br)


def _mm_kernel(*refs, n_w, n_e, nk, epilogue):
    a_ref = refs[0]
    w_refs = refs[1:1 + n_w]
    e_refs = refs[1 + n_w:1 + n_w + n_e]
    o_ref = refs[1 + n_w + n_e]
    acc_refs = refs[2 + n_w + n_e:]
    a = a_ref[...]
    parts = [_dot(a, w[...]) for w in w_refs]
    if nk == 1:
        o_ref[...] = epilogue(parts, [e[...] for e in e_refs]).astype(o_ref.dtype)
        return
    k = pl.program_id(2)

    @pl.when(k == 0)
    def _():
        for acc, p in zip(acc_refs, parts):
            acc[...] = p

    @pl.when(k > 0)
    def _():
        for acc, p in zip(acc_refs, parts):
            acc[...] += p

    @pl.when(k == nk - 1)
    def _():
        o_ref[...] = epilogue([acc[...] for acc in acc_refs], [e[...] for e in e_refs]).astype(o_ref.dtype)


def matmul(a, ws, epilogue, extras=(), extra_kinds=(), out_dtype=F32, tm=1024, tn=1024, tk=4096, name="matmul"):
    m, kdim = a.shape
    n = ws[0].shape[1]
    tm, tn, tk = _tile(m, tm), _tile(n, tn), _tile(kdim, tk)
    nk = kdim // tk
    in_specs = [pl.BlockSpec((tm, tk), lambda i, j, k: (i, k))]
    in_specs += [pl.BlockSpec((tk, tn), lambda i, j, k: (k, j)) for _ in ws]
    for kind in extra_kinds:
        if kind == "mn":
            in_specs.append(pl.BlockSpec((tm, tn), lambda i, j, k: (i, j)))
        elif kind == "m":
            in_specs.append(pl.BlockSpec((tm, LANES), lambda i, j, k: (i, 0)))
        else:
            in_specs.append(pl.BlockSpec((1, tn), lambda i, j, k: (0, j)))
    scratch = [pltpu.VMEM((tm, tn), F32) for _ in ws] if nk > 1 else []
    return pl.pallas_call(
        functools.partial(_mm_kernel, n_w=len(ws), n_e=len(extras), nk=nk, epilogue=epilogue),
        grid=(m // tm, n // tn, nk),
        in_specs=in_specs,
        out_specs=pl.BlockSpec((tm, tn), lambda i, j, k: (i, j)),
        out_shape=jax.ShapeDtypeStruct((m, n), out_dtype),
        scratch_shapes=scratch,
        compiler_params=_cparams(("parallel", "parallel", "arbitrary")),
        name=name,
    )(a, *ws, *extras)


def _epi_plain(parts, extras):
    return parts[0]


def _epi_residual(parts, extras):
    return extras[0] + parts[0]


def _epi_swiglu(parts, extras):
    g, u = parts
    return g * jax.nn.sigmoid(g) * u


def _epi_swiglu_scaled(col):
    def epi(parts, extras):
        g, u = parts
        return extras[0][:, col:col + 1] * (g * jax.nn.sigmoid(g) * u)
    return epi


def _epi_glu(parts, extras):
    y, b = extras
    return y * jax.nn.sigmoid(parts[0] + b)


def _merge_kernel(h_ref, wg_ref, b_ref, y_ref, wb_ref, o_ref, accg_ref, accm_ref, *, nk, nb):
    j = pl.program_id(2)
    k = pl.program_id(3)
    part = _dot(h_ref[...], wg_ref[0])

    @pl.when(k == 0)
    def _():
        accg_ref[...] = part

    @pl.when(k > 0)
    def _():
        accg_ref[...] += part

    @pl.when(k == nk - 1)
    def _():
        gate = jax.nn.sigmoid(accg_ref[...] + b_ref[0])
        term = gate * _dot(y_ref[0], wb_ref[0])

        @pl.when(j == 0)
        def _():
            accm_ref[...] = term

        @pl.when(j > 0)
        def _():
            accm_ref[...] += term

        @pl.when(j == nb - 1)
        def _():
            o_ref[...] = accm_ref[...].astype(o_ref.dtype)


def gated_merge(h, w_gate, b_gate, ys, w_branch, tm=1024, tn=1024, tk=2048):
    t, d = h.shape
    nb, _, wdt = ys.shape
    tm, tn, tk = _tile(t, tm), _tile(d, tn), _tile(d, tk)
    nk = d // tk
    return pl.pallas_call(
        functools.partial(_merge_kernel, nk=nk, nb=nb),
        grid=(t // tm, d // tn, nb, nk),
        in_specs=[
            pl.BlockSpec((tm, tk), lambda i, n, j, k: (i, k)),
            pl.BlockSpec((1, tk, tn), lambda i, n, j, k: (j, k, n)),
            pl.BlockSpec((1, 1, tn), lambda i, n, j, k: (j, 0, n)),
            pl.BlockSpec((1, tm, wdt), lambda i, n, j, k: (j, i, 0)),
            pl.BlockSpec((1, wdt, tn), lambda i, n, j, k: (j, 0, n)),
        ],
        out_specs=pl.BlockSpec((tm, tn), lambda i, n, j, k: (i, n)),
        out_shape=jax.ShapeDtypeStruct((t, d), BF16),
        scratch_shapes=[pltpu.VMEM((tm, tn), F32), pltpu.VMEM((tm, tn), F32)],
        compiler_params=_cparams(("parallel", "parallel", "arbitrary", "arbitrary")),
        name="gated_merge",
    )(h, w_gate, b_gate, ys, w_branch)


def _softplus(x):
    return jnp.maximum(x, 0.0) + jnp.log1p(jnp.exp(-jnp.abs(x)))


def _ssd_kernel(xbc_ref, z_ref, dt_ref, dtt_ref, cw_ref, cb_ref, dtb_ref, dtbt_ref, alog_ref, alogt_ref,
                dskip_ref, ng_ref, expand_ref, o_ref, xe_ref, st_ref, *, n_heads, d_inner):
    q = SSD_CHUNK
    gn = SSD_N_GROUPS * SSD_D_STATE
    hpg = n_heads // SSD_N_GROUPS
    gw = hpg * SSD_HEAD_DIM
    c = pl.program_id(1)

    @pl.when(c == 0)
    def _():
        xe_ref[0:8, :] = jnp.zeros((8, xe_ref.shape[1]), F32)
        st_ref[...] = jnp.zeros(st_ref.shape, F32)

    xe_ref[8:8 + q, :] = xbc_ref[...]
    acc = cb_ref[...] + cw_ref[3:4, :] * xe_ref[8:8 + q, :]
    for kk in range(3):
        acc = acc + cw_ref[kk:kk + 1, :] * xe_ref[5 + kk:5 + kk + q, :]
    xe_ref[0:8, :] = xe_ref[q:q + 8, :]
    xc = acc * jax.nn.sigmoid(acc)
    xs = xc[:, :d_inner]
    bm = xc[:, d_inner:d_inner + gn].astype(BF16)
    cm = xc[:, d_inner + gn:].astype(BF16)

    dt = _softplus(dt_ref[...] + dtb_ref[...])
    da = dt * (-jnp.exp(alog_ref[...]))
    row = lax.broadcasted_iota(jnp.int32, (q, q), 0)
    col = lax.broadcasted_iota(jnp.int32, (q, q), 1)
    lower = row >= col
    tril = jnp.where(lower, 1.0, 0.0).astype(BF16)
    acs = _dot_exact_lhs01(tril, da)
    dtt = _softplus(dtt_ref[0] + dtbt_ref[...])
    dat = dtt * (-jnp.exp(alogt_ref[...]))
    triu = jnp.where(row <= col, 1.0, 0.0).astype(BF16)
    acst = _dot_exact_rhs01(dat, triu)

    expand = expand_ref[...]
    dt_x = _dot_exact_rhs01(dt, expand)
    acs_x = _dot_exact_rhs01(acs, expand)
    tot_x = acs_x[q - 1:q, :]
    xdt = xs * dt_x
    xdt_b = xdt.astype(BF16)
    xw_b = (xdt * jnp.exp(tot_x - acs_x)).astype(BF16)
    in_decay = jnp.exp(acs_x)
    chunk_decay = jnp.exp(tot_x)

    lane = lax.broadcasted_iota(jnp.int32, (1, gw), 1)
    ys = []
    for g in range(SSD_N_GROUPS):
        cg = cm[:, g * SSD_D_STATE:(g + 1) * SSD_D_STATE]
        bg = bm[:, g * SSD_D_STATE:(g + 1) * SSD_D_STATE]
        sl = slice(g * gw, (g + 1) * gw)
        gmat = lax.dot_general(cg, bg, (((1,), (1,)), ((), ())), preferred_element_type=F32)
        state = st_ref[g]
        y_g = _dot(cg, state.astype(BF16)) * in_decay[:, sl]
        xg = xdt_b[:, sl]
        for hh in range(hpg):
            hd = g * hpg + hh
            seg = acs[:, hd:hd + 1] - acst[hd:hd + 1, :]
            lmat = jnp.exp(jnp.where(lower, seg, -jnp.inf))
            mh = (gmat * lmat).astype(BF16)
            head_cols = (lane >= hh * SSD_HEAD_DIM) & (lane < (hh + 1) * SSD_HEAD_DIM)
            y_g = y_g + _dot(mh, jnp.where(head_cols, xg, jnp.zeros_like(xg)))
        upd = lax.dot_general(bg, xw_b[:, sl], (((0,), (0,)), ((), ())), preferred_element_type=F32)
        st_ref[g] = state * chunk_decay[:, sl] + upd
        ys.append(y_g)
    y = jnp.concatenate(ys, axis=-1) + xs * dskip_ref[...]
    zz = z_ref[...]
    y = y * (zz * jax.nn.sigmoid(zz))
    outs = []
    for g in range(SSD_N_GROUPS):
        yg = y[:, g * gw:(g + 1) * gw]
        ms = jnp.mean(yg * yg, axis=-1, keepdims=True)
        outs.append(yg * lax.rsqrt(ms + RMS_EPS))
    o_ref[...] = (jnp.concatenate(outs, axis=-1) * ng_ref[...]).astype(o_ref.dtype)


def ssd_mixer(proj, dt_t, cols, bsz, seq, conv_w, conv_b, dt_bias, a_log, d_skip, norm_g):
    n_heads = dt_bias.shape[0]
    d_inner = n_heads * SSD_HEAD_DIM
    conv_dim = conv_w.shape[1]
    q = SSD_CHUNK
    nc = seq // q
    pad = LANES - n_heads
    expand = jnp.repeat(jnp.eye(LANES, dtype=BF16)[:, :n_heads], SSD_HEAD_DIM, axis=1)
    dtb = jnp.pad(dt_bias, (0, pad)).reshape(1, LANES)
    alog = jnp.pad(a_log, (0, pad)).reshape(1, LANES)
    x_blk, z_blk, dt_blk = cols["xbc"] // conv_dim, cols["z"] // d_inner, cols["dt"] // LANES
    const = lambda b, c: (0, 0)
    return pl.pallas_call(
        functools.partial(_ssd_kernel, n_heads=n_heads, d_inner=d_inner),
        grid=(bsz, nc),
        in_specs=[
            pl.BlockSpec((q, conv_dim), lambda b, c: (b * nc + c, x_blk)),
            pl.BlockSpec((q, d_inner), lambda b, c: (b * nc + c, z_blk)),
            pl.BlockSpec((q, LANES), lambda b, c: (b * nc + c, dt_blk)),
            pl.BlockSpec((1, n_heads, q), lambda b, c: (b, 0, c)),
            pl.BlockSpec((4, conv_dim), const),
            pl.BlockSpec((1, conv_dim), const),
            pl.BlockSpec((1, LANES), const),
            pl.BlockSpec((n_heads, 1), const),
            pl.BlockSpec((1, LANES), const),
            pl.BlockSpec((n_heads, 1), const),
            pl.BlockSpec((1, d_inner), const),
            pl.BlockSpec((1, d_inner), const),
            pl.BlockSpec((LANES, d_inner), const),
        ],
        out_specs=pl.BlockSpec((q, d_inner), lambda b, c: (b * nc + c, 0)),
        out_shape=jax.ShapeDtypeStruct((bsz * seq, d_inner), BF16),
        scratch_shapes=[pltpu.VMEM((q + 8, conv_dim), F32),
                        pltpu.VMEM((SSD_N_GROUPS, SSD_D_STATE, d_inner // SSD_N_GROUPS), F32)],
        compiler_params=_cparams(("parallel", "arbitrary")),
        name="ssd_mixer",
    )(proj, proj, proj, dt_t, conv_w, conv_b.reshape(1, conv_dim), dtb, dt_bias.reshape(n_heads, 1),
      alog, a_log.reshape(n_heads, 1), jnp.repeat(d_skip, SSD_HEAD_DIM).reshape(1, d_inner),
      norm_g.reshape(1, d_inner), expand)


def _s5_disc_kernel(are_ref, aim_ref, ldt_ref, brt_ref, bit_ref, lr_ref, li_ref, bbr_ref, bbi_ref):
    dt = jnp.exp(ldt_ref[...])
    ar, ai = are_ref[...], aim_ref[...]
    mag = jnp.exp(ar * dt)
    lr, li = mag * jnp.cos(ai * dt), mag * jnp.sin(ai * dt)
    den = ar * ar + ai * ai
    fr = ((lr - 1.0) * ar + li * ai) / den
    fi = (li * ar - (lr - 1.0) * ai) / den
    lr_ref[...] = lr
    li_ref[...] = li
    br, bi = brt_ref[...], bit_ref[...]
    bbr_ref[...] = fr[:, None, :] * br - fi[:, None, :] * bi
    bbi_ref[...] = fr[:, None, :] * bi + fi[:, None, :] * br


def s5_discretise(a_re, a_im, log_dt, b_re, b_im):
    g, n = a_re.shape
    c = b_re.shape[2]
    brt, bit = jnp.swapaxes(b_re, 1, 2), jnp.swapaxes(b_im, 1, 2)
    return pl.pallas_call(
        _s5_disc_kernel,
        out_shape=[jax.ShapeDtypeStruct((g, n), F32), jax.ShapeDtypeStruct((g, n), F32),
                   jax.ShapeDtypeStruct((g, c, n), F32), jax.ShapeDtypeStruct((g, c, n), F32)],
        name="s5_discretise",
    )(a_re, a_im, log_dt.reshape(g, 1), brt, bit)


def _block_diag(blocks, per):
    g, r, c = blocks.shape
    nb = g // per
    eye = jnp.eye(per, dtype=blocks.dtype)
    out = blocks.reshape(nb, per, r, 1, c) * eye[None, :, None, :, None]
    return out.reshape(nb, per * r, per * c)


def _s5_in_kernel(u_ref, wr_ref, wi_ref, xr_ref, xi_ref):
    u = u_ref[...].astype(BF16)
    xr_ref[...] = _dot(u, wr_ref[0])
    xi_ref[...] = _dot(u, wi_ref[0])


def _s5_scan_kernel(lr_ref, li_ref, xr_ref, xi_ref, sr_ref, si_ref, st_ref, *, tc):
    @pl.when(pl.program_id(1) == 0)
    def _():
        st_ref[...] = jnp.zeros(st_ref.shape, F32)

    lr, li = lr_ref[...], li_ref[...]

    def body(t, carry):
        sr, si = carry
        nr = lr * sr - li * si + xr_ref[0, t]
        ni = lr * si + li * sr + xi_ref[0, t]
        sr_ref[0, t] = nr
        si_ref[0, t] = ni
        return nr, ni

    sr, si = lax.fori_loop(0, tc, body, (st_ref[0], st_ref[1]), unroll=4)
    st_ref[0] = sr
    st_ref[1] = si


def _s5_out_kernel(sr_ref, si_ref, cr_ref, ci_ref, u_ref, d_ref, o_ref):
    y = _dot(sr_ref[...].astype(BF16), cr_ref[0]) - _dot(si_ref[...].astype(BF16), ci_ref[0])
    y = y + d_ref[...] * u_ref[...]
    o_ref[...] = jax.nn.gelu(y).astype(o_ref.dtype)


def s5_mixer(proj, u_col, bsz, seq, a_re, a_im, b_re, b_im, c_re, c_im, log_dt, d_skip, w_glu, b_glu):
    t = bsz * seq
    g, n = a_re.shape
    width = g * S5_GROUP
    per = S5_GROUPS_PER_BLOCK
    nblk = g // per
    sw = per * n
    ns = g * n
    lr, li, bbr, bbi = s5_discretise(a_re, a_im, log_dt, b_re, b_im)
    w_r = _block_diag(bbr, per).astype(BF16)
    w_i = _block_diag(bbi, per).astype(BF16)
    tm = _tile(t, 1024)
    u_blk = u_col // LANES
    xr, xi = pl.pallas_call(
        _s5_in_kernel,
        grid=(t // tm, nblk),
        in_specs=[pl.BlockSpec((tm, LANES), lambda i, j: (i, u_blk + j)),
                  pl.BlockSpec((1, LANES, sw), lambda i, j: (j, 0, 0)),
                  pl.BlockSpec((1, LANES, sw), lambda i, j: (j, 0, 0))],
        out_specs=[pl.BlockSpec((tm, sw), lambda i, j: (i, j)), pl.BlockSpec((tm, sw), lambda i, j: (i, j))],
        out_shape=[jax.ShapeDtypeStruct((t, ns), F32), jax.ShapeDtypeStruct((t, ns), F32)],
        compiler_params=_cparams(("parallel", "parallel")),
        name="s5_in",
    )(proj, w_r, w_i)

    rows = ns // LANES
    tc = _tile(seq, 128)
    nct = seq // tc
    blk = pl.BlockSpec((1, tc, rows, LANES), lambda b, c: (b, c, 0, 0))
    lam = pl.BlockSpec((rows, LANES), lambda b, c: (0, 0))
    sr, si = pl.pallas_call(
        functools.partial(_s5_scan_kernel, tc=tc),
        grid=(bsz, nct),
        in_specs=[lam, lam, blk, blk],
        out_specs=[blk, blk],
        out_shape=[jax.ShapeDtypeStruct((bsz, seq, rows, LANES), F32)] * 2,
        scratch_shapes=[pltpu.VMEM((2, rows, LANES), F32)],
        compiler_params=_cparams(("parallel", "arbitrary")),
        name="s5_scan",
    )(lr.reshape(rows, LANES), li.reshape(rows, LANES),
      xr.reshape(bsz, seq, rows, LANES), xi.reshape(bsz, seq, rows, LANES))

    c_r = _block_diag(jnp.swapaxes(c_re, 1, 2), per).astype(BF16)
    c_i = _block_diag(jnp.swapaxes(c_im, 1, 2), per).astype(BF16)
    y1 = pl.pallas_call(
        _s5_out_kernel,
        grid=(t // tm, nblk),
        in_specs=[pl.BlockSpec((tm, sw), lambda i, j: (i, j)), pl.BlockSpec((tm, sw), lambda i, j: (i, j)),
                  pl.BlockSpec((1, sw, LANES), lambda i, j: (j, 0, 0)),
                  pl.BlockSpec((1, sw, LANES), lambda i, j: (j, 0, 0)),
                  pl.BlockSpec((tm, LANES), lambda i, j: (i, u_blk + j)),
                  pl.BlockSpec((1, LANES), lambda i, j: (0, j))],
        out_specs=pl.BlockSpec((tm, LANES), lambda i, j: (i, j)),
        out_shape=jax.ShapeDtypeStruct((t, width), F32),
        compiler_params=_cparams(("parallel", "parallel")),
        name="s5_out",
    )(sr.reshape(t, ns), si.reshape(t, ns), c_r, c_i, proj, d_skip.reshape(1, width))
    return matmul(y1.astype(BF16), [w_glu.astype(BF16)], _epi_glu, extras=(y1, b_glu.reshape(1, width)),
                  extra_kinds=("mn", "n"), out_dtype=BF16, name="s5_glu")


def _flash_kernel(*refs, tq, n_maps, has_bias, finalize):
    if has_bias:
        q_ref, k_ref, v_ref, bias_ref, far_ref, *rest = refs
    else:
        q_ref, k_ref, v_ref, *rest = refs
    *fin_refs, o_ref = rest
    h = pl.program_id(1)
    qi = pl.program_id(2)
    rows = n_maps * tq
    dv = v_ref.shape[1]
    qv = q_ref[...].reshape(rows, q_ref.shape[-1])

    def step(kblk, vblk, m, l, acc, add):
        s = lax.dot_general(qv, kblk, (((1,), (1,)), ((), ())), preferred_element_type=F32)
        if add is not None:
            s = s + add
        m_new = jnp.maximum(m, jnp.max(s, axis=-1, keepdims=True))
        alpha = jnp.exp(m - m_new)
        p = jnp.exp(s - m_new)
        l_new = alpha * l + jnp.sum(p, axis=-1, keepdims=True)
        acc_new = alpha * acc + _dot(p.astype(BF16), vblk)
        return m_new, l_new, acc_new

    def tiled(b):
        return jnp.concatenate([b] * n_maps, axis=0) if n_maps > 1 else b

    far = far_ref[h] if has_bias else None
    n_far = jnp.maximum(qi - 1, 0) if has_bias else qi

    def far_body(ki, carry):
        off = pl.multiple_of(ki * tq, tq)
        return step(k_ref[pl.ds(off, tq), :], v_ref[pl.ds(off, tq), :], *carry, far)

    carry = (jnp.full((rows, 1), -jnp.inf, F32), jnp.zeros((rows, 1), F32), jnp.zeros((rows, dv), F32))
    carry = lax.fori_loop(0, n_far, far_body, carry)
    if has_bias:
        prev = jnp.maximum(qi - 1, 0)
        off = pl.multiple_of(prev * tq, tq)
        near = step(k_ref[pl.ds(off, tq), :], v_ref[pl.ds(off, tq), :], *carry, tiled(bias_ref[0, 1]))
        use = qi > 0
        carry = tuple(jnp.where(use, a, b) for a, b in zip(near, carry))
    off = pl.multiple_of(qi * tq, tq)
    row = lax.broadcasted_iota(jnp.int32, (tq, tq), 0)
    col = lax.broadcasted_iota(jnp.int32, (tq, tq), 1)
    diag = jnp.where(row >= col, bias_ref[0, 0] if has_bias else 0.0, -jnp.inf)
    m, l, acc = step(k_ref[pl.ds(off, tq), :], v_ref[pl.ds(off, tq), :], *carry, tiled(diag))
    o_ref[...] = finalize(acc / l, [r[...] for r in fin_refs]).astype(o_ref.dtype)


def flash_attention(q, k, v, bsz, seq, n_heads, dq, n_maps=1, bias=None, far=None, fin=(), finalize=None,
                    tq=512):
    t = bsz * seq
    dv = v.shape[1] // n_heads
    tq = _tile(seq, tq)
    nq = seq // tq
    has_bias = bias is not None
    if finalize is None:
        finalize = lambda o, extras: o
    in_specs = [pl.BlockSpec((n_maps, tq, dq), lambda b, h, i: (0, b * nq + i, h)),
                pl.BlockSpec((seq, dq), lambda b, h, i: (b, h)),
                pl.BlockSpec((seq, dv), lambda b, h, i: (b, h))]
    args = [q, k, v]
    if has_bias:
        in_specs += [pl.BlockSpec((1, 2, tq, tq), lambda b, h, i: (h, 0, 0, 0)),
                     pl.BlockSpec(memory_space=pltpu.SMEM)]
        args += [bias, far]
    for f in fin:
        in_specs.append(pl.BlockSpec(f.shape, lambda b, h, i: (0, 0)))
        args.append(f)
    return pl.pallas_call(
        functools.partial(_flash_kernel, tq=tq, n_maps=n_maps, has_bias=has_bias, finalize=finalize),
        grid=(bsz, n_heads, nq),
        in_specs=in_specs,
        out_specs=pl.BlockSpec((tq, dv), lambda b, h, i: (b * nq + i, h)),
        out_shape=jax.ShapeDtypeStruct((t, n_heads * dv), BF16),
        compiler_params=_cparams(("parallel", "parallel", "arbitrary")),
        name="flash_bias" if has_bias else "flash",
    )(*args)


def _mla_proj_kernel(cq_ref, ckv_ref, kpe_ref, cos_ref, sin_ref, qng_ref, kvng_ref, wq_ref, wkv_ref,
                     qg_ref, kg_ref, q_ref, k_ref, v_ref, *, n_heads, scale):
    half = LANES // 2

    def norm(x, g):
        return (x * lax.rsqrt(jnp.mean(x * x, axis=-1, keepdims=True) + RMS_EPS) * g).astype(BF16)

    cq = norm(cq_ref[...], qng_ref[...])
    ckv = norm(ckv_ref[...], kvng_ref[...])
    cos, sin = cos_ref[...], sin_ref[...]
    kpe = kpe_ref[...]
    kpe_ss = jnp.sum(jnp.where(lax.broadcasted_iota(jnp.int32, kpe.shape, 1) < half, kpe * kpe, 0.0),
                     axis=-1, keepdims=True)
    n_qk = float(MLA_NOPE + MLA_ROPE)
    qg, kg = qg_ref[...], kg_ref[...]

    def rotary(hi, gains, inv):
        return inv * (hi * (gains[1:2] * cos) + pltpu.roll(hi, half, axis=1) * (gains[2:3] * sin))

    k_rot_raw = kpe * (kg[1:2] * cos) + pltpu.roll(kpe, half, axis=1) * (kg[2:3] * sin)
    for h in range(n_heads):
        xq = _dot(cq, wq_ref[:, h * MLA_PAD:(h + 1) * MLA_PAD])
        lo, hi = xq[:, :LANES], xq[:, LANES:]
        ss = jnp.sum(lo * lo, axis=-1, keepdims=True) + 0.5 * jnp.sum(hi * hi, axis=-1, keepdims=True)
        inv = lax.rsqrt(ss / n_qk + RMS_EPS) * scale
        q_ref[:, h * MLA_PAD:h * MLA_PAD + LANES] = (lo * inv * qg[0:1]).astype(BF16)
        q_ref[:, h * MLA_PAD + LANES:(h + 1) * MLA_PAD] = rotary(hi, qg, inv).astype(BF16)
        xkv = _dot(ckv, wkv_ref[:, h * MLA_PAD:(h + 1) * MLA_PAD])
        kn = xkv[:, :LANES]
        inv_k = lax.rsqrt((jnp.sum(kn * kn, axis=-1, keepdims=True) + kpe_ss) / n_qk + RMS_EPS)
        k_ref[:, h * MLA_PAD:h * MLA_PAD + LANES] = (kn * inv_k * kg[0:1]).astype(BF16)
        k_ref[:, h * MLA_PAD + LANES:(h + 1) * MLA_PAD] = (k_rot_raw * inv_k).astype(BF16)
        v_ref[:, h * LANES:(h + 1) * LANES] = xkv[:, LANES:].astype(BF16)


def _rope_gain_rows(gain):
    rot = gain[MLA_NOPE:]
    r = MLA_ROPE // 2
    swapped = jnp.concatenate([rot[r:], rot[:r]])
    z = jnp.zeros((LANES - MLA_ROPE,), F32)
    return jnp.stack([gain[:MLA_NOPE], jnp.concatenate([rot, z]), jnp.concatenate([swapped, z])])


def mla_mixer(proj, cols, bsz, seq, q_norm_g, w_q_up, kv_norm_g, w_kv_up, q_gain, k_gain):
    t = bsz * seq
    hds = MLA_N_HEADS
    q_rank, kv_rank = w_q_up.shape[0], w_kv_up.shape[0]
    r = MLA_ROPE // 2
    wq = w_q_up.reshape(q_rank, hds, MLA_NOPE + MLA_ROPE)
    wq = jnp.concatenate([wq, wq[:, :, MLA_NOPE + r:], wq[:, :, MLA_NOPE:MLA_NOPE + r]], axis=-1)
    wq = wq.reshape(q_rank, hds * MLA_PAD).astype(BF16)
    wkv = w_kv_up.astype(BF16)
    inv = 1.0 / (ROPE_THETA ** (jnp.arange(0, MLA_ROPE, 2, dtype=F32) / MLA_ROPE))
    ang = jnp.arange(seq, dtype=F32)[:, None] * inv[None, :]
    zer = jnp.zeros((seq, LANES - MLA_ROPE), F32)
    cos_t = jnp.concatenate([jnp.cos(ang), jnp.cos(ang), zer], axis=-1)
    sin_t = jnp.concatenate([-jnp.sin(ang), jnp.sin(ang), zer], axis=-1)
    tm = _tile(seq, 512)
    npos = seq // tm
    const = lambda i: (0, 0)
    q, k, v = pl.pallas_call(
        functools.partial(_mla_proj_kernel, n_heads=hds, scale=float(MLA_NOPE + MLA_ROPE) ** -0.5),
        grid=(t // tm,),
        in_specs=[pl.BlockSpec((tm, q_rank), lambda i: (i, cols["cq"] // q_rank)),
                  pl.BlockSpec((tm, kv_rank), lambda i: (i, cols["ckv"] // kv_rank)),
                  pl.BlockSpec((tm, LANES), lambda i: (i, cols["kpe"] // LANES)),
                  pl.BlockSpec((tm, LANES), lambda i: (i % npos, 0)),
                  pl.BlockSpec((tm, LANES), lambda i: (i % npos, 0)),
                  pl.BlockSpec((1, q_rank), const), pl.BlockSpec((1, kv_rank), const),
                  pl.BlockSpec((q_rank, hds * MLA_PAD), const), pl.BlockSpec((kv_rank, hds * MLA_PAD), const),
                  pl.BlockSpec((3, LANES), const), pl.BlockSpec((3, LANES), const)],
        out_specs=[pl.BlockSpec((tm, hds * MLA_PAD), lambda i: (i, 0)),
                   pl.BlockSpec((tm, hds * MLA_PAD), lambda i: (i, 0)),
                   pl.BlockSpec((tm, hds * LANES), lambda i: (i, 0))],
        out_shape=[jax.ShapeDtypeStruct((t, hds * MLA_PAD), BF16), jax.ShapeDtypeStruct((t, hds * MLA_PAD), BF16),
                   jax.ShapeDtypeStruct((t, hds * LANES), BF16)],
        compiler_params=_cparams(("parallel",)),
        name="mla_proj",
    )(proj, proj, proj, cos_t, sin_t, q_norm_g.reshape(1, q_rank), kv_norm_g.reshape(1, kv_rank), wq, wkv,
      _rope_gain_rows(q_gain), _rope_gain_rows(k_gain))
    return flash_attention(q.reshape(1, t, hds * MLA_PAD), k, v, bsz, seq, hds, MLA_PAD)


def _diff_prep_kernel(q_ref, k_ref, v_ref, qg_ref, kg_ref, seg_ref, qo_ref, ko_ref, vo_ref, *, hd, scale):
    seg = seg_ref[...]
    lane = lax.broadcasted_iota(jnp.int32, (1, LANES), 1)
    n_col = q_ref.shape[1] // LANES
    for c in range(n_col):
        sl = slice(c * LANES, (c + 1) * LANES)
        xq, xk = q_ref[:, sl], k_ref[:, sl]
        ssq = _dot_exact_rhs01(xq * xq, seg)
        ssk = _dot_exact_rhs01(xk * xk, seg)
        qn = xq * lax.rsqrt(ssq / hd + RMS_EPS) * (qg_ref[...] * scale)
        kn = xk * lax.rsqrt(ssk / hd + RMS_EPS) * kg_ref[...]
        qo_ref[0, :, sl] = jnp.where(lane < hd, qn, 0.0).astype(BF16)
        qo_ref[1, :, sl] = jnp.where(lane >= hd, qn, 0.0).astype(BF16)
        ko_ref[:, sl] = kn.astype(BF16)
    vo_ref[...] = v_ref[...].astype(BF16)


def _t5_bucket_np(n):
    max_exact = REL_BUCKETS // 2
    nf = np.maximum(n, 1).astype(np.float32)
    large = max_exact + (np.log(nf / np.float32(max_exact)) / np.float32(math.log(REL_MAX_DIST / max_exact))
                         * np.float32(REL_BUCKETS - max_exact)).astype(np.int32)
    large = np.minimum(large, REL_BUCKETS - 1)
    return np.where(n < max_exact, n, large).astype(np.int32)


def _bias_tiles_kernel(bkt_ref, rb_ref, o_ref):
    h = pl.program_id(0)
    for d in range(2):
        bkt = bkt_ref[d]
        tile = jnp.zeros(bkt.shape, F32)
        for b in range(REL_BUCKETS):
            tile = jnp.where(bkt == b, rb_ref[b, h], tile)
        o_ref[0, d] = tile


def _diff_finalize(lambda_init, tq):
    def fin(o, extras):
        lq1, lk1, lq2, lk2, sub_g = extras
        lam = (jnp.exp(jnp.sum(lq1 * lk1, axis=-1, keepdims=True))
               - jnp.exp(jnp.sum(lq2 * lk2, axis=-1, keepdims=True)) + lambda_init)
        d = o[:tq] - lam * o[tq:]
        ms = jnp.mean(d * d, axis=-1, keepdims=True)
        return d * lax.rsqrt(ms + RMS_EPS) * (sub_g * (1.0 - lambda_init))
    return fin


def diff_mixer(proj, cols, bsz, seq, q_gain, k_gain, lq1, lk1, lq2, lk2, subln_g, rel_bias, lambda_init):
    t = bsz * seq
    hds = DIFF_N_HEADS
    hd = q_gain.shape[0]
    width = hds * 2 * hd
    tm = _tile(t, 512)
    seg = (np.arange(LANES)[:, None] // hd == np.arange(LANES)[None, :] // hd)
    seg = jnp.asarray(seg, BF16)
    gq, gk = jnp.tile(q_gain, LANES // hd).reshape(1, LANES), jnp.tile(k_gain, LANES // hd).reshape(1, LANES)
    const = lambda i: (0, 0)
    qd, kd, vd = pl.pallas_call(
        functools.partial(_diff_prep_kernel, hd=hd, scale=float(hd) ** -0.5),
        grid=(t // tm,),
        in_specs=[pl.BlockSpec((tm, width), lambda i: (i, cols["dq"] // width)),
                  pl.BlockSpec((tm, width), lambda i: (i, cols["dk"] // width)),
                  pl.BlockSpec((tm, width), lambda i: (i, cols["dv"] // width)),
                  pl.BlockSpec((1, LANES), const), pl.BlockSpec((1, LANES), const),
                  pl.BlockSpec((LANES, LANES), const)],
        out_specs=[pl.BlockSpec((2, tm, width), lambda i: (0, i, 0)),
                   pl.BlockSpec((tm, width), lambda i: (i, 0)), pl.BlockSpec((tm, width), lambda i: (i, 0))],
        out_shape=[jax.ShapeDtypeStruct((2, t, width), BF16), jax.ShapeDtypeStruct((t, width), BF16),
                   jax.ShapeDtypeStruct((t, width), BF16)],
        compiler_params=_cparams(("parallel",)),
        name="diff_prep",
    )(proj, proj, proj, gq, gk, seg)

    tq = _tile(seq, 512)
    assert tq + 1 > REL_MAX_DIST, "blocks two or more before the diagonal must all be at the far distance"
    idx = np.arange(tq)
    dist0 = np.maximum(idx[:, None] - idx[None, :], 0)
    dist1 = idx[:, None] - idx[None, :] + tq
    bkt = jnp.asarray(np.stack([_t5_bucket_np(dist0), _t5_bucket_np(dist1)]))
    bias = pl.pallas_call(
        _bias_tiles_kernel,
        grid=(hds,),
        in_specs=[pl.BlockSpec((2, tq, tq), lambda h: (0, 0, 0)), pl.BlockSpec(memory_space=pltpu.SMEM)],
        out_specs=pl.BlockSpec((1, 2, tq, tq), lambda h: (h, 0, 0, 0)),
        out_shape=jax.ShapeDtypeStruct((hds, 2, tq, tq), F32),
        compiler_params=_cparams(("parallel",)),
        name="t5_bias_tiles",
    )(bkt, rel_bias)
    far = rel_bias[REL_BUCKETS - 1]
    fin = tuple(a.reshape(1, -1) for a in (lq1, lk1, lq2, lk2, subln_g))
    return flash_attention(qd, kd, vd, bsz, seq, hds, 2 * hd, n_maps=2, bias=bias, far=far, fin=fin,
                           finalize=_diff_finalize(lambda_init, tq), tq=tq)


def _pack_w_in(w_in_l, widths):
    (z_w, xbc_w, dt_w, u_w, cq_w, ckv_w, kpe_w, dq_w, dk_w, dv_w) = widths
    d = w_in_l.shape[0]
    starts = np.cumsum([0] + list(widths))
    seg = {n: w_in_l[:, starts[i]:starts[i + 1]]
           for i, n in enumerate(("z", "xbc", "dt", "u", "cq", "ckv", "kpe", "dq", "dk", "dv"))}
    r = kpe_w // 2
    kpe2 = jnp.concatenate([seg["kpe"], seg["kpe"][:, r:], seg["kpe"][:, :r],
                            jnp.zeros((d, LANES - 2 * kpe_w), F32)], axis=1)
    dtp = jnp.concatenate([seg["dt"], jnp.zeros((d, LANES - dt_w), F32)], axis=1)
    order = [("xbc", seg["xbc"]), ("z", seg["z"]), ("u", seg["u"]), ("dq", seg["dq"]), ("dk", seg["dk"]),
             ("dv", seg["dv"]), ("ckv", seg["ckv"]), ("cq", seg["cq"]), ("kpe", kpe2), ("dt", dtp)]
    cols, off = {}, 0
    for name, w in order:
        assert off % w.shape[1] == 0, (name, off, w.shape[1])
        cols[name] = off
        off += w.shape[1]
    return jnp.concatenate([w for _, w in order], axis=1).astype(BF16), cols


def kernel(x, norm_mix_g, w_in, ssd_conv_w, ssd_conv_b, ssd_dt_bias, ssd_a_log, ssd_d, ssd_norm_g, s5_a_re, s5_a_im, s5_b_re, s5_b_im, s5_c_re, s5_c_im, s5_log_dt, s5_d, s5_w_glu, s5_b_glu, mla_q_norm_g, mla_w_q_up, mla_kv_norm_g, mla_w_kv_up, mla_q_gain, mla_k_gain, diff_q_gain, diff_k_gain, diff_lq1, diff_lk1, diff_lq2, diff_lk2, diff_subln_g, rel_bias, w_gate, b_gate, w_branch, w_out, norm_ffn_g, ffn_w_gate, ffn_w_up, ffn_w_down, moe_w_router, moe_b_router, moe_w_gate, moe_w_up, moe_w_down):
    bsz, seq, d = x.shape
    t = bsz * seq
    depth = w_in.shape[0]
    mix = d // 4
    n_ssd_heads = ssd_dt_bias.shape[1]
    widths = (mix, ssd_conv_w.shape[2], n_ssd_heads, mix, mla_w_q_up.shape[1], mla_w_kv_up.shape[1], MLA_ROPE,
              mix, mix, mix)
    xf = x.reshape(t, d)
    for i in range(depth):
        h = rmsnorm(xf, norm_mix_g[i])
        w_packed, cols = _pack_w_in(w_in[i], widths)
        proj = matmul(h, [w_packed], _epi_plain, out_dtype=F32, tn=512, name="in_proj")
        dt_t = jnp.swapaxes(proj[:, cols["dt"]:cols["dt"] + n_ssd_heads].reshape(bsz, seq, n_ssd_heads), 1, 2)
        y_a = ssd_mixer(proj, dt_t, cols, bsz, seq, ssd_conv_w[i], ssd_conv_b[i], ssd_dt_bias[i], ssd_a_log[i],
                        ssd_d[i], ssd_norm_g[i])
        y_b = s5_mixer(proj, cols["u"], bsz, seq, s5_a_re[i], s5_a_im[i], s5_b_re[i], s5_b_im[i], s5_c_re[i],
                       s5_c_im[i], s5_log_dt[i], s5_d[i], s5_w_glu[i], s5_b_glu[i])
        y_c = mla_mixer(proj, cols, bsz, seq, mla_q_norm_g[i], mla_w_q_up[i], mla_kv_norm_g[i], mla_w_kv_up[i],
                        mla_q_gain[i], mla_k_gain[i])
        lambda_init = 0.8 - 0.6 * math.exp(-0.3 * i)
        y_d = diff_mixer(proj, cols, bsz, seq, diff_q_gain[i], diff_k_gain[i], diff_lq1[i], diff_lk1[i],
                         diff_lq2[i], diff_lk2[i], diff_subln_g[i], rel_bias, lambda_init)
        merged = gated_merge(h, w_gate[i].astype(BF16), b_gate[i][:, None, :], jnp.stack([y_a, y_b, y_c, y_d]),
                             w_branch[i].astype(BF16))
        xf = matmul(merged, [w_out[i].astype(BF16)], _epi_residual, extras=(xf,), extra_kinds=("mn",),
                    tk=2048, name="out_proj")
        e = i // 2
        if i % 2 == 0:
            h2 = rmsnorm(xf, norm_ffn_g[i])
            act = matmul(h2, [ffn_w_gate[e].astype(BF16), ffn_w_up[e].astype(BF16)], _epi_swiglu,
                         out_dtype=BF16, tn=512, name="ffn_up")
            xf = matmul(act, [ffn_w_down[e].astype(BF16)], _epi_residual, extras=(xf,), extra_kinds=("mn",),
                        tk=3584, name="ffn_down")
        else:
            h2, comb = rmsnorm_router(xf, norm_ffn_g[i], moe_w_router[e], moe_b_router[e])
            for ex in range(moe_w_gate.shape[1]):
                act = matmul(h2, [moe_w_gate[e, ex].astype(BF16), moe_w_up[e, ex].astype(BF16)],
                             _epi_swiglu_scaled(ex), extras=(comb,), extra_kinds=("m",), out_dtype=BF16,
                             tn=512, name="moe_up")
                xf = matmul(act, [moe_w_down[e, ex].astype(BF16)], _epi_residual, extras=(xf,),
                            extra_kinds=("mn",), tk=3584, name="moe_down")
    return xf.reshape(bsz, seq, d)
```

```python
import functools
import math

import numpy as np
import jax
import jax.numpy as jnp
from jax import lax
from jax.experimental import pallas as pl
from jax.experimental.pallas import tpu as pltpu

F32 = jnp.float32
BF16 = jnp.bfloat16

V7X_VMEM_BYTES = 64 * 1024 * 1024
VMEM_LIMIT = V7X_VMEM_BYTES - 8 * 1024 * 1024
LANES = 128

RMS_EPS = 1e-6
SSD_HEAD_DIM = 64
SSD_N_GROUPS = 4
SSD_D_STATE = 128
SSD_CHUNK = 128
S5_GROUP = 16
S5_STATE = 64
S5_GROUPS_PER_BLOCK = 8
MLA_N_HEADS = 8
MLA_NOPE = 128
MLA_ROPE = 64
MLA_PAD = 256
ROPE_THETA = 10000.0
DIFF_N_HEADS = 8
REL_BUCKETS = 32
REL_MAX_DIST = 128
TOP_K = 2
N_BRANCH = 4
LOG2E = math.log2(math.e)


def _cparams(sem):
    return pltpu.CompilerParams(dimension_semantics=sem, vmem_limit_bytes=VMEM_LIMIT)


def _tile(n, pref):
    t = min(n, pref)
    while n % t:
        t //= 2
    return t


def _split3(a):
    hi = a.astype(BF16)
    r1 = a - hi.astype(F32)
    mid = r1.astype(BF16)
    lo = (r1 - mid.astype(F32)).astype(BF16)
    return hi, mid, lo


def _dot(a, b):
    return jnp.dot(a, b, preferred_element_type=F32)


def _dot_exact_rhs01(a, sel):
    hi, mid, lo = _split3(a)
    return _dot(hi, sel) + _dot(mid, sel) + _dot(lo, sel)


def _dot_exact_lhs01(sel, b):
    hi, mid, lo = _split3(b)
    return _dot(sel, hi) + _dot(sel, mid) + _dot(sel, lo)


def _rmsnorm_kernel(x_ref, g_ref, o_ref):
    x = x_ref[...]
    ms = jnp.mean(x * x, axis=-1, keepdims=True)
    o_ref[...] = (x * lax.rsqrt(ms + RMS_EPS) * g_ref[...]).astype(o_ref.dtype)


def rmsnorm(x, g, tm=256):
    t, d = x.shape
    tm = _tile(t, tm)
    return pl.pallas_call(
        _rmsnorm_kernel,
        grid=(t // tm,),
        in_specs=[pl.BlockSpec((tm, d), lambda i: (i, 0)), pl.BlockSpec((1, d), lambda i: (0, 0))],
        out_specs=pl.BlockSpec((tm, d), lambda i: (i, 0)),
        out_shape=jax.ShapeDtypeStruct((t, d), BF16),
        compiler_params=_cparams(("parallel",)),
        name="rmsnorm",
    )(x, g.reshape(1, d))


def _rmsnorm_router_kernel(x_ref, g_ref, wr_ref, br_ref, hp_ref, route_ref, cnt_ref, run_ref, *, n_experts):
    @pl.when(pl.program_id(0) == 0)
    def _():
        run_ref[...] = jnp.zeros(run_ref.shape, F32)

    x = x_ref[...]
    tm, d = x.shape
    ms = jnp.mean(x * x, axis=-1, keepdims=True)
    h = x * lax.rsqrt(ms + RMS_EPS) * g_ref[...]
    bits = lax.bitcast_convert_type(h.astype(BF16).astype(F32), jnp.uint32)
    hp_ref[...] = (bits[:, :d // 2] >> 16) | (bits[:, d // 2:] & jnp.uint32(0xFFFF0000))
    logits = jnp.dot(h, wr_ref[...], preferred_element_type=F32, precision=lax.Precision.HIGHEST) + br_ref[...]
    lane = lax.broadcasted_iota(jnp.int32, logits.shape, 1)
    neg = jnp.float32(-jnp.inf)
    logits = jnp.where(lane < n_experts, logits, neg)
    m1 = jnp.max(logits, axis=-1, keepdims=True)
    i1 = jnp.min(jnp.where(logits == m1, lane, LANES), axis=-1, keepdims=True)
    rest = jnp.where(lane == i1, neg, logits)
    m2 = jnp.max(rest, axis=-1, keepdims=True)
    i2 = jnp.min(jnp.where(rest == m2, lane, LANES), axis=-1, keepdims=True)
    e2 = jnp.exp(m2 - m1)
    w1 = 1.0 / (1.0 + e2)
    w2 = e2 / (1.0 + e2)
    sel = jnp.where(lane == i1, 1.0, 0.0) + jnp.where(lane == i2, 1.0, 0.0)
    row = lax.broadcasted_iota(jnp.int32, (tm, tm), 0)
    col = lax.broadcasted_iota(jnp.int32, (tm, tm), 1)
    before = jnp.where(col < row, 1.0, 0.0).astype(BF16)
    excl = _dot(before, sel.astype(BF16)) + run_ref[0:1, :]
    run_ref[...] = run_ref[...] + jnp.sum(sel, axis=0, keepdims=True)
    cnt_ref[...] = run_ref[...]
    r1 = jnp.sum(jnp.where(lane == i1, excl, 0.0), axis=-1, keepdims=True)
    r2 = jnp.sum(jnp.where(lane == i2, excl, 0.0), axis=-1, keepdims=True)
    route = jnp.zeros(logits.shape, F32)
    for pos, val in enumerate((i1.astype(F32), i2.astype(F32), r1, r2, w1, w2)):
        route = jnp.where(lane == pos, val, route)
    route_ref[...] = route


def rmsnorm_router(x, g, w_router, b_router, tm=256):
    t, d = x.shape
    ne = w_router.shape[1]
    tm = _tile(t, tm)
    wr = jnp.zeros((d, LANES), F32).at[:, :ne].set(w_router)
    br = jnp.zeros((1, LANES), F32).at[0, :ne].set(b_router)
    return pl.pallas_call(
        functools.partial(_rmsnorm_router_kernel, n_experts=ne),
        grid=(t // tm,),
        in_specs=[pl.BlockSpec((tm, d), lambda i: (i, 0)), pl.BlockSpec((1, d), lambda i: (0, 0)),
                  pl.BlockSpec((d, LANES), lambda i: (0, 0)), pl.BlockSpec((1, LANES), lambda i: (0, 0))],
        out_specs=[pl.BlockSpec((tm, d // 2), lambda i: (i, 0)), pl.BlockSpec((tm, LANES), lambda i: (i, 0)),
                   pl.BlockSpec((8, LANES), lambda i: (0, 0))],
        out_shape=[jax.ShapeDtypeStruct((t, d // 2), jnp.uint32), jax.ShapeDtypeStruct((t, LANES), F32),
                   jax.ShapeDtypeStruct((8, LANES), F32)],
        scratch_shapes=[pltpu.VMEM((8, LANES), F32)],
        compiler_params=_cparams(("arbitrary",)),
        name="rmsnorm_router",
    )(x, g.reshape(1, d), wr, br)


MOE_ROW_TILE = 512
MOE_TOKEN_BLOCK = 256


def _dispatch_kernel(dest_ref, h_hbm, xs_in_hbm, xs_hbm, sem, *, tb):
    del xs_in_hbm
    base = pl.program_id(0) * tb

    def start(t, c):
        src = h_hbm.at[pl.ds(base + t, 1)]
        for k in range(TOP_K):
            pltpu.make_async_copy(src, xs_hbm.at[pl.ds(dest_ref[0, 0, TOP_K * t + k], 1)], sem).start()
        return c

    lax.fori_loop(0, tb, start, 0)

    def wait(t, c):
        pltpu.make_async_copy(h_hbm.at[pl.ds(0, 1)], xs_hbm.at[pl.ds(0, 1)], sem).wait()
        return c

    lax.fori_loop(0, TOP_K * tb, wait, 0)


def moe_dispatch(hp, dest, n_rows):
    t, dw = hp.shape
    tb = _tile(t, MOE_TOKEN_BLOCK)
    return pl.pallas_call(
        functools.partial(_dispatch_kernel, tb=tb),
        grid=(t // tb,),
        in_specs=[pl.BlockSpec((1, 1, TOP_K * tb), lambda i: (i, 0, 0), memory_space=pltpu.SMEM),
                  pl.BlockSpec(memory_space=pl.ANY), pl.BlockSpec(memory_space=pl.ANY)],
        out_specs=pl.BlockSpec(memory_space=pl.ANY),
        out_shape=jax.ShapeDtypeStruct((n_rows, dw), hp.dtype),
        scratch_shapes=[pltpu.SemaphoreType.DMA(())],
        input_output_aliases={2: 0},
        compiler_params=_cparams(("arbitrary",)),
        name="moe_dispatch",
    )(dest.reshape(t // tb, 1, TOP_K * tb), hp, jnp.zeros((n_rows, dw), hp.dtype))


def _unpack_halves(xp):
    lo = lax.bitcast_convert_type(xp << 16, F32).astype(BF16)
    hi = lax.bitcast_convert_type(xp & jnp.uint32(0xFFFF0000), F32).astype(BF16)
    return lo, hi


def _grouped_kernel(e_tab, jw_tab, rx_tab, j_tab, r_tab, first_tab, valid_tab, x_ref, *rest, n_w, packed,
                    epilogue):
    del e_tab, jw_tab, rx_tab, j_tab, r_tab
    w_refs, o_ref, wb_refs = rest[:n_w], rest[n_w], rest[n_w + 1:]
    s = pl.program_id(0)

    @pl.when(first_tab[s] == 1)
    def _():
        for w, wb in zip(w_refs, wb_refs):
            wb[...] = w[0].astype(BF16)

    @pl.when(valid_tab[s] == 1)
    def _():
        if packed:
            lo, hi = _unpack_halves(x_ref[...])
            half = lo.shape[1]
            parts = [_dot(lo, wb[:half, :]) + _dot(hi, wb[half:, :]) for wb in wb_refs]
        else:
            x = x_ref[...]
            parts = [_dot(x, wb[...]) for wb in wb_refs]
        o_ref[...] = epilogue(parts, []).astype(o_ref.dtype)

    @pl.when(valid_tab[s] == 0)
    def _():
        o_ref[...] = jnp.zeros(o_ref.shape, o_ref.dtype)


def grouped_matmul(x, ws, tabs, epilogue, out_dtype, tn, packed, name):
    n_rows, kx = x.shape
    ne, kdim, n = ws[0].shape
    tmg = MOE_ROW_TILE
    tn = _tile(n, tn)
    n_steps = tabs[0].shape[0]
    grid_spec = pltpu.PrefetchScalarGridSpec(
        num_scalar_prefetch=7,
        grid=(n_steps,),
        in_specs=[pl.BlockSpec((tmg, kx), lambda s, e, jw, rx, j, r, f, v: (rx[s], 0))]
        + [pl.BlockSpec((1, kdim, tn), lambda s, e, jw, rx, j, r, f, v: (e[s], 0, jw[s])) for _ in ws],
        out_specs=pl.BlockSpec((tmg, tn), lambda s, e, jw, rx, j, r, f, v: (r[s], j[s])),
        scratch_shapes=[pltpu.VMEM((kdim, tn), BF16) for _ in ws],
    )
    return pl.pallas_call(
        functools.partial(_grouped_kernel, n_w=len(ws), packed=packed, epilogue=epilogue),
        grid_spec=grid_spec,
        out_shape=jax.ShapeDtypeStruct((n_rows, n), out_dtype),
        compiler_params=_cparams(("arbitrary",)),
        name=name,
    )(*tabs, x, *ws)


def _group_steps(padded, nj, n_tiles):
    tmg = MOE_ROW_TILE
    tiles = padded // tmg
    tile_start = jnp.cumsum(tiles) - tiles
    steps = tiles * nj
    step_end = jnp.cumsum(steps)
    step_start = step_end - steps
    total = step_end[-1]
    used_tiles = jnp.sum(tiles)
    s = jnp.arange(n_tiles * nj, dtype=jnp.int32)
    sc = jnp.minimum(s, total - 1)
    e = jnp.minimum(jnp.searchsorted(step_end, sc, side="right"), tiles.shape[0] - 1).astype(jnp.int32)
    local = sc - step_start[e]
    te = jnp.maximum(tiles[e], 1)
    jw = local // te
    rl = local % te
    rx = tile_start[e] + rl
    valid = s < total
    first = valid & (rl == 0)
    extra = jnp.maximum(s - total, 0)
    j = jnp.where(valid, jw, extra % nj)
    r = jnp.where(valid, rx, used_tiles + extra // nj)
    i32 = lambda a: a.astype(jnp.int32)
    return i32(e), i32(jw), i32(rx), i32(j), i32(r), i32(first), i32(valid)


def _combine_kernel(dest_ref, x_ref, route_ref, ys_hbm, o_ref, buf, sem, *, tb):
    def start(t, c):
        for k in range(TOP_K):
            pltpu.make_async_copy(ys_hbm.at[pl.ds(dest_ref[0, 0, TOP_K * t + k], 1)], buf.at[k, pl.ds(t, 1)],
                                  sem).start()
        return c

    lax.fori_loop(0, tb, start, 0)

    def wait(t, c):
        pltpu.make_async_copy(ys_hbm.at[pl.ds(0, 1)], buf.at[0, pl.ds(0, 1)], sem).wait()
        return c

    lax.fori_loop(0, TOP_K * tb, wait, 0)
    route = route_ref[...]
    o_ref[...] = x_ref[...] + route[:, 4:5] * buf[0] + route[:, 5:6] * buf[1]


def moe_combine(x, route, ys, dest):
    t, d = x.shape
    tb = _tile(t, MOE_TOKEN_BLOCK)
    return pl.pallas_call(
        functools.partial(_combine_kernel, tb=tb),
        grid=(t // tb,),
        in_specs=[pl.BlockSpec((1, 1, TOP_K * tb), lambda i: (i, 0, 0), memory_space=pltpu.SMEM),
                  pl.BlockSpec((tb, d), lambda i: (i, 0)), pl.BlockSpec((tb, LANES), lambda i: (i, 0)),
                  pl.BlockSpec(memory_space=pl.ANY)],
        out_specs=pl.BlockSpec((tb, d), lambda i: (i, 0)),
        out_shape=jax.ShapeDtypeStruct((t, d), F32),
        scratch_shapes=[pltpu.VMEM((TOP_K, tb, d), F32), pltpu.SemaphoreType.DMA(())],
        compiler_params=_cparams(("arbitrary",)),
        name="moe_combine",
    )(dest.reshape(t // tb, 1, TOP_K * tb), x, route, ys)


def moe_block(x, norm_g, w_router, b_router, w_gate, w_up, w_down):
    t, d = x.shape
    ne, _, f = w_gate.shape
    tmg = MOE_ROW_TILE
    hp, route, cnt = rmsnorm_router(x, norm_g, w_router, b_router)
    counts = cnt[0, :ne].astype(jnp.int32)
    padded = (counts + tmg - 1) // tmg * tmg
    starts = jnp.cumsum(padded) - padded
    ids = route[:, 0:2].astype(jnp.int32)
    dest = starts[ids] + route[:, 2:4].astype(jnp.int32)
    n_rows = t * TOP_K + ne * tmg
    n_tiles = n_rows // tmg
    xs = moe_dispatch(hp, dest, n_rows)
    tn_up, tn_down = _tile(f, 256), _tile(d, 512)
    act = grouped_matmul(xs, [w_gate, w_up], _group_steps(padded, f // tn_up, n_tiles), _epi_swiglu, BF16,
                         tn_up, True, "moe_up")
    ys = grouped_matmul(act, [w_down], _group_steps(padded, d // tn_down, n_tiles), _epi_plain, F32,
                        tn_down, False, "moe_down")
    return moe_combine(x, route, ys, dest)


def _mm_kernel(*refs, n_w, n_e, nk, epilogue):
    a_ref = refs[0]
    w_refs = refs[1:1 + n_w]
    e_refs = refs[1 + n_w:1 + n_w + n_e]
    o_ref = refs[1 + n_w + n_e]
    acc_refs = refs[2 + n_w + n_e:]
    a = a_ref[...]
    parts = [_dot(a, w[...]) for w in w_refs]
    if nk == 1:
        o_ref[...] = epilogue(parts, [e[...] for e in e_refs]).astype(o_ref.dtype)
        return
    k = pl.program_id(2)

    @pl.when(k == 0)
    def _():
        for acc, p in zip(acc_refs, parts):
            acc[...] = p

    @pl.when(k > 0)
    def _():
        for acc, p in zip(acc_refs, parts):
            acc[...] += p

    @pl.when(k == nk - 1)
    def _():
        o_ref[...] = epilogue([acc[...] for acc in acc_refs], [e[...] for e in e_refs]).astype(o_ref.dtype)


def matmul(a, ws, epilogue, extras=(), extra_kinds=(), out_dtype=F32, tm=1024, tn=1024, tk=4096, name="matmul"):
    m, kdim = a.shape
    n = ws[0].shape[1]
    tm, tn, tk = _tile(m, tm), _tile(n, tn), _tile(kdim, tk)
    nk = kdim // tk
    in_specs = [pl.BlockSpec((tm, tk), lambda i, j, k: (i, k))]
    in_specs += [pl.BlockSpec((tk, tn), lambda i, j, k: (k, j)) for _ in ws]
    for kind in extra_kinds:
        if kind == "mn":
            in_specs.append(pl.BlockSpec((tm, tn), lambda i, j, k: (i, j)))
        elif kind == "m":
            in_specs.append(pl.BlockSpec((tm, LANES), lambda i, j, k: (i, 0)))
        else:
            in_specs.append(pl.BlockSpec((1, tn), lambda i, j, k: (0, j)))
    scratch = [pltpu.VMEM((tm, tn), F32) for _ in ws] if nk > 1 else []
    return pl.pallas_call(
        functools.partial(_mm_kernel, n_w=len(ws), n_e=len(extras), nk=nk, epilogue=epilogue),
        grid=(m // tm, n // tn, nk),
        in_specs=in_specs,
        out_specs=pl.BlockSpec((tm, tn), lambda i, j, k: (i, j)),
        out_shape=jax.ShapeDtypeStruct((m, n), out_dtype),
        scratch_shapes=scratch,
        compiler_params=_cparams(("parallel", "parallel", "arbitrary")),
        name=name,
    )(a, *ws, *extras)


def _epi_plain(parts, extras):
    return parts[0]


def _epi_residual(parts, extras):
    return extras[0] + parts[0]


def _epi_swiglu(parts, extras):
    g, u = parts
    return g * jax.nn.sigmoid(g) * u


def _epi_glu(parts, extras):
    y, b = extras
    return y * jax.nn.sigmoid(parts[0] + b)


def _merge_kernel(h_ref, wg_ref, b_ref, y_ref, wb_ref, o_ref, accm_ref, *accg, nk, nb):
    j = pl.program_id(2)
    k = pl.program_id(3)
    part = _dot(h_ref[...], wg_ref[0])

    def finish(pre):
        term = jax.nn.sigmoid(pre + b_ref[0]) * _dot(y_ref[0], wb_ref[0])

        @pl.when(j == 0)
        def _():
            accm_ref[...] = term

        @pl.when((j > 0) & (j < nb - 1))
        def _():
            accm_ref[...] += term

        @pl.when(j == nb - 1)
        def _():
            o_ref[...] = (accm_ref[...] + term).astype(o_ref.dtype)

    if nk == 1:
        finish(part)
        return

    accg_ref, = accg

    @pl.when(k == 0)
    def _():
        accg_ref[...] = part

    @pl.when(k > 0)
    def _():
        accg_ref[...] += part

    @pl.when(k == nk - 1)
    def _():
        finish(accg_ref[...])


def gated_merge(h, w_gate, b_gate, ys, w_branch, tm=1024, tn=512, tk=4096):
    t, d = h.shape
    nb, _, wdt = ys.shape
    tm, tn, tk = _tile(t, tm), _tile(d, tn), _tile(d, tk)
    nk = d // tk
    assert nb >= 2
    return pl.pallas_call(
        functools.partial(_merge_kernel, nk=nk, nb=nb),
        grid=(t // tm, d // tn, nb, nk),
        in_specs=[
            pl.BlockSpec((tm, tk), lambda i, n, j, k: (i, k)),
            pl.BlockSpec((1, tk, tn), lambda i, n, j, k: (j, k, n)),
            pl.BlockSpec((1, 1, tn), lambda i, n, j, k: (j, 0, n)),
            pl.BlockSpec((1, tm, wdt), lambda i, n, j, k: (j, i, 0)),
            pl.BlockSpec((1, wdt, tn), lambda i, n, j, k: (j, 0, n)),
        ],
        out_specs=pl.BlockSpec((tm, tn), lambda i, n, j, k: (i, n)),
        out_shape=jax.ShapeDtypeStruct((t, d), BF16),
        scratch_shapes=[pltpu.VMEM((tm, tn), F32)] * (1 if nk == 1 else 2),
        compiler_params=_cparams(("parallel", "parallel", "arbitrary", "arbitrary")),
        name="gated_merge",
    )(h, w_gate, b_gate, ys, w_branch)


def _softplus(x):
    return jnp.maximum(x, 0.0) + jnp.log1p(jnp.exp(-jnp.abs(x)))


def _ssd_kernel(xbc_ref, z_ref, dt_ref, dtt_ref, cw_ref, cb_ref, dtb_ref, dtbt_ref, alog_ref, alogt_ref,
                dskip_ref, ng_ref, expand_ref, ybuf_ref, o_ref, xe_ref, st_ref, *, n_heads, d_inner):
    del ybuf_ref
    q = SSD_CHUNK
    gn = SSD_N_GROUPS * SSD_D_STATE
    hpg = n_heads // SSD_N_GROUPS
    gw = hpg * SSD_HEAD_DIM
    c = pl.program_id(1)

    @pl.when(c == 0)
    def _():
        xe_ref[0:8, :] = jnp.zeros((8, xe_ref.shape[1]), F32)
        st_ref[...] = jnp.zeros(st_ref.shape, F32)

    xe_ref[8:8 + q, :] = xbc_ref[...]
    acc = cb_ref[...] + cw_ref[3:4, :] * xe_ref[8:8 + q, :]
    for kk in range(3):
        acc = acc + cw_ref[kk:kk + 1, :] * xe_ref[5 + kk:5 + kk + q, :]
    xe_ref[0:8, :] = xe_ref[q:q + 8, :]
    xc = acc * jax.nn.sigmoid(acc)
    xs = xc[:, :d_inner]
    bm = xc[:, d_inner:d_inner + gn].astype(BF16)
    cm = xc[:, d_inner + gn:].astype(BF16)

    dt = _softplus(dt_ref[...] + dtb_ref[...])
    da = dt * (-jnp.exp(alog_ref[...]))
    row = lax.broadcasted_iota(jnp.int32, (q, q), 0)
    col = lax.broadcasted_iota(jnp.int32, (q, q), 1)
    lower = row >= col
    tril = jnp.where(lower, 1.0, 0.0).astype(BF16)
    acs = _dot_exact_lhs01(tril, da)
    dtt = _softplus(dtt_ref[0] + dtbt_ref[...])
    dat = dtt * (-jnp.exp(alogt_ref[...]))
    triu = jnp.where(row <= col, 1.0, 0.0).astype(BF16)
    acst = _dot_exact_rhs01(dat, triu)

    expand = expand_ref[...]
    dt_x = _dot_exact_rhs01(dt, expand)
    acs_x = _dot_exact_rhs01(acs, expand)
    tot_x = acs_x[q - 1:q, :]
    xdt = xs * dt_x
    xdt_b = xdt.astype(BF16)
    xw_b = (xdt * jnp.exp(tot_x - acs_x)).astype(BF16)
    in_decay = jnp.exp(acs_x)
    chunk_decay = jnp.exp(tot_x)

    lane = lax.broadcasted_iota(jnp.int32, (1, gw), 1)
    ys = []
    for g in range(SSD_N_GROUPS):
        cg = cm[:, g * SSD_D_STATE:(g + 1) * SSD_D_STATE]
        bg = bm[:, g * SSD_D_STATE:(g + 1) * SSD_D_STATE]
        sl = slice(g * gw, (g + 1) * gw)
        gmat = lax.dot_general(cg, bg, (((1,), (1,)), ((), ())), preferred_element_type=F32)
        state = st_ref[g]
        y_g = _dot(cg, state.astype(BF16)) * in_decay[:, sl]
        xg = xdt_b[:, sl]
        for hh in range(hpg):
            hd = g * hpg + hh
            seg = acs[:, hd:hd + 1] - acst[hd:hd + 1, :]
            lmat = jnp.exp(jnp.where(lower, seg, -jnp.inf))
            mh = (gmat * lmat).astype(BF16)
            head_cols = (lane >= hh * SSD_HEAD_DIM) & (lane < (hh + 1) * SSD_HEAD_DIM)
            y_g = y_g + _dot(mh, jnp.where(head_cols, xg, jnp.zeros_like(xg)))
        upd = lax.dot_general(bg, xw_b[:, sl], (((0,), (0,)), ((), ())), preferred_element_type=F32)
        st_ref[g] = state * chunk_decay[:, sl] + upd
        ys.append(y_g)
    y = jnp.concatenate(ys, axis=-1) + xs * dskip_ref[...]
    zz = z_ref[...]
    y = y * (zz * jax.nn.sigmoid(zz))
    outs = []
    for g in range(SSD_N_GROUPS):
        yg = y[:, g * gw:(g + 1) * gw]
        ms = jnp.mean(yg * yg, axis=-1, keepdims=True)
        outs.append(yg * lax.rsqrt(ms + RMS_EPS))
    o_ref[0] = (jnp.concatenate(outs, axis=-1) * ng_ref[...]).astype(o_ref.dtype)


def ssd_mixer(proj, dt_t, cols, bsz, seq, conv_w, conv_b, dt_bias, a_log, d_skip, norm_g, ybuf, slot):
    n_heads = dt_bias.shape[0]
    d_inner = n_heads * SSD_HEAD_DIM
    conv_dim = conv_w.shape[1]
    q = SSD_CHUNK
    nc = seq // q
    pad = LANES - n_heads
    expand = jnp.repeat(jnp.eye(LANES, dtype=BF16)[:, :n_heads], SSD_HEAD_DIM, axis=1)
    dtb = jnp.pad(dt_bias, (0, pad)).reshape(1, LANES)
    alog = jnp.pad(a_log, (0, pad)).reshape(1, LANES)
    x_blk, z_blk, dt_blk = cols["xbc"] // conv_dim, cols["z"] // d_inner, cols["dt"] // LANES
    const = lambda b, c: (0, 0)
    return pl.pallas_call(
        functools.partial(_ssd_kernel, n_heads=n_heads, d_inner=d_inner),
        grid=(bsz, nc),
        in_specs=[
            pl.BlockSpec((q, conv_dim), lambda b, c: (b * nc + c, x_blk)),
            pl.BlockSpec((q, d_inner), lambda b, c: (b * nc + c, z_blk)),
            pl.BlockSpec((q, LANES), lambda b, c: (b * nc + c, dt_blk)),
            pl.BlockSpec((1, n_heads, q), lambda b, c: (b, 0, c)),
            pl.BlockSpec((4, conv_dim), const),
            pl.BlockSpec((1, conv_dim), const),
            pl.BlockSpec((1, LANES), const),
            pl.BlockSpec((n_heads, 1), const),
            pl.BlockSpec((1, LANES), const),
            pl.BlockSpec((n_heads, 1), const),
            pl.BlockSpec((1, d_inner), const),
            pl.BlockSpec((1, d_inner), const),
            pl.BlockSpec((LANES, d_inner), const),
            pl.BlockSpec(memory_space=pl.ANY),
        ],
        out_specs=pl.BlockSpec((1, q, d_inner), lambda b, c: (slot, b * nc + c, 0)),
        out_shape=jax.ShapeDtypeStruct(ybuf.shape, ybuf.dtype),
        scratch_shapes=[pltpu.VMEM((q + 8, conv_dim), F32),
                        pltpu.VMEM((SSD_N_GROUPS, SSD_D_STATE, d_inner // SSD_N_GROUPS), F32)],
        input_output_aliases={13: 0},
        compiler_params=_cparams(("parallel", "arbitrary")),
        name="ssd_mixer",
    )(proj, proj, proj, dt_t, conv_w, conv_b.reshape(1, conv_dim), dtb, dt_bias.reshape(n_heads, 1),
      alog, a_log.reshape(n_heads, 1), jnp.repeat(d_skip, SSD_HEAD_DIM).reshape(1, d_inner),
      norm_g.reshape(1, d_inner), expand, ybuf)


S5_ROWS = 8


def _s5_disc_kernel(are_ref, aim_ref, ldt_ref, brt_ref, bit_ref, pr_ref, pi_ref, bbr_ref, bbi_ref):
    dt = jnp.exp(ldt_ref[...])
    ar, ai = are_ref[...], aim_ref[...]
    mag = jnp.exp(ar * dt)
    lr, li = mag * jnp.cos(ai * dt), mag * jnp.sin(ai * dt)
    den = ar * ar + ai * ai
    fr = ((lr - 1.0) * ar + li * ai) / den
    fi = (li * ar - (lr - 1.0) * ai) / den
    pr, pi = lr, li
    for k in range(S5_ROWS):
        pr_ref[k] = pr
        pi_ref[k] = pi
        pr, pi = pr * lr - pi * li, pr * li + pi * lr
    br, bi = brt_ref[...], bit_ref[...]
    bbr_ref[...] = fr[:, None, :] * br - fi[:, None, :] * bi
    bbi_ref[...] = fr[:, None, :] * bi + fi[:, None, :] * br


def s5_discretise(a_re, a_im, log_dt, b_re, b_im):
    g, n = a_re.shape
    c = b_re.shape[2]
    brt, bit = jnp.swapaxes(b_re, 1, 2), jnp.swapaxes(b_im, 1, 2)
    return pl.pallas_call(
        _s5_disc_kernel,
        out_shape=[jax.ShapeDtypeStruct((S5_ROWS, g, n), F32), jax.ShapeDtypeStruct((S5_ROWS, g, n), F32),
                   jax.ShapeDtypeStruct((g, c, n), F32), jax.ShapeDtypeStruct((g, c, n), F32)],
        name="s5_discretise",
    )(a_re, a_im, log_dt.reshape(g, 1), brt, bit)


def _block_diag(blocks, per):
    g, r, c = blocks.shape
    nb = g // per
    eye = jnp.eye(per, dtype=blocks.dtype)
    out = blocks.reshape(nb, per, r, 1, c) * eye[None, :, None, :, None]
    return out.reshape(nb, per * r, per * c)


def _s5_kernel(u_ref, wr_ref, wi_ref, cr_ref, ci_ref, ctab_ref, d_ref, wglu_ref, bglu_ref, ybuf_ref, o_ref,
               xr_scr, xi_scr, carry_ref, y_scr, *, nblk, sw):
    del ybuf_ref
    tm = u_ref.shape[0]

    @pl.when(pl.program_id(1) == 0)
    def _():
        carry_ref[...] = jnp.zeros(carry_ref.shape, F32)

    for blk in range(nblk):
        cs = slice(blk * sw, (blk + 1) * sw)
        ls = slice(blk * LANES, (blk + 1) * LANES)
        u = u_ref[:, ls]
        ub = u.astype(BF16)
        xr_scr[...] = _dot(ub, wr_ref[blk])
        xi_scr[...] = _dot(ub, wi_ref[blk])

        def body(i, carry, cs=cs):
            c_re, c_im = carry
            r0 = pl.multiple_of(i * S5_ROWS, S5_ROWS)
            sr, si = xr_scr[pl.ds(r0, S5_ROWS), :], xi_scr[pl.ds(r0, S5_ROWS), :]
            for lvl, dist in enumerate((1, 2, 4)):
                lr, li = ctab_ref[2 * lvl, :, cs], ctab_ref[2 * lvl + 1, :, cs]
                tr, ti = pltpu.roll(sr, dist, axis=0), pltpu.roll(si, dist, axis=0)
                sr, si = sr + lr * tr - li * ti, si + lr * ti + li * tr
            pr, pi = ctab_ref[6, :, cs], ctab_ref[7, :, cs]
            sr, si = sr + pr * c_re - pi * c_im, si + pr * c_im + pi * c_re
            xr_scr[pl.ds(r0, S5_ROWS), :] = sr
            xi_scr[pl.ds(r0, S5_ROWS), :] = si
            return sr[S5_ROWS - 1:S5_ROWS, :], si[S5_ROWS - 1:S5_ROWS, :]

        c_re, c_im = lax.fori_loop(0, tm // S5_ROWS, body, (carry_ref[0:1, cs], carry_ref[1:2, cs]), unroll=2)
        carry_ref[0:1, cs] = c_re
        carry_ref[1:2, cs] = c_im
        y = _dot(xr_scr[...].astype(BF16), cr_ref[blk]) - _dot(xi_scr[...].astype(BF16), ci_ref[blk])
        y_scr[:, ls] = jax.nn.gelu(y + d_ref[:, ls] * u)
    y = y_scr[...]
    o_ref[0] = (y * jax.nn.sigmoid(_dot(y.astype(BF16), wglu_ref[...]) + bglu_ref[...])).astype(o_ref.dtype)


def s5_mixer(proj, u_col, bsz, seq, a_re, a_im, b_re, b_im, c_re, c_im, log_dt, d_skip, w_glu, b_glu,
             ybuf, slot):
    t = bsz * seq
    g, n = a_re.shape
    width = g * S5_GROUP
    per = S5_GROUPS_PER_BLOCK
    nblk = g // per
    sw = per * n
    ns = g * n
    pw_r, pw_i, bbr, bbi = s5_discretise(a_re, a_im, log_dt, b_re, b_im)
    w_r = _block_diag(bbr, per).astype(BF16)
    w_i = _block_diag(bbi, per).astype(BF16)
    c_r = _block_diag(jnp.swapaxes(c_re, 1, 2), per).astype(BF16)
    c_i = _block_diag(jnp.swapaxes(c_im, 1, 2), per).astype(BF16)
    pw_r, pw_i = pw_r.reshape(S5_ROWS, ns), pw_i.reshape(S5_ROWS, ns)
    sub = jnp.arange(S5_ROWS)[:, None]
    kinds = []
    for dist in (1, 2, 4):
        kinds += [jnp.where(sub >= dist, pw_r[dist - 1][None, :], 0.0),
                  jnp.where(sub >= dist, pw_i[dist - 1][None, :], 0.0)]
    ctab = jnp.stack(kinds + [pw_r, pw_i])
    tm = _tile(seq, 256)
    nt = seq // tm
    const2 = lambda b, c: (0, 0)
    const3 = lambda b, c: (0, 0, 0)
    return pl.pallas_call(
        functools.partial(_s5_kernel, nblk=nblk, sw=sw),
        grid=(bsz, nt),
        in_specs=[pl.BlockSpec((tm, width), lambda b, c: (b * nt + c, u_col // width)),
                  pl.BlockSpec((nblk, LANES, sw), const3), pl.BlockSpec((nblk, LANES, sw), const3),
                  pl.BlockSpec((nblk, sw, LANES), const3), pl.BlockSpec((nblk, sw, LANES), const3),
                  pl.BlockSpec((8, S5_ROWS, ns), const3),
                  pl.BlockSpec((1, width), const2), pl.BlockSpec((width, width), const2),
                  pl.BlockSpec((1, width), const2), pl.BlockSpec(memory_space=pl.ANY)],
        out_specs=pl.BlockSpec((1, tm, width), lambda b, c: (slot, b * nt + c, 0)),
        out_shape=jax.ShapeDtypeStruct(ybuf.shape, ybuf.dtype),
        scratch_shapes=[pltpu.VMEM((tm, sw), F32), pltpu.VMEM((tm, sw), F32), pltpu.VMEM((8, ns), F32),
                        pltpu.VMEM((tm, width), F32)],
        input_output_aliases={9: 0},
        compiler_params=_cparams(("parallel", "arbitrary")),
        name="s5_mixer",
    )(proj, w_r, w_i, c_r, c_i, ctab, d_skip.reshape(1, width), w_glu.astype(BF16), b_glu.reshape(1, width),
      ybuf)


def _flash_kernel(*refs, tq, n_maps, has_bias, finalize):
    if has_bias:
        q_ref, k_ref, v_ref, bias_ref, far_ref, *rest = refs
    else:
        q_ref, k_ref, v_ref, *rest = refs
    fin_refs = rest[:-2]
    o_ref = rest[-1]
    h = pl.program_id(1)
    qi = pl.program_id(2)
    rows = n_maps * tq
    dv = v_ref.shape[1]
    qv = q_ref[...].reshape(rows, q_ref.shape[-1])

    def tiled(b):
        return jnp.concatenate([b] * n_maps, axis=0) if n_maps > 1 else b

    def block(kb, carry, add, shift):
        m, l, acc = carry
        off = pl.multiple_of(kb * tq, tq)
        s = lax.dot_general(qv, k_ref[pl.ds(off, tq), :], (((1,), (1,)), ((), ())), preferred_element_type=F32)
        if add is not None:
            s = s + add
        mx = jnp.max(s, axis=-1, keepdims=True)
        m_new = jnp.maximum(m, mx if shift is None else mx + shift)
        alpha = jnp.exp2(m - m_new)
        p = jnp.exp2(s - (m_new if shift is None else m_new - shift))
        l_new = alpha * l + jnp.sum(p, axis=-1, keepdims=True)
        acc_new = alpha * acc + _dot(p.astype(BF16), v_ref[pl.ds(off, tq), :])
        return m_new, l_new, acc_new

    carry = (jnp.full((rows, 1), -jnp.inf, F32), jnp.zeros((rows, 1), F32), jnp.zeros((rows, dv), F32))
    if has_bias:
        far = far_ref[h]
        n_far = jnp.maximum(qi - 1, 0)
        carry = lax.fori_loop(0, n_far, lambda kb, c: block(kb, c, None, far), carry)
        carry = lax.fori_loop(n_far, qi, lambda kb, c: block(kb, c, tiled(bias_ref[0, 1]), None), carry)
    else:
        carry = lax.fori_loop(0, qi, lambda kb, c: block(kb, c, None, None), carry)
    row = lax.broadcasted_iota(jnp.int32, (tq, tq), 0)
    col = lax.broadcasted_iota(jnp.int32, (tq, tq), 1)
    diag = jnp.where(row >= col, bias_ref[0, 0] if has_bias else 0.0, -jnp.inf)
    m, l, acc = block(qi, carry, tiled(diag), None)
    o_ref[0] = finalize(acc / l, [r[...] for r in fin_refs]).astype(o_ref.dtype)


def flash_attention(q, k, v, bsz, seq, n_heads, dq, ybuf, slot, n_maps=1, bias=None, far=None, fin=(),
                    finalize=None, tq=512):
    dv = v.shape[1] // n_heads
    tq = _tile(seq, tq)
    nq = seq // tq
    rows = n_maps * tq
    has_bias = bias is not None
    if finalize is None:
        finalize = lambda o, extras: o
    in_specs = [pl.BlockSpec((n_maps, tq, dq), lambda b, h, i: (0, b * nq + i, h)),
                pl.BlockSpec((seq, dq), lambda b, h, i: (b, h)),
                pl.BlockSpec((seq, dv), lambda b, h, i: (b, h))]
    args = [q, k, v]
    if has_bias:
        in_specs += [pl.BlockSpec((1, 2, tq, tq), lambda b, h, i: (h, 0, 0, 0)),
                     pl.BlockSpec(memory_space=pltpu.SMEM)]
        args += [bias, far]
    for f in fin:
        in_specs.append(pl.BlockSpec(f.shape, lambda b, h, i: (0, 0)))
        args.append(f)
    in_specs.append(pl.BlockSpec(memory_space=pl.ANY))
    args.append(ybuf)
    return pl.pallas_call(
        functools.partial(_flash_kernel, tq=tq, n_maps=n_maps, has_bias=has_bias, finalize=finalize),
        grid=(bsz, n_heads, nq),
        in_specs=in_specs,
        out_specs=pl.BlockSpec((1, tq, dv), lambda b, h, i: (slot, b * nq + i, h)),
        out_shape=jax.ShapeDtypeStruct(ybuf.shape, ybuf.dtype),
        input_output_aliases={len(args) - 1: 0},
        compiler_params=_cparams(("parallel", "parallel", "arbitrary")),
        name="flash_bias" if has_bias else "flash",
    )(*args)


def _mla_proj_kernel(cq_ref, ckv_ref, kpe_ref, cos_ref, sin_ref, qng_ref, kvng_ref, wq_ref, wkv_ref,
                     qg_ref, kg_ref, q_ref, k_ref, v_ref, *, n_heads, scale):
    half = LANES // 2

    def norm(x, g):
        return (x * lax.rsqrt(jnp.mean(x * x, axis=-1, keepdims=True) + RMS_EPS) * g).astype(BF16)

    cq = norm(cq_ref[...], qng_ref[...])
    ckv = norm(ckv_ref[...], kvng_ref[...])
    cos, sin = cos_ref[...], sin_ref[...]
    kpe = kpe_ref[...]
    kpe_ss = jnp.sum(jnp.where(lax.broadcasted_iota(jnp.int32, kpe.shape, 1) < half, kpe * kpe, 0.0),
                     axis=-1, keepdims=True)
    n_qk = float(MLA_NOPE + MLA_ROPE)
    qg, kg = qg_ref[...], kg_ref[...]

    def rotary(hi, gains, inv):
        return inv * (hi * (gains[1:2] * cos) + pltpu.roll(hi, half, axis=1) * (gains[2:3] * sin))

    k_rot_raw = kpe * (kg[1:2] * cos) + pltpu.roll(kpe, half, axis=1) * (kg[2:3] * sin)
    for h in range(n_heads):
        xq = _dot(cq, wq_ref[:, h * MLA_PAD:(h + 1) * MLA_PAD])
        lo, hi = xq[:, :LANES], xq[:, LANES:]
        ss = jnp.sum(lo * lo, axis=-1, keepdims=True) + 0.5 * jnp.sum(hi * hi, axis=-1, keepdims=True)
        inv = lax.rsqrt(ss / n_qk + RMS_EPS) * scale
        q_ref[:, h * MLA_PAD:h * MLA_PAD + LANES] = (lo * inv * qg[0:1]).astype(BF16)
        q_ref[:, h * MLA_PAD + LANES:(h + 1) * MLA_PAD] = rotary(hi, qg, inv).astype(BF16)
        xkv = _dot(ckv, wkv_ref[:, h * MLA_PAD:(h + 1) * MLA_PAD])
        kn = xkv[:, :LANES]
        inv_k = lax.rsqrt((jnp.sum(kn * kn, axis=-1, keepdims=True) + kpe_ss) / n_qk + RMS_EPS)
        k_ref[:, h * MLA_PAD:h * MLA_PAD + LANES] = (kn * inv_k * kg[0:1]).astype(BF16)
        k_ref[:, h * MLA_PAD + LANES:(h + 1) * MLA_PAD] = (k_rot_raw * inv_k).astype(BF16)
        v_ref[:, h * LANES:(h + 1) * LANES] = xkv[:, LANES:].astype(BF16)


def _rope_gain_rows(gain):
    rot = gain[MLA_NOPE:]
    r = MLA_ROPE // 2
    swapped = jnp.concatenate([rot[r:], rot[:r]])
    z = jnp.zeros((LANES - MLA_ROPE,), F32)
    return jnp.stack([gain[:MLA_NOPE], jnp.concatenate([rot, z]), jnp.concatenate([swapped, z])])


def mla_mixer(proj, cols, bsz, seq, q_norm_g, w_q_up, kv_norm_g, w_kv_up, q_gain, k_gain, ybuf, slot):
    t = bsz * seq
    hds = MLA_N_HEADS
    q_rank, kv_rank = w_q_up.shape[0], w_kv_up.shape[0]
    r = MLA_ROPE // 2
    wq = w_q_up.reshape(q_rank, hds, MLA_NOPE + MLA_ROPE)
    wq = jnp.concatenate([wq, wq[:, :, MLA_NOPE + r:], wq[:, :, MLA_NOPE:MLA_NOPE + r]], axis=-1)
    wq = wq.reshape(q_rank, hds * MLA_PAD).astype(BF16)
    wkv = w_kv_up.astype(BF16)
    inv = 1.0 / (ROPE_THETA ** (jnp.arange(0, MLA_ROPE, 2, dtype=F32) / MLA_ROPE))
    ang = jnp.arange(seq, dtype=F32)[:, None] * inv[None, :]
    zer = jnp.zeros((seq, LANES - MLA_ROPE), F32)
    cos_t = jnp.concatenate([jnp.cos(ang), jnp.cos(ang), zer], axis=-1)
    sin_t = jnp.concatenate([-jnp.sin(ang), jnp.sin(ang), zer], axis=-1)
    tm = _tile(seq, 512)
    npos = seq // tm
    const = lambda i: (0, 0)
    q, k, v = pl.pallas_call(
        functools.partial(_mla_proj_kernel, n_heads=hds, scale=LOG2E * float(MLA_NOPE + MLA_ROPE) ** -0.5),
        grid=(t // tm,),
        in_specs=[pl.BlockSpec((tm, q_rank), lambda i: (i, cols["cq"] // q_rank)),
                  pl.BlockSpec((tm, kv_rank), lambda i: (i, cols["ckv"] // kv_rank)),
                  pl.BlockSpec((tm, LANES), lambda i: (i, cols["kpe"] // LANES)),
                  pl.BlockSpec((tm, LANES), lambda i: (i % npos, 0)),
                  pl.BlockSpec((tm, LANES), lambda i: (i % npos, 0)),
                  pl.BlockSpec((1, q_rank), const), pl.BlockSpec((1, kv_rank), const),
                  pl.BlockSpec((q_rank, hds * MLA_PAD), const), pl.BlockSpec((kv_rank, hds * MLA_PAD), const),
                  pl.BlockSpec((3, LANES), const), pl.BlockSpec((3, LANES), const)],
        out_specs=[pl.BlockSpec((tm, hds * MLA_PAD), lambda i: (i, 0)),
                   pl.BlockSpec((tm, hds * MLA_PAD), lambda i: (i, 0)),
                   pl.BlockSpec((tm, hds * LANES), lambda i: (i, 0))],
        out_shape=[jax.ShapeDtypeStruct((t, hds * MLA_PAD), BF16), jax.ShapeDtypeStruct((t, hds * MLA_PAD), BF16),
                   jax.ShapeDtypeStruct((t, hds * LANES), BF16)],
        compiler_params=_cparams(("parallel",)),
        name="mla_proj",
    )(proj, proj, proj, cos_t, sin_t, q_norm_g.reshape(1, q_rank), kv_norm_g.reshape(1, kv_rank), wq, wkv,
      _rope_gain_rows(q_gain), _rope_gain_rows(k_gain))
    return flash_attention(q.reshape(1, t, hds * MLA_PAD), k, v, bsz, seq, hds, MLA_PAD, ybuf, slot)


def _diff_prep_kernel(q_ref, k_ref, v_ref, qg_ref, kg_ref, seg_ref, qo_ref, ko_ref, vo_ref, *, hd, scale):
    seg = seg_ref[...]
    lane = lax.broadcasted_iota(jnp.int32, (1, LANES), 1)
    n_col = q_ref.shape[1] // LANES
    for c in range(n_col):
        sl = slice(c * LANES, (c + 1) * LANES)
        xq, xk = q_ref[:, sl], k_ref[:, sl]
        ssq = _dot_exact_rhs01(xq * xq, seg)
        ssk = _dot_exact_rhs01(xk * xk, seg)
        qn = xq * lax.rsqrt(ssq / hd + RMS_EPS) * (qg_ref[...] * scale)
        kn = xk * lax.rsqrt(ssk / hd + RMS_EPS) * kg_ref[...]
        qo_ref[0, :, sl] = jnp.where(lane < hd, qn, 0.0).astype(BF16)
        qo_ref[1, :, sl] = jnp.where(lane >= hd, qn, 0.0).astype(BF16)
        ko_ref[:, sl] = kn.astype(BF16)
    vo_ref[...] = v_ref[...].astype(BF16)


def _t5_bucket_np(n):
    max_exact = REL_BUCKETS // 2
    nf = np.maximum(n, 1).astype(np.float32)
    large = max_exact + (np.log(nf / np.float32(max_exact)) / np.float32(math.log(REL_MAX_DIST / max_exact))
                         * np.float32(REL_BUCKETS - max_exact)).astype(np.int32)
    large = np.minimum(large, REL_BUCKETS - 1)
    return np.where(n < max_exact, n, large).astype(np.int32)


def _bias_tiles_kernel(bkt_ref, rb_ref, o_ref):
    h = pl.program_id(0)
    for d in range(2):
        bkt = bkt_ref[d]
        tile = jnp.zeros(bkt.shape, F32)
        for b in range(REL_BUCKETS):
            tile = jnp.where(bkt == b, rb_ref[b, h] * LOG2E, tile)
        o_ref[0, d] = tile


def _diff_finalize(lambda_init, tq):
    def fin(o, extras):
        lq1, lk1, lq2, lk2, sub_g = extras
        lam = (jnp.exp(jnp.sum(lq1 * lk1, axis=-1, keepdims=True))
               - jnp.exp(jnp.sum(lq2 * lk2, axis=-1, keepdims=True)) + lambda_init)
        d = o[:tq] - lam * o[tq:]
        ms = jnp.mean(d * d, axis=-1, keepdims=True)
        return d * lax.rsqrt(ms + RMS_EPS) * (sub_g * (1.0 - lambda_init))
    return fin


def diff_mixer(proj, cols, bsz, seq, q_gain, k_gain, lq1, lk1, lq2, lk2, subln_g, rel_bias, lambda_init,
               ybuf, slot):
    t = bsz * seq
    hds = DIFF_N_HEADS
    hd = q_gain.shape[0]
    width = hds * 2 * hd
    tm = _tile(t, 512)
    seg = (np.arange(LANES)[:, None] // hd == np.arange(LANES)[None, :] // hd)
    seg = jnp.asarray(seg, BF16)
    gq, gk = jnp.tile(q_gain, LANES // hd).reshape(1, LANES), jnp.tile(k_gain, LANES // hd).reshape(1, LANES)
    const = lambda i: (0, 0)
    qd, kd, vd = pl.pallas_call(
        functools.partial(_diff_prep_kernel, hd=hd, scale=LOG2E * float(hd) ** -0.5),
        grid=(t // tm,),
        in_specs=[pl.BlockSpec((tm, width), lambda i: (i, cols["dq"] // width)),
                  pl.BlockSpec((tm, width), lambda i: (i, cols["dk"] // width)),
                  pl.BlockSpec((tm, width), lambda i: (i, cols["dv"] // width)),
                  pl.BlockSpec((1, LANES), const), pl.BlockSpec((1, LANES), const),
                  pl.BlockSpec((LANES, LANES), const)],
        out_specs=[pl.BlockSpec((2, tm, width), lambda i: (0, i, 0)),
                   pl.BlockSpec((tm, width), lambda i: (i, 0)), pl.BlockSpec((tm, width), lambda i: (i, 0))],
        out_shape=[jax.ShapeDtypeStruct((2, t, width), BF16), jax.ShapeDtypeStruct((t, width), BF16),
                   jax.ShapeDtypeStruct((t, width), BF16)],
        compiler_params=_cparams(("parallel",)),
        name="diff_prep",
    )(proj, proj, proj, gq, gk, seg)

    tq = _tile(seq, 512)
    assert tq + 1 > REL_MAX_DIST, "blocks two or more before the diagonal must all be at the far distance"
    idx = np.arange(tq)
    dist0 = np.maximum(idx[:, None] - idx[None, :], 0)
    dist1 = idx[:, None] - idx[None, :] + tq
    bkt = jnp.asarray(np.stack([_t5_bucket_np(dist0), _t5_bucket_np(dist1)]))
    bias = pl.pallas_call(
        _bias_tiles_kernel,
        grid=(hds,),
        in_specs=[pl.BlockSpec((2, tq, tq), lambda h: (0, 0, 0)), pl.BlockSpec(memory_space=pltpu.SMEM)],
        out_specs=pl.BlockSpec((1, 2, tq, tq), lambda h: (h, 0, 0, 0)),
        out_shape=jax.ShapeDtypeStruct((hds, 2, tq, tq), F32),
        compiler_params=_cparams(("parallel",)),
        name="t5_bias_tiles",
    )(bkt, rel_bias)
    far = rel_bias[REL_BUCKETS - 1] * LOG2E
    fin = tuple(a.reshape(1, -1) for a in (lq1, lk1, lq2, lk2, subln_g))
    return flash_attention(qd, kd, vd, bsz, seq, hds, 2 * hd, ybuf, slot, n_maps=2, bias=bias, far=far, fin=fin,
                           finalize=_diff_finalize(lambda_init, tq), tq=tq)


def _pack_w_in(w_in_l, widths):
    (z_w, xbc_w, dt_w, u_w, cq_w, ckv_w, kpe_w, dq_w, dk_w, dv_w) = widths
    d = w_in_l.shape[0]
    w_in_l = w_in_l.astype(BF16)
    starts = np.cumsum([0] + list(widths))
    seg = {n: w_in_l[:, starts[i]:starts[i + 1]]
           for i, n in enumerate(("z", "xbc", "dt", "u", "cq", "ckv", "kpe", "dq", "dk", "dv"))}
    r = kpe_w // 2
    kpe2 = jnp.concatenate([seg["kpe"], seg["kpe"][:, r:], seg["kpe"][:, :r],
                            jnp.zeros((d, LANES - 2 * kpe_w), BF16)], axis=1)
    dtp = jnp.concatenate([seg["dt"], jnp.zeros((d, LANES - dt_w), BF16)], axis=1)
    order = [("xbc", seg["xbc"]), ("z", seg["z"]), ("u", seg["u"]), ("dq", seg["dq"]), ("dk", seg["dk"]),
             ("dv", seg["dv"]), ("ckv", seg["ckv"]), ("cq", seg["cq"]), ("kpe", kpe2), ("dt", dtp)]
    cols, off = {}, 0
    for name, w in order:
        assert off % w.shape[1] == 0, (name, off, w.shape[1])
        cols[name] = off
        off += w.shape[1]
    return jnp.concatenate([w for _, w in order], axis=1), cols


def kernel(x, norm_mix_g, w_in, ssd_conv_w, ssd_conv_b, ssd_dt_bias, ssd_a_log, ssd_d, ssd_norm_g, s5_a_re, s5_a_im, s5_b_re, s5_b_im, s5_c_re, s5_c_im, s5_log_dt, s5_d, s5_w_glu, s5_b_glu, mla_q_norm_g, mla_w_q_up, mla_kv_norm_g, mla_w_kv_up, mla_q_gain, mla_k_gain, diff_q_gain, diff_k_gain, diff_lq1, diff_lk1, diff_lq2, diff_lk2, diff_subln_g, rel_bias, w_gate, b_gate, w_branch, w_out, norm_ffn_g, ffn_w_gate, ffn_w_up, ffn_w_down, moe_w_router, moe_b_router, moe_w_gate, moe_w_up, moe_w_down):
    bsz, seq, d = x.shape
    t = bsz * seq
    depth = w_in.shape[0]
    mix = d // 4
    n_ssd_heads = ssd_dt_bias.shape[1]
    widths = (mix, ssd_conv_w.shape[2], n_ssd_heads, mix, mla_w_q_up.shape[1], mla_w_kv_up.shape[1], MLA_ROPE,
              mix, mix, mix)
    xf = x.reshape(t, d)
    for i in range(depth):
        h = rmsnorm(xf, norm_mix_g[i])
        w_packed, cols = _pack_w_in(w_in[i], widths)
        proj = matmul(h, [w_packed], _epi_plain, out_dtype=F32, tn=512, name="in_proj")
        dt_t = jnp.swapaxes(proj[:, cols["dt"]:cols["dt"] + n_ssd_heads].reshape(bsz, seq, n_ssd_heads), 1, 2)
        ys = ssd_mixer(proj, dt_t, cols, bsz, seq, ssd_conv_w[i], ssd_conv_b[i], ssd_dt_bias[i], ssd_a_log[i],
                       ssd_d[i], ssd_norm_g[i], jnp.zeros((N_BRANCH, t, mix), BF16), 0)
        ys = s5_mixer(proj, cols["u"], bsz, seq, s5_a_re[i], s5_a_im[i], s5_b_re[i], s5_b_im[i], s5_c_re[i],
                      s5_c_im[i], s5_log_dt[i], s5_d[i], s5_w_glu[i], s5_b_glu[i], ys, 1)
        ys = mla_mixer(proj, cols, bsz, seq, mla_q_norm_g[i], mla_w_q_up[i], mla_kv_norm_g[i], mla_w_kv_up[i],
                       mla_q_gain[i], mla_k_gain[i], ys, 2)
        lambda_init = 0.8 - 0.6 * math.exp(-0.3 * i)
        ys = diff_mixer(proj, cols, bsz, seq, diff_q_gain[i], diff_k_gain[i], diff_lq1[i], diff_lk1[i],
                        diff_lq2[i], diff_lk2[i], diff_subln_g[i], rel_bias, lambda_init, ys, 3)
        merged = gated_merge(h, w_gate[i].astype(BF16), b_gate[i][:, None, :], ys, w_branch[i].astype(BF16))
        xf = matmul(merged, [w_out[i].astype(BF16)], _epi_residual, extras=(xf,), extra_kinds=("mn",),
                    tk=2048, name="out_proj")
        e = i // 2
        if i % 2 == 0:
            h2 = rmsnorm(xf, norm_ffn_g[i])
            act = matmul(h2, [ffn_w_gate[e].astype(BF16), ffn_w_up[e].astype(BF16)], _epi_swiglu,
                         out_dtype=BF16, tn=512, name="ffn_up")
            xf = matmul(act, [ffn_w_down[e].astype(BF16)], _epi_residual, extras=(xf,), extra_kinds=("mn",),
                        tk=3584, name="ffn_down")
        else:
            xf = moe_block(xf, norm_ffn_g[i], moe_w_router[e], moe_b_router[e], moe_w_gate[e], moe_w_up[e],
                           moe_w_down[e])
    return xf.reshape(bsz, seq, d)
```

```python
import functools
import math

import numpy as np
import jax
import jax.numpy as jnp
from jax import lax
from jax.experimental import pallas as pl
from jax.experimental.pallas import tpu as pltpu

F32 = jnp.float32
BF16 = jnp.bfloat16

V7X_VMEM_BYTES = 64 * 1024 * 1024
VMEM_LIMIT = V7X_VMEM_BYTES - 8 * 1024 * 1024
LANES = 128

RMS_EPS = 1e-6
SSD_HEAD_DIM = 64
SSD_N_GROUPS = 4
SSD_D_STATE = 128
SSD_CHUNK = 128
S5_GROUP = 16
S5_STATE = 64
S5_GROUPS_PER_BLOCK = 8
MLA_N_HEADS = 8
MLA_NOPE = 128
MLA_ROPE = 64
MLA_PAD = 256
ROPE_THETA = 10000.0
DIFF_N_HEADS = 8
REL_BUCKETS = 32
REL_MAX_DIST = 128
TOP_K = 2
N_BRANCH = 4
LOG2E = math.log2(math.e)


def _cparams(sem):
    return pltpu.CompilerParams(dimension_semantics=sem, vmem_limit_bytes=VMEM_LIMIT)


def _tile(n, pref):
    t = min(n, pref)
    while n % t:
        t //= 2
    return t


def _split3(a):
    hi = a.astype(BF16)
    r1 = a - hi.astype(F32)
    mid = r1.astype(BF16)
    lo = (r1 - mid.astype(F32)).astype(BF16)
    return hi, mid, lo


def _dot(a, b):
    return jnp.dot(a, b, preferred_element_type=F32)


def _dot_exact_rhs01(a, sel):
    hi, mid, lo = _split3(a)
    return _dot(hi, sel) + _dot(mid, sel) + _dot(lo, sel)


def _dot_exact_lhs01(sel, b):
    hi, mid, lo = _split3(b)
    return _dot(sel, hi) + _dot(sel, mid) + _dot(sel, lo)


def _rmsnorm_kernel(x_ref, g_ref, o_ref):
    x = x_ref[...]
    ms = jnp.mean(x * x, axis=-1, keepdims=True)
    o_ref[...] = (x * lax.rsqrt(ms + RMS_EPS) * g_ref[...]).astype(o_ref.dtype)


def rmsnorm(x, g, tm=256):
    t, d = x.shape
    tm = _tile(t, tm)
    return pl.pallas_call(
        _rmsnorm_kernel,
        grid=(t // tm,),
        in_specs=[pl.BlockSpec((tm, d), lambda i: (i, 0)), pl.BlockSpec((1, d), lambda i: (0, 0))],
        out_specs=pl.BlockSpec((tm, d), lambda i: (i, 0)),
        out_shape=jax.ShapeDtypeStruct((t, d), BF16),
        compiler_params=_cparams(("parallel",)),
        name="rmsnorm",
    )(x, g.reshape(1, d))


def _rmsnorm_router_kernel(x_ref, g_ref, wr_ref, br_ref, hp_ref, route_ref, cnt_ref, run_ref, *, n_experts):
    @pl.when(pl.program_id(0) == 0)
    def _():
        run_ref[...] = jnp.zeros(run_ref.shape, F32)

    x = x_ref[...]
    tm, d = x.shape
    ms = jnp.mean(x * x, axis=-1, keepdims=True)
    h = x * lax.rsqrt(ms + RMS_EPS) * g_ref[...]
    bits = lax.bitcast_convert_type(h.astype(BF16).astype(F32), jnp.uint32)
    hp_ref[...] = (bits[:, :d // 2] >> 16) | (bits[:, d // 2:] & jnp.uint32(0xFFFF0000))
    logits = jnp.dot(h, wr_ref[...], preferred_element_type=F32, precision=lax.Precision.HIGHEST) + br_ref[...]
    lane = lax.broadcasted_iota(jnp.int32, logits.shape, 1)
    neg = jnp.float32(-jnp.inf)
    logits = jnp.where(lane < n_experts, logits, neg)
    m1 = jnp.max(logits, axis=-1, keepdims=True)
    i1 = jnp.min(jnp.where(logits == m1, lane, LANES), axis=-1, keepdims=True)
    rest = jnp.where(lane == i1, neg, logits)
    m2 = jnp.max(rest, axis=-1, keepdims=True)
    i2 = jnp.min(jnp.where(rest == m2, lane, LANES), axis=-1, keepdims=True)
    e2 = jnp.exp(m2 - m1)
    w1 = 1.0 / (1.0 + e2)
    w2 = e2 / (1.0 + e2)
    sel = jnp.where(lane == i1, 1.0, 0.0) + jnp.where(lane == i2, 1.0, 0.0)
    row = lax.broadcasted_iota(jnp.int32, (tm, tm), 0)
    col = lax.broadcasted_iota(jnp.int32, (tm, tm), 1)
    before = jnp.where(col < row, 1.0, 0.0).astype(BF16)
    excl = _dot(before, sel.astype(BF16)) + run_ref[0:1, :]
    run_ref[...] = run_ref[...] + jnp.sum(sel, axis=0, keepdims=True)
    cnt_ref[...] = run_ref[...]
    r1 = jnp.sum(jnp.where(lane == i1, excl, 0.0), axis=-1, keepdims=True)
    r2 = jnp.sum(jnp.where(lane == i2, excl, 0.0), axis=-1, keepdims=True)
    route = jnp.zeros(logits.shape, F32)
    for pos, val in enumerate((i1.astype(F32), i2.astype(F32), r1, r2, w1, w2)):
        route = jnp.where(lane == pos, val, route)
    route_ref[...] = route


def rmsnorm_router(x, g, w_router, b_router, tm=256):
    t, d = x.shape
    ne = w_router.shape[1]
    tm = _tile(t, tm)
    wr = jnp.zeros((d, LANES), F32).at[:, :ne].set(w_router)
    br = jnp.zeros((1, LANES), F32).at[0, :ne].set(b_router)
    return pl.pallas_call(
        functools.partial(_rmsnorm_router_kernel, n_experts=ne),
        grid=(t // tm,),
        in_specs=[pl.BlockSpec((tm, d), lambda i: (i, 0)), pl.BlockSpec((1, d), lambda i: (0, 0)),
                  pl.BlockSpec((d, LANES), lambda i: (0, 0)), pl.BlockSpec((1, LANES), lambda i: (0, 0))],
        out_specs=[pl.BlockSpec((tm, d // 2), lambda i: (i, 0)), pl.BlockSpec((tm, LANES), lambda i: (i, 0)),
                   pl.BlockSpec((8, LANES), lambda i: (0, 0))],
        out_shape=[jax.ShapeDtypeStruct((t, d // 2), jnp.uint32), jax.ShapeDtypeStruct((t, LANES), F32),
                   jax.ShapeDtypeStruct((8, LANES), F32)],
        scratch_shapes=[pltpu.VMEM((8, LANES), F32)],
        compiler_params=_cparams(("arbitrary",)),
        name="rmsnorm_router",
    )(x, g.reshape(1, d), wr, br)


MOE_ROW_TILE = 512
MOE_TOKEN_BLOCK = 256


def _dispatch_kernel(src_ref, h_hbm, o_ref, sem, *, tmg):
    def start(r2, c):
        for p in range(2):
            r = 2 * r2 + p
            pltpu.make_async_copy(h_hbm.at[pl.ds(src_ref[0, 0, r], 1)], o_ref.at[pl.ds(r, 1)], sem).start(
                priority=p)
        return c

    lax.fori_loop(0, tmg // 2, start, 0)

    def wait(r, c):
        pltpu.make_async_copy(h_hbm.at[pl.ds(0, 1)], o_ref.at[pl.ds(0, 1)], sem).wait()
        return c

    lax.fori_loop(0, tmg, wait, 0)


def moe_dispatch(hp, src, n_rows):
    dw = hp.shape[1]
    tmg = MOE_ROW_TILE
    return pl.pallas_call(
        functools.partial(_dispatch_kernel, tmg=tmg),
        grid=(n_rows // tmg,),
        in_specs=[pl.BlockSpec((1, 1, tmg), lambda i: (i, 0, 0), memory_space=pltpu.SMEM),
                  pl.BlockSpec(memory_space=pl.ANY)],
        out_specs=pl.BlockSpec((tmg, dw), lambda i: (i, 0)),
        out_shape=jax.ShapeDtypeStruct((n_rows, dw), hp.dtype),
        scratch_shapes=[pltpu.SemaphoreType.DMA(())],
        compiler_params=_cparams(("arbitrary",)),
        name="moe_dispatch",
    )(src.reshape(n_rows // tmg, 1, tmg), hp)


def _unpack_halves(xp):
    lo = lax.bitcast_convert_type(xp << 16, F32).astype(BF16)
    hi = lax.bitcast_convert_type(xp & jnp.uint32(0xFFFF0000), F32).astype(BF16)
    return lo, hi


def _grouped_kernel(e_tab, jw_tab, rx_tab, j_tab, r_tab, first_tab, valid_tab, x_ref, *rest, n_w, packed,
                    epilogue):
    del e_tab, jw_tab, rx_tab, j_tab, r_tab
    w_refs, o_ref, wb_refs = rest[:n_w], rest[n_w], rest[n_w + 1:]
    s = pl.program_id(0)

    @pl.when(first_tab[s] == 1)
    def _():
        for w, wb in zip(w_refs, wb_refs):
            wb[...] = w[0].astype(BF16)

    @pl.when(valid_tab[s] == 1)
    def _():
        if packed:
            lo, hi = _unpack_halves(x_ref[...])
            half = lo.shape[1]
            parts = [_dot(lo, wb[:half, :]) + _dot(hi, wb[half:, :]) for wb in wb_refs]
        else:
            x = x_ref[...]
            parts = [_dot(x, wb[...]) for wb in wb_refs]
        o_ref[...] = epilogue(parts, []).astype(o_ref.dtype)

    @pl.when(valid_tab[s] == 0)
    def _():
        o_ref[...] = jnp.zeros(o_ref.shape, o_ref.dtype)


def grouped_matmul(x, ws, tabs, epilogue, out_dtype, tn, packed, name):
    n_rows, kx = x.shape
    ne, kdim, n = ws[0].shape
    tmg = MOE_ROW_TILE
    tn = _tile(n, tn)
    n_steps = tabs[0].shape[0]
    grid_spec = pltpu.PrefetchScalarGridSpec(
        num_scalar_prefetch=7,
        grid=(n_steps,),
        in_specs=[pl.BlockSpec((tmg, kx), lambda s, e, jw, rx, j, r, f, v: (rx[s], 0))]
        + [pl.BlockSpec((1, kdim, tn), lambda s, e, jw, rx, j, r, f, v: (e[s], 0, jw[s])) for _ in ws],
        out_specs=pl.BlockSpec((tmg, tn), lambda s, e, jw, rx, j, r, f, v: (r[s], j[s])),
        scratch_shapes=[pltpu.VMEM((kdim, tn), BF16) for _ in ws],
    )
    return pl.pallas_call(
        functools.partial(_grouped_kernel, n_w=len(ws), packed=packed, epilogue=epilogue),
        grid_spec=grid_spec,
        out_shape=jax.ShapeDtypeStruct((n_rows, n), out_dtype),
        compiler_params=_cparams(("arbitrary",)),
        name=name,
    )(*tabs, x, *ws)


def _group_steps(padded, nj, n_tiles):
    tmg = MOE_ROW_TILE
    tiles = padded // tmg
    tile_start = jnp.cumsum(tiles) - tiles
    steps = tiles * nj
    step_end = jnp.cumsum(steps)
    step_start = step_end - steps
    total = step_end[-1]
    used_tiles = jnp.sum(tiles)
    s = jnp.arange(n_tiles * nj, dtype=jnp.int32)
    sc = jnp.minimum(s, total - 1)
    e = jnp.minimum(jnp.searchsorted(step_end, sc, side="right"), tiles.shape[0] - 1).astype(jnp.int32)
    local = sc - step_start[e]
    te = jnp.maximum(tiles[e], 1)
    jw = local // te
    rl = local % te
    rx = tile_start[e] + rl
    valid = s < total
    first = valid & (rl == 0)
    extra = jnp.maximum(s - total, 0)
    j = jnp.where(valid, jw, extra % nj)
    r = jnp.where(valid, rx, used_tiles + extra // nj)
    i32 = lambda a: a.astype(jnp.int32)
    return i32(e), i32(jw), i32(rx), i32(j), i32(r), i32(first), i32(valid)


def _combine_kernel(dest_ref, x_ref, route_ref, ys_hbm, o_ref, buf, sem, *, tb):
    def start(t, c):
        for k in range(TOP_K):
            pltpu.make_async_copy(ys_hbm.at[pl.ds(dest_ref[0, 0, TOP_K * t + k], 1)], buf.at[k, pl.ds(t, 1)],
                                  sem).start(priority=k % 2)
        return c

    lax.fori_loop(0, tb, start, 0)

    def wait(t, c):
        pltpu.make_async_copy(ys_hbm.at[pl.ds(0, 1)], buf.at[0, pl.ds(0, 1)], sem).wait()
        return c

    lax.fori_loop(0, TOP_K * tb, wait, 0)
    route = route_ref[...]
    o_ref[...] = x_ref[...] + route[:, 4:5] * buf[0] + route[:, 5:6] * buf[1]


def moe_combine(x, route, ys, dest):
    t, d = x.shape
    tb = _tile(t, MOE_TOKEN_BLOCK)
    return pl.pallas_call(
        functools.partial(_combine_kernel, tb=tb),
        grid=(t // tb,),
        in_specs=[pl.BlockSpec((1, 1, TOP_K * tb), lambda i: (i, 0, 0), memory_space=pltpu.SMEM),
                  pl.BlockSpec((tb, d), lambda i: (i, 0)), pl.BlockSpec((tb, LANES), lambda i: (i, 0)),
                  pl.BlockSpec(memory_space=pl.ANY)],
        out_specs=pl.BlockSpec((tb, d), lambda i: (i, 0)),
        out_shape=jax.ShapeDtypeStruct((t, d), F32),
        scratch_shapes=[pltpu.VMEM((TOP_K, tb, d), F32), pltpu.SemaphoreType.DMA(())],
        compiler_params=_cparams(("arbitrary",)),
        name="moe_combine",
    )(dest.reshape(t // tb, 1, TOP_K * tb), x, route, ys)


def moe_block(x, norm_g, w_router, b_router, w_gate, w_up, w_down):
    t, d = x.shape
    ne, _, f = w_gate.shape
    tmg = MOE_ROW_TILE
    hp, route, cnt = rmsnorm_router(x, norm_g, w_router, b_router)
    counts = cnt[0, :ne].astype(jnp.int32)
    padded = (counts + tmg - 1) // tmg * tmg
    starts = jnp.cumsum(padded) - padded
    ids = route[:, 0:2].astype(jnp.int32)
    dest = starts[ids] + route[:, 2:4].astype(jnp.int32)
    n_rows = t * TOP_K + ne * tmg
    n_tiles = n_rows // tmg
    src = jnp.zeros((n_rows,), jnp.int32).at[dest.reshape(-1)].set(
        jnp.repeat(jnp.arange(t, dtype=jnp.int32), TOP_K), unique_indices=True)
    xs = moe_dispatch(hp, src, n_rows)
    tn_up, tn_down = _tile(f, 512), _tile(d, 1024)
    act = grouped_matmul(xs, [w_gate, w_up], _group_steps(padded, f // tn_up, n_tiles), _epi_swiglu, BF16,
                         tn_up, True, "moe_up")
    ys = grouped_matmul(act, [w_down], _group_steps(padded, d // tn_down, n_tiles), _epi_plain, F32,
                        tn_down, False, "moe_down")
    return moe_combine(x, route, ys, dest)


def _mm_kernel(*refs, n_w, n_e, nk, epilogue):
    a_ref = refs[0]
    w_refs = refs[1:1 + n_w]
    e_refs = refs[1 + n_w:1 + n_w + n_e]
    o_ref = refs[1 + n_w + n_e]
    acc_refs = refs[2 + n_w + n_e:]
    a = a_ref[...]
    parts = [_dot(a, w[...]) for w in w_refs]
    if nk == 1:
        o_ref[...] = epilogue(parts, [e[...] for e in e_refs]).astype(o_ref.dtype)
        return
    k = pl.program_id(2)

    @pl.when(k == 0)
    def _():
        for acc, p in zip(acc_refs, parts):
            acc[...] = p

    @pl.when(k > 0)
    def _():
        for acc, p in zip(acc_refs, parts):
            acc[...] += p

    @pl.when(k == nk - 1)
    def _():
        o_ref[...] = epilogue([acc[...] for acc in acc_refs], [e[...] for e in e_refs]).astype(o_ref.dtype)


def matmul(a, ws, epilogue, extras=(), extra_kinds=(), out_dtype=F32, tm=1024, tn=1024, tk=4096, w_prefix=(),
           name="matmul"):
    m, kdim = a.shape
    n = ws[0].shape[-1]
    tm, tn, tk = _tile(m, tm), _tile(n, tn), _tile(kdim, tk)
    nk = kdim // tk
    squeezed = (None,) * len(w_prefix)
    a_mode = dict(pipeline_mode=pl.Buffered(1)) if nk == 1 else {}
    in_specs = [pl.BlockSpec((tm, tk), lambda i, j, k: (i, k), **a_mode)]
    in_specs += [pl.BlockSpec(squeezed + (tk, tn), lambda i, j, k: (*w_prefix, k, j)) for _ in ws]
    for kind in extra_kinds:
        if kind == "mn":
            in_specs.append(pl.BlockSpec((tm, tn), lambda i, j, k: (i, j)))
        elif kind == "m":
            in_specs.append(pl.BlockSpec((tm, LANES), lambda i, j, k: (i, 0)))
        else:
            in_specs.append(pl.BlockSpec((1, tn), lambda i, j, k: (0, j)))
    scratch = [pltpu.VMEM((tm, tn), F32) for _ in ws] if nk > 1 else []
    return pl.pallas_call(
        functools.partial(_mm_kernel, n_w=len(ws), n_e=len(extras), nk=nk, epilogue=epilogue),
        grid=(m // tm, n // tn, nk),
        in_specs=in_specs,
        out_specs=pl.BlockSpec((tm, tn), lambda i, j, k: (i, j)),
        out_shape=jax.ShapeDtypeStruct((m, n), out_dtype),
        scratch_shapes=scratch,
        compiler_params=_cparams(("parallel", "parallel", "arbitrary")),
        name=name,
    )(a, *ws, *extras)


def _epi_plain(parts, extras):
    return parts[0]


def _epi_residual(parts, extras):
    return extras[0] + parts[0]


def _epi_swiglu(parts, extras):
    g, u = parts
    return g * jax.nn.sigmoid(g) * u


def _epi_glu(parts, extras):
    y, b = extras
    return y * jax.nn.sigmoid(parts[0] + b)


def _merge_kernel(h_ref, wg_ref, b_ref, y_ref, wb_ref, o_ref, accm_ref, *accg, nk, nb):
    j = pl.program_id(2)
    k = pl.program_id(3)
    part = _dot(h_ref[...], wg_ref[0])

    def finish(pre):
        term = jax.nn.sigmoid(pre + b_ref[0]) * _dot(y_ref[0], wb_ref[0])

        @pl.when(j == 0)
        def _():
            accm_ref[...] = term

        @pl.when((j > 0) & (j < nb - 1))
        def _():
            accm_ref[...] += term

        @pl.when(j == nb - 1)
        def _():
            o_ref[...] = (accm_ref[...] + term).astype(o_ref.dtype)

    if nk == 1:
        finish(part)
        return

    accg_ref, = accg

    @pl.when(k == 0)
    def _():
        accg_ref[...] = part

    @pl.when(k > 0)
    def _():
        accg_ref[...] += part

    @pl.when(k == nk - 1)
    def _():
        finish(accg_ref[...])


def gated_merge(h, w_gate, b_gate, ys, w_branch, layer, tm=2048, tn=512, tk=4096):
    t, d = h.shape
    nb, _, wdt = ys.shape
    tm, tn, tk = _tile(t, tm), _tile(d, tn), _tile(d, tk)
    nk = d // tk
    assert nb >= 2
    h_mode = dict(pipeline_mode=pl.Buffered(1)) if nk == 1 else {}
    return pl.pallas_call(
        functools.partial(_merge_kernel, nk=nk, nb=nb),
        grid=(t // tm, d // tn, nb, nk),
        in_specs=[
            pl.BlockSpec((tm, tk), lambda i, n, j, k: (i, k), **h_mode),
            pl.BlockSpec((None, 1, tk, tn), lambda i, n, j, k: (layer, j, k, n)),
            pl.BlockSpec((1, 1, tn), lambda i, n, j, k: (j, 0, n)),
            pl.BlockSpec((1, tm, wdt), lambda i, n, j, k: (j, i, 0)),
            pl.BlockSpec((None, 1, wdt, tn), lambda i, n, j, k: (layer, j, 0, n)),
        ],
        out_specs=pl.BlockSpec((tm, tn), lambda i, n, j, k: (i, n)),
        out_shape=jax.ShapeDtypeStruct((t, d), BF16),
        scratch_shapes=[pltpu.VMEM((tm, tn), F32)] * (1 if nk == 1 else 2),
        compiler_params=_cparams(("parallel", "parallel", "arbitrary", "arbitrary")),
        name="gated_merge",
    )(h, w_gate, b_gate, ys, w_branch)


def _softplus(x):
    return jnp.maximum(x, 0.0) + jnp.log1p(jnp.exp(-jnp.abs(x)))


def _ssd_kernel(xbc_ref, z_ref, dt_ref, dtt_ref, cw_ref, cb_ref, dtb_ref, dtbt_ref, alog_ref, alogt_ref,
                dskip_ref, ng_ref, expand_ref, ybuf_ref, o_ref, xe_ref, st_ref, *, n_heads, d_inner):
    del ybuf_ref
    q = SSD_CHUNK
    gn = SSD_N_GROUPS * SSD_D_STATE
    hpg = n_heads // SSD_N_GROUPS
    gw = hpg * SSD_HEAD_DIM
    c = pl.program_id(1)

    @pl.when(c == 0)
    def _():
        xe_ref[0:8, :] = jnp.zeros((8, xe_ref.shape[1]), F32)
        st_ref[...] = jnp.zeros(st_ref.shape, F32)

    xe_ref[8:8 + q, :] = xbc_ref[...]
    acc = cb_ref[...] + cw_ref[3:4, :] * xe_ref[8:8 + q, :]
    for kk in range(3):
        acc = acc + cw_ref[kk:kk + 1, :] * xe_ref[5 + kk:5 + kk + q, :]
    xe_ref[0:8, :] = xe_ref[q:q + 8, :]
    xc = acc * jax.nn.sigmoid(acc)
    xs = xc[:, :d_inner]
    bm = xc[:, d_inner:d_inner + gn].astype(BF16)
    cm = xc[:, d_inner + gn:].astype(BF16)

    dt = _softplus(dt_ref[...] + dtb_ref[...])
    da = dt * (-jnp.exp(alog_ref[...]))
    row = lax.broadcasted_iota(jnp.int32, (q, q), 0)
    col = lax.broadcasted_iota(jnp.int32, (q, q), 1)
    lower = row >= col
    tril = jnp.where(lower, 1.0, 0.0).astype(BF16)
    acs = _dot_exact_lhs01(tril, da)
    dtt = _softplus(dtt_ref[0] + dtbt_ref[...])
    dat = dtt * (-jnp.exp(alogt_ref[...]))
    triu = jnp.where(row <= col, 1.0, 0.0).astype(BF16)
    acst = _dot_exact_rhs01(dat, triu)

    expand = expand_ref[...]
    dt_x = _dot_exact_rhs01(dt, expand)
    acs_x = _dot_exact_rhs01(acs, expand)
    tot_x = acs_x[q - 1:q, :]
    xdt = xs * dt_x
    xdt_b = xdt.astype(BF16)
    xw_b = (xdt * jnp.exp(tot_x - acs_x)).astype(BF16)
    in_decay = jnp.exp(acs_x)
    chunk_decay = jnp.exp(tot_x)

    lane = lax.broadcasted_iota(jnp.int32, (1, gw), 1)
    ys = []
    for g in range(SSD_N_GROUPS):
        cg = cm[:, g * SSD_D_STATE:(g + 1) * SSD_D_STATE]
        bg = bm[:, g * SSD_D_STATE:(g + 1) * SSD_D_STATE]
        sl = slice(g * gw, (g + 1) * gw)
        gmat = lax.dot_general(cg, bg, (((1,), (1,)), ((), ())), preferred_element_type=F32)
        state = st_ref[g]
        y_g = _dot(cg, state.astype(BF16)) * in_decay[:, sl]
        xg = xdt_b[:, sl]
        for hh in range(hpg):
            hd = g * hpg + hh
            seg = acs[:, hd:hd + 1] - acst[hd:hd + 1, :]
            lmat = jnp.exp(jnp.where(lower, seg, -jnp.inf))
            mh = (gmat * lmat).astype(BF16)
            head_cols = (lane >= hh * SSD_HEAD_DIM) & (lane < (hh + 1) * SSD_HEAD_DIM)
            y_g = y_g + _dot(mh, jnp.where(head_cols, xg, jnp.zeros_like(xg)))
        upd = lax.dot_general(bg, xw_b[:, sl], (((0,), (0,)), ((), ())), preferred_element_type=F32)
        st_ref[g] = state * chunk_decay[:, sl] + upd
        ys.append(y_g)
    y = jnp.concatenate(ys, axis=-1) + xs * dskip_ref[...]
    zz = z_ref[...]
    y = y * (zz * jax.nn.sigmoid(zz))
    outs = []
    for g in range(SSD_N_GROUPS):
        yg = y[:, g * gw:(g + 1) * gw]
        ms = jnp.mean(yg * yg, axis=-1, keepdims=True)
        outs.append(yg * lax.rsqrt(ms + RMS_EPS))
    o_ref[0] = (jnp.concatenate(outs, axis=-1) * ng_ref[...]).astype(o_ref.dtype)


def ssd_mixer(proj, dt_t, cols, bsz, seq, conv_w, conv_b, dt_bias, a_log, d_skip, norm_g, ybuf, slot):
    n_heads = dt_bias.shape[0]
    d_inner = n_heads * SSD_HEAD_DIM
    conv_dim = conv_w.shape[1]
    q = SSD_CHUNK
    nc = seq // q
    pad = LANES - n_heads
    expand = jnp.repeat(jnp.eye(LANES, dtype=BF16)[:, :n_heads], SSD_HEAD_DIM, axis=1)
    dtb = jnp.pad(dt_bias, (0, pad)).reshape(1, LANES)
    alog = jnp.pad(a_log, (0, pad)).reshape(1, LANES)
    x_blk, z_blk, dt_blk = cols["xbc"] // conv_dim, cols["z"] // d_inner, cols["dt"] // LANES
    const = lambda b, c: (0, 0)
    return pl.pallas_call(
        functools.partial(_ssd_kernel, n_heads=n_heads, d_inner=d_inner),
        grid=(bsz, nc),
        in_specs=[
            pl.BlockSpec((q, conv_dim), lambda b, c: (b * nc + c, x_blk)),
            pl.BlockSpec((q, d_inner), lambda b, c: (b * nc + c, z_blk)),
            pl.BlockSpec((q, LANES), lambda b, c: (b * nc + c, dt_blk)),
            pl.BlockSpec((1, n_heads, q), lambda b, c: (b, 0, c)),
            pl.BlockSpec((4, conv_dim), const),
            pl.BlockSpec((1, conv_dim), const),
            pl.BlockSpec((1, LANES), const),
            pl.BlockSpec((n_heads, 1), const),
            pl.BlockSpec((1, LANES), const),
            pl.BlockSpec((n_heads, 1), const),
            pl.BlockSpec((1, d_inner), const),
            pl.BlockSpec((1, d_inner), const),
            pl.BlockSpec((LANES, d_inner), const),
            pl.BlockSpec(memory_space=pl.ANY),
        ],
        out_specs=pl.BlockSpec((1, q, d_inner), lambda b, c: (slot, b * nc + c, 0)),
        out_shape=jax.ShapeDtypeStruct(ybuf.shape, ybuf.dtype),
        scratch_shapes=[pltpu.VMEM((q + 8, conv_dim), F32),
                        pltpu.VMEM((SSD_N_GROUPS, SSD_D_STATE, d_inner // SSD_N_GROUPS), F32)],
        input_output_aliases={13: 0},
        compiler_params=_cparams(("parallel", "arbitrary")),
        name="ssd_mixer",
    )(proj, proj, proj, dt_t, conv_w, conv_b.reshape(1, conv_dim), dtb, dt_bias.reshape(n_heads, 1),
      alog, a_log.reshape(n_heads, 1), jnp.repeat(d_skip, SSD_HEAD_DIM).reshape(1, d_inner),
      norm_g.reshape(1, d_inner), expand, ybuf)


S5_ROWS = 8


def _s5_disc_kernel(are_ref, aim_ref, ldt_ref, brt_ref, bit_ref, pr_ref, pi_ref, bbr_ref, bbi_ref):
    dt = jnp.exp(ldt_ref[...])
    ar, ai = are_ref[...], aim_ref[...]
    mag = jnp.exp(ar * dt)
    lr, li = mag * jnp.cos(ai * dt), mag * jnp.sin(ai * dt)
    den = ar * ar + ai * ai
    fr = ((lr - 1.0) * ar + li * ai) / den
    fi = (li * ar - (lr - 1.0) * ai) / den
    pr, pi = lr, li
    for k in range(S5_ROWS):
        pr_ref[k] = pr
        pi_ref[k] = pi
        pr, pi = pr * lr - pi * li, pr * li + pi * lr
    br, bi = brt_ref[...], bit_ref[...]
    bbr_ref[...] = fr[:, None, :] * br - fi[:, None, :] * bi
    bbi_ref[...] = fr[:, None, :] * bi + fi[:, None, :] * br


def s5_discretise(a_re, a_im, log_dt, b_re, b_im):
    g, n = a_re.shape
    c = b_re.shape[2]
    brt, bit = jnp.swapaxes(b_re, 1, 2), jnp.swapaxes(b_im, 1, 2)
    return pl.pallas_call(
        _s5_disc_kernel,
        out_shape=[jax.ShapeDtypeStruct((S5_ROWS, g, n), F32), jax.ShapeDtypeStruct((S5_ROWS, g, n), F32),
                   jax.ShapeDtypeStruct((g, c, n), F32), jax.ShapeDtypeStruct((g, c, n), F32)],
        name="s5_discretise",
    )(a_re, a_im, log_dt.reshape(g, 1), brt, bit)


def _block_diag(blocks, per):
    g, r, c = blocks.shape
    nb = g // per
    eye = jnp.eye(per, dtype=blocks.dtype)
    out = blocks.reshape(nb, per, r, 1, c) * eye[None, :, None, :, None]
    return out.reshape(nb, per * r, per * c)


def _s5_kernel(u_ref, wr_ref, wi_ref, cr_ref, ci_ref, ctab_ref, d_ref, wglu_ref, bglu_ref, ybuf_ref, o_ref,
               xr_scr, xi_scr, carry_ref, y_scr, *, nblk, sw):
    del ybuf_ref
    tm = u_ref.shape[0]

    @pl.when(pl.program_id(1) == 0)
    def _():
        carry_ref[...] = jnp.zeros(carry_ref.shape, F32)

    for blk in range(nblk):
        cs = slice(blk * sw, (blk + 1) * sw)
        ls = slice(blk * LANES, (blk + 1) * LANES)
        u = u_ref[:, ls]
        ub = u.astype(BF16)
        xr_scr[...] = _dot(ub, wr_ref[blk])
        xi_scr[...] = _dot(ub, wi_ref[blk])

        def body(i, carry, cs=cs):
            c_re, c_im = carry
            r0 = pl.multiple_of(i * S5_ROWS, S5_ROWS)
            sr, si = xr_scr[pl.ds(r0, S5_ROWS), :], xi_scr[pl.ds(r0, S5_ROWS), :]
            for lvl, dist in enumerate((1, 2, 4)):
                lr, li = ctab_ref[2 * lvl, :, cs], ctab_ref[2 * lvl + 1, :, cs]
                tr, ti = pltpu.roll(sr, dist, axis=0), pltpu.roll(si, dist, axis=0)
                sr, si = sr + lr * tr - li * ti, si + lr * ti + li * tr
            pr, pi = ctab_ref[6, :, cs], ctab_ref[7, :, cs]
            sr, si = sr + pr * c_re - pi * c_im, si + pr * c_im + pi * c_re
            xr_scr[pl.ds(r0, S5_ROWS), :] = sr
            xi_scr[pl.ds(r0, S5_ROWS), :] = si
            return sr[S5_ROWS - 1:S5_ROWS, :], si[S5_ROWS - 1:S5_ROWS, :]

        c_re, c_im = lax.fori_loop(0, tm // S5_ROWS, body, (carry_ref[0:1, cs], carry_ref[1:2, cs]), unroll=2)
        carry_ref[0:1, cs] = c_re
        carry_ref[1:2, cs] = c_im
        y = _dot(xr_scr[...].astype(BF16), cr_ref[blk]) - _dot(xi_scr[...].astype(BF16), ci_ref[blk])
        y_scr[:, ls] = jax.nn.gelu(y + d_ref[:, ls] * u)
    y = y_scr[...]
    o_ref[0] = (y * jax.nn.sigmoid(_dot(y.astype(BF16), wglu_ref[...]) + bglu_ref[...])).astype(o_ref.dtype)


def s5_mixer(proj, u_col, bsz, seq, a_re, a_im, b_re, b_im, c_re, c_im, log_dt, d_skip, w_glu, b_glu,
             ybuf, slot):
    t = bsz * seq
    g, n = a_re.shape
    width = g * S5_GROUP
    per = S5_GROUPS_PER_BLOCK
    nblk = g // per
    sw = per * n
    ns = g * n
    pw_r, pw_i, bbr, bbi = s5_discretise(a_re, a_im, log_dt, b_re, b_im)
    w_r = _block_diag(bbr, per).astype(BF16)
    w_i = _block_diag(bbi, per).astype(BF16)
    c_r = _block_diag(jnp.swapaxes(c_re, 1, 2), per).astype(BF16)
    c_i = _block_diag(jnp.swapaxes(c_im, 1, 2), per).astype(BF16)
    pw_r, pw_i = pw_r.reshape(S5_ROWS, ns), pw_i.reshape(S5_ROWS, ns)
    sub = jnp.arange(S5_ROWS)[:, None]
    kinds = []
    for dist in (1, 2, 4):
        kinds += [jnp.where(sub >= dist, pw_r[dist - 1][None, :], 0.0),
                  jnp.where(sub >= dist, pw_i[dist - 1][None, :], 0.0)]
    ctab = jnp.stack(kinds + [pw_r, pw_i])
    tm = _tile(seq, 256)
    nt = seq // tm
    const2 = lambda b, c: (0, 0)
    const3 = lambda b, c: (0, 0, 0)
    return pl.pallas_call(
        functools.partial(_s5_kernel, nblk=nblk, sw=sw),
        grid=(bsz, nt),
        in_specs=[pl.BlockSpec((tm, width), lambda b, c: (b * nt + c, u_col // width)),
                  pl.BlockSpec((nblk, LANES, sw), const3), pl.BlockSpec((nblk, LANES, sw), const3),
                  pl.BlockSpec((nblk, sw, LANES), const3), pl.BlockSpec((nblk, sw, LANES), const3),
                  pl.BlockSpec((8, S5_ROWS, ns), const3),
                  pl.BlockSpec((1, width), const2), pl.BlockSpec((width, width), const2),
                  pl.BlockSpec((1, width), const2), pl.BlockSpec(memory_space=pl.ANY)],
        out_specs=pl.BlockSpec((1, tm, width), lambda b, c: (slot, b * nt + c, 0)),
        out_shape=jax.ShapeDtypeStruct(ybuf.shape, ybuf.dtype),
        scratch_shapes=[pltpu.VMEM((tm, sw), F32), pltpu.VMEM((tm, sw), F32), pltpu.VMEM((8, ns), F32),
                        pltpu.VMEM((tm, width), F32)],
        input_output_aliases={9: 0},
        compiler_params=_cparams(("parallel", "arbitrary")),
        name="s5_mixer",
    )(proj, w_r, w_i, c_r, c_i, ctab, d_skip.reshape(1, width), w_glu.astype(BF16), b_glu.reshape(1, width),
      ybuf)


def _flash_kernel(*refs, tq, n_maps, has_bias, finalize):
    if has_bias:
        q_ref, k_ref, v_ref, bias_ref, far_ref, *rest = refs
    else:
        q_ref, k_ref, v_ref, *rest = refs
    fin_refs = rest[:-2]
    o_ref = rest[-1]
    h = pl.program_id(1)
    qi = pl.program_id(2)
    rows = n_maps * tq
    dvo = v_ref.shape[1]
    qv = q_ref[...].reshape(rows, q_ref.shape[-1])

    def tiled(b):
        return jnp.concatenate([b] * n_maps, axis=0) if n_maps > 1 else b

    def block(kb, carry, add, shift):
        m, l, acc = carry
        off = pl.multiple_of(kb * tq, tq)
        s = lax.dot_general(qv, k_ref[pl.ds(off, tq), :], (((1,), (1,)), ((), ())), preferred_element_type=F32)
        sb = s.astype(BF16)
        if add is not None:
            sb = sb + add
        mx = jnp.max(sb, axis=-1, keepdims=True).astype(F32)
        m_hi = jnp.maximum(m, mx if shift is None else mx + shift)
        ref_b = (m_hi if shift is None else m_hi - shift).astype(BF16)
        m_new = ref_b.astype(F32) if shift is None else ref_b.astype(F32) + shift
        alpha = jnp.exp2(m - m_new)
        p = jnp.exp2(sb - ref_b)
        l_new = alpha * l + jnp.sum(p.astype(F32), axis=-1, keepdims=True)
        return m_new, l_new, alpha * acc + _dot(p, v_ref[pl.ds(off, tq), :])

    carry = (jnp.full((rows, 1), -jnp.inf, F32), jnp.zeros((rows, 1), F32), jnp.zeros((rows, dvo), F32))
    if has_bias:
        far = far_ref[h]
        n_far = jnp.maximum(qi - 1, 0)
        carry = lax.fori_loop(0, n_far, lambda kb, c: block(kb, c, None, far), carry)
        carry = lax.fori_loop(n_far, qi, lambda kb, c: block(kb, c, tiled(bias_ref[0, 1]), None), carry)
    else:
        carry = lax.fori_loop(0, qi, lambda kb, c: block(kb, c, None, None), carry)
    row = lax.broadcasted_iota(jnp.int32, (tq, tq), 0)
    col = lax.broadcasted_iota(jnp.int32, (tq, tq), 1)
    diag = jnp.where(row >= col, bias_ref[0, 0].astype(F32) if has_bias else 0.0, -jnp.inf).astype(BF16)
    m, l, acc = block(qi, carry, tiled(diag), None)
    o_ref[0] = finalize(acc / l, [r[...] for r in fin_refs]).astype(o_ref.dtype)


def flash_attention(q, k, v, bsz, seq, n_heads, dq, ybuf, slot, n_maps=1, bias=None, far=None, fin=(),
                    finalize=None, tq=512):
    dv = v.shape[1] // n_heads
    tq = _tile(seq, tq)
    nq = seq // tq
    has_bias = bias is not None
    if finalize is None:
        finalize = lambda o, extras: o
    in_specs = [pl.BlockSpec((n_maps, tq, dq), lambda b, h, i: (0, b * nq + i, h)),
                pl.BlockSpec((seq, dq), lambda b, h, i: (b, h)),
                pl.BlockSpec((seq, dv), lambda b, h, i: (b, h))]
    args = [q, k, v]
    if has_bias:
        in_specs += [pl.BlockSpec((1, 2, tq, tq), lambda b, h, i: (h, 0, 0, 0)),
                     pl.BlockSpec(memory_space=pltpu.SMEM)]
        args += [bias, far]
    for f in fin:
        in_specs.append(pl.BlockSpec(f.shape, lambda b, h, i: (0, 0)))
        args.append(f)
    in_specs.append(pl.BlockSpec(memory_space=pl.ANY))
    args.append(ybuf)
    return pl.pallas_call(
        functools.partial(_flash_kernel, tq=tq, n_maps=n_maps, has_bias=has_bias, finalize=finalize),
        grid=(bsz, n_heads, nq),
        in_specs=in_specs,
        out_specs=pl.BlockSpec((1, tq, dv), lambda b, h, i: (slot, b * nq + i, h)),
        out_shape=jax.ShapeDtypeStruct(ybuf.shape, ybuf.dtype),
        input_output_aliases={len(args) - 1: 0},
        compiler_params=_cparams(("parallel", "parallel", "arbitrary")),
        name="flash_bias" if has_bias else "flash",
    )(*args)


def _mla_proj_kernel(cq_ref, ckv_ref, kpe_ref, cos_ref, sin_ref, qng_ref, kvng_ref, wq_ref, wkv_ref,
                     qg_ref, kg_ref, q_ref, k_ref, v_ref, *, n_heads, scale):
    half = LANES // 2

    def norm(x, g):
        return (x * lax.rsqrt(jnp.mean(x * x, axis=-1, keepdims=True) + RMS_EPS) * g).astype(BF16)

    cq = norm(cq_ref[...], qng_ref[...])
    ckv = norm(ckv_ref[...], kvng_ref[...])
    cos, sin = cos_ref[...], sin_ref[...]
    kpe = kpe_ref[...]
    kpe_ss = jnp.sum(jnp.where(lax.broadcasted_iota(jnp.int32, kpe.shape, 1) < half, kpe * kpe, 0.0),
                     axis=-1, keepdims=True)
    n_qk = float(MLA_NOPE + MLA_ROPE)
    qg, kg = qg_ref[...], kg_ref[...]

    def rotary(hi, gains, inv):
        return inv * (hi * (gains[1:2] * cos) + pltpu.roll(hi, half, axis=1) * (gains[2:3] * sin))

    k_rot_raw = kpe * (kg[1:2] * cos) + pltpu.roll(kpe, half, axis=1) * (kg[2:3] * sin)
    for h in range(n_heads):
        xq = _dot(cq, wq_ref[:, h * MLA_PAD:(h + 1) * MLA_PAD])
        lo, hi = xq[:, :LANES], xq[:, LANES:]
        ss = jnp.sum(lo * lo, axis=-1, keepdims=True) + 0.5 * jnp.sum(hi * hi, axis=-1, keepdims=True)
        inv = lax.rsqrt(ss / n_qk + RMS_EPS) * scale
        q_ref[:, h * MLA_PAD:h * MLA_PAD + LANES] = (lo * inv * qg[0:1]).astype(BF16)
        q_ref[:, h * MLA_PAD + LANES:(h + 1) * MLA_PAD] = rotary(hi, qg, inv).astype(BF16)
        xkv = _dot(ckv, wkv_ref[:, h * MLA_PAD:(h + 1) * MLA_PAD])
        kn = xkv[:, :LANES]
        inv_k = lax.rsqrt((jnp.sum(kn * kn, axis=-1, keepdims=True) + kpe_ss) / n_qk + RMS_EPS)
        k_ref[:, h * MLA_PAD:h * MLA_PAD + LANES] = (kn * inv_k * kg[0:1]).astype(BF16)
        k_ref[:, h * MLA_PAD + LANES:(h + 1) * MLA_PAD] = (k_rot_raw * inv_k).astype(BF16)
        v_ref[:, h * LANES:(h + 1) * LANES] = xkv[:, LANES:].astype(BF16)


def _rope_gain_rows(gain):
    rot = gain[MLA_NOPE:]
    r = MLA_ROPE // 2
    swapped = jnp.concatenate([rot[r:], rot[:r]])
    z = jnp.zeros((LANES - MLA_ROPE,), F32)
    return jnp.stack([gain[:MLA_NOPE], jnp.concatenate([rot, z]), jnp.concatenate([swapped, z])])


def mla_mixer(proj, cols, bsz, seq, q_norm_g, w_q_up, kv_norm_g, w_kv_up, q_gain, k_gain, ybuf, slot):
    t = bsz * seq
    hds = MLA_N_HEADS
    q_rank, kv_rank = w_q_up.shape[0], w_kv_up.shape[0]
    r = MLA_ROPE // 2
    wq = w_q_up.reshape(q_rank, hds, MLA_NOPE + MLA_ROPE)
    wq = jnp.concatenate([wq, wq[:, :, MLA_NOPE + r:], wq[:, :, MLA_NOPE:MLA_NOPE + r]], axis=-1)
    wq = wq.reshape(q_rank, hds * MLA_PAD).astype(BF16)
    wkv = w_kv_up.astype(BF16)
    inv = 1.0 / (ROPE_THETA ** (jnp.arange(0, MLA_ROPE, 2, dtype=F32) / MLA_ROPE))
    ang = jnp.arange(seq, dtype=F32)[:, None] * inv[None, :]
    zer = jnp.zeros((seq, LANES - MLA_ROPE), F32)
    cos_t = jnp.concatenate([jnp.cos(ang), jnp.cos(ang), zer], axis=-1)
    sin_t = jnp.concatenate([-jnp.sin(ang), jnp.sin(ang), zer], axis=-1)
    tm = _tile(seq, 512)
    npos = seq // tm
    const = lambda i: (0, 0)
    q, k, v = pl.pallas_call(
        functools.partial(_mla_proj_kernel, n_heads=hds, scale=LOG2E * float(MLA_NOPE + MLA_ROPE) ** -0.5),
        grid=(t // tm,),
        in_specs=[pl.BlockSpec((tm, q_rank), lambda i: (i, cols["cq"] // q_rank)),
                  pl.BlockSpec((tm, kv_rank), lambda i: (i, cols["ckv"] // kv_rank)),
                  pl.BlockSpec((tm, LANES), lambda i: (i, cols["kpe"] // LANES)),
                  pl.BlockSpec((tm, LANES), lambda i: (i % npos, 0)),
                  pl.BlockSpec((tm, LANES), lambda i: (i % npos, 0)),
                  pl.BlockSpec((1, q_rank), const), pl.BlockSpec((1, kv_rank), const),
                  pl.BlockSpec((q_rank, hds * MLA_PAD), const), pl.BlockSpec((kv_rank, hds * MLA_PAD), const),
                  pl.BlockSpec((3, LANES), const), pl.BlockSpec((3, LANES), const)],
        out_specs=[pl.BlockSpec((tm, hds * MLA_PAD), lambda i: (i, 0)),
                   pl.BlockSpec((tm, hds * MLA_PAD), lambda i: (i, 0)),
                   pl.BlockSpec((tm, hds * LANES), lambda i: (i, 0))],
        out_shape=[jax.ShapeDtypeStruct((t, hds * MLA_PAD), BF16), jax.ShapeDtypeStruct((t, hds * MLA_PAD), BF16),
                   jax.ShapeDtypeStruct((t, hds * LANES), BF16)],
        compiler_params=_cparams(("parallel",)),
        name="mla_proj",
    )(proj, proj, proj, cos_t, sin_t, q_norm_g.reshape(1, q_rank), kv_norm_g.reshape(1, kv_rank), wq, wkv,
      _rope_gain_rows(q_gain), _rope_gain_rows(k_gain))
    return flash_attention(q.reshape(1, t, hds * MLA_PAD), k, v, bsz, seq, hds, MLA_PAD, ybuf, slot)


def _diff_prep_kernel(q_ref, k_ref, v_ref, qg_ref, kg_ref, seg_ref, qo_ref, ko_ref, vo_ref, *, hd, scale):
    seg = seg_ref[...]
    lane = lax.broadcasted_iota(jnp.int32, (1, LANES), 1)
    n_col = q_ref.shape[1] // LANES
    for c in range(n_col):
        sl = slice(c * LANES, (c + 1) * LANES)
        xq, xk = q_ref[:, sl], k_ref[:, sl]
        ssq = _dot_exact_rhs01(xq * xq, seg)
        ssk = _dot_exact_rhs01(xk * xk, seg)
        qn = xq * lax.rsqrt(ssq / hd + RMS_EPS) * (qg_ref[...] * scale)
        kn = xk * lax.rsqrt(ssk / hd + RMS_EPS) * kg_ref[...]
        qo_ref[0, :, sl] = jnp.where(lane < hd, qn, 0.0).astype(BF16)
        qo_ref[1, :, sl] = jnp.where(lane >= hd, qn, 0.0).astype(BF16)
        ko_ref[:, sl] = kn.astype(BF16)
    vo_ref[...] = v_ref[...].astype(BF16)


def _t5_bucket_np(n):
    max_exact = REL_BUCKETS // 2
    nf = np.maximum(n, 1).astype(np.float32)
    large = max_exact + (np.log(nf / np.float32(max_exact)) / np.float32(math.log(REL_MAX_DIST / max_exact))
                         * np.float32(REL_BUCKETS - max_exact)).astype(np.int32)
    large = np.minimum(large, REL_BUCKETS - 1)
    return np.where(n < max_exact, n, large).astype(np.int32)


def _bias_tiles_kernel(bkt_ref, rb_ref, o_ref):
    h = pl.program_id(0)
    for d in range(2):
        bkt = bkt_ref[d]
        tile = jnp.zeros(bkt.shape, F32)
        for b in range(REL_BUCKETS):
            tile = jnp.where(bkt == b, rb_ref[b, h] * LOG2E, tile)
        o_ref[0, d] = tile.astype(o_ref.dtype)


def _diff_finalize(lambda_init, tq):
    def fin(o, extras):
        lq1, lk1, lq2, lk2, sub_g = extras
        lam = (jnp.exp(jnp.sum(lq1 * lk1, axis=-1, keepdims=True))
               - jnp.exp(jnp.sum(lq2 * lk2, axis=-1, keepdims=True)) + lambda_init)
        d = o[:tq] - lam * o[tq:]
        ms = jnp.mean(d * d, axis=-1, keepdims=True)
        return d * lax.rsqrt(ms + RMS_EPS) * (sub_g * (1.0 - lambda_init))
    return fin


def diff_mixer(proj, cols, bsz, seq, q_gain, k_gain, lq1, lk1, lq2, lk2, subln_g, rel_bias, lambda_init,
               ybuf, slot):
    t = bsz * seq
    hds = DIFF_N_HEADS
    hd = q_gain.shape[0]
    width = hds * 2 * hd
    tm = _tile(t, 512)
    seg = (np.arange(LANES)[:, None] // hd == np.arange(LANES)[None, :] // hd)
    seg = jnp.asarray(seg, BF16)
    gq, gk = jnp.tile(q_gain, LANES // hd).reshape(1, LANES), jnp.tile(k_gain, LANES // hd).reshape(1, LANES)
    const = lambda i: (0, 0)
    qd, kd, vd = pl.pallas_call(
        functools.partial(_diff_prep_kernel, hd=hd, scale=LOG2E * float(hd) ** -0.5),
        grid=(t // tm,),
        in_specs=[pl.BlockSpec((tm, width), lambda i: (i, cols["dq"] // width)),
                  pl.BlockSpec((tm, width), lambda i: (i, cols["dk"] // width)),
                  pl.BlockSpec((tm, width), lambda i: (i, cols["dv"] // width)),
                  pl.BlockSpec((1, LANES), const), pl.BlockSpec((1, LANES), const),
                  pl.BlockSpec((LANES, LANES), const)],
        out_specs=[pl.BlockSpec((2, tm, width), lambda i: (0, i, 0)),
                   pl.BlockSpec((tm, width), lambda i: (i, 0)), pl.BlockSpec((tm, width), lambda i: (i, 0))],
        out_shape=[jax.ShapeDtypeStruct((2, t, width), BF16), jax.ShapeDtypeStruct((t, width), BF16),
                   jax.ShapeDtypeStruct((t, width), BF16)],
        compiler_params=_cparams(("parallel",)),
        name="diff_prep",
    )(proj, proj, proj, gq, gk, seg)

    tq = _tile(seq, 512)
    assert tq + 1 > REL_MAX_DIST, "blocks two or more before the diagonal must all be at the far distance"
    idx = np.arange(tq)
    dist0 = np.maximum(idx[:, None] - idx[None, :], 0)
    dist1 = idx[:, None] - idx[None, :] + tq
    bkt = jnp.asarray(np.stack([_t5_bucket_np(dist0), _t5_bucket_np(dist1)]))
    bias = pl.pallas_call(
        _bias_tiles_kernel,
        grid=(hds,),
        in_specs=[pl.BlockSpec((2, tq, tq), lambda h: (0, 0, 0)), pl.BlockSpec(memory_space=pltpu.SMEM)],
        out_specs=pl.BlockSpec((1, 2, tq, tq), lambda h: (h, 0, 0, 0)),
        out_shape=jax.ShapeDtypeStruct((hds, 2, tq, tq), BF16),
        compiler_params=_cparams(("parallel",)),
        name="t5_bias_tiles",
    )(bkt, rel_bias)
    far = rel_bias[REL_BUCKETS - 1] * LOG2E
    fin = tuple(a.reshape(1, -1) for a in (lq1, lk1, lq2, lk2, subln_g))
    return flash_attention(qd, kd, vd, bsz, seq, hds, 2 * hd, ybuf, slot, n_maps=2, bias=bias, far=far, fin=fin,
                           finalize=_diff_finalize(lambda_init, tq), tq=tq)


def _pack_w_in(w_in_l, widths):
    (z_w, xbc_w, dt_w, u_w, cq_w, ckv_w, kpe_w, dq_w, dk_w, dv_w) = widths
    lead = w_in_l.shape[:-1]
    w_in_l = w_in_l.astype(BF16)
    starts = np.cumsum([0] + list(widths))
    seg = {n: w_in_l[..., starts[i]:starts[i + 1]]
           for i, n in enumerate(("z", "xbc", "dt", "u", "cq", "ckv", "kpe", "dq", "dk", "dv"))}
    r = kpe_w // 2
    kpe2 = jnp.concatenate([seg["kpe"], seg["kpe"][..., r:], seg["kpe"][..., :r],
                            jnp.zeros(lead + (LANES - 2 * kpe_w,), BF16)], axis=-1)
    dtp = jnp.concatenate([seg["dt"], jnp.zeros(lead + (LANES - dt_w,), BF16)], axis=-1)
    order = [("xbc", seg["xbc"]), ("z", seg["z"]), ("u", seg["u"]), ("dq", seg["dq"]), ("dk", seg["dk"]),
             ("dv", seg["dv"]), ("ckv", seg["ckv"]), ("cq", seg["cq"]), ("kpe", kpe2), ("dt", dtp)]
    cols, off = {}, 0
    for name, w in order:
        assert off % w.shape[-1] == 0, (name, off, w.shape[-1])
        cols[name] = off
        off += w.shape[-1]
    return jnp.concatenate([w for _, w in order], axis=-1), cols


def kernel(x, norm_mix_g, w_in, ssd_conv_w, ssd_conv_b, ssd_dt_bias, ssd_a_log, ssd_d, ssd_norm_g, s5_a_re, s5_a_im, s5_b_re, s5_b_im, s5_c_re, s5_c_im, s5_log_dt, s5_d, s5_w_glu, s5_b_glu, mla_q_norm_g, mla_w_q_up, mla_kv_norm_g, mla_w_kv_up, mla_q_gain, mla_k_gain, diff_q_gain, diff_k_gain, diff_lq1, diff_lk1, diff_lq2, diff_lk2, diff_subln_g, rel_bias, w_gate, b_gate, w_branch, w_out, norm_ffn_g, ffn_w_gate, ffn_w_up, ffn_w_down, moe_w_router, moe_b_router, moe_w_gate, moe_w_up, moe_w_down):
    bsz, seq, d = x.shape
    t = bsz * seq
    depth = w_in.shape[0]
    mix = d // 4
    n_ssd_heads = ssd_dt_bias.shape[1]
    widths = (mix, ssd_conv_w.shape[2], n_ssd_heads, mix, mla_w_q_up.shape[1], mla_w_kv_up.shape[1], MLA_ROPE,
              mix, mix, mix)
    xf = x.reshape(t, d)
    w_in_b, cols = _pack_w_in(w_in, widths)
    w_gate_b, w_branch_b, w_out_b = w_gate.astype(BF16), w_branch.astype(BF16), w_out.astype(BF16)
    ffn_gate_b, ffn_up_b, ffn_down_b = ffn_w_gate.astype(BF16), ffn_w_up.astype(BF16), ffn_w_down.astype(BF16)
    for i in range(depth):
        h = rmsnorm(xf, norm_mix_g[i])
        proj = matmul(h, [w_in_b], _epi_plain, out_dtype=F32, tm=2048, tn=512, w_prefix=(i,), name="in_proj")
        dt_t = jnp.swapaxes(proj[:, cols["dt"]:cols["dt"] + n_ssd_heads].reshape(bsz, seq, n_ssd_heads), 1, 2)
        ys = ssd_mixer(proj, dt_t, cols, bsz, seq, ssd_conv_w[i], ssd_conv_b[i], ssd_dt_bias[i], ssd_a_log[i],
                       ssd_d[i], ssd_norm_g[i], jnp.zeros((N_BRANCH, t, mix), BF16), 0)
        ys = s5_mixer(proj, cols["u"], bsz, seq, s5_a_re[i], s5_a_im[i], s5_b_re[i], s5_b_im[i], s5_c_re[i],
                      s5_c_im[i], s5_log_dt[i], s5_d[i], s5_w_glu[i], s5_b_glu[i], ys, 1)
        ys = mla_mixer(proj, cols, bsz, seq, mla_q_norm_g[i], mla_w_q_up[i], mla_kv_norm_g[i], mla_w_kv_up[i],
                       mla_q_gain[i], mla_k_gain[i], ys, 2)
        lambda_init = 0.8 - 0.6 * math.exp(-0.3 * i)
        ys = diff_mixer(proj, cols, bsz, seq, diff_q_gain[i], diff_k_gain[i], diff_lq1[i], diff_lk1[i],
                        diff_lq2[i], diff_lk2[i], diff_subln_g[i], rel_bias, lambda_init, ys, 3)
        merged = gated_merge(h, w_gate_b, b_gate[i][:, None, :], ys, w_branch_b, i)
        xf = matmul(merged, [w_out_b], _epi_residual, extras=(xf,), extra_kinds=("mn",), w_prefix=(i,),
                    name="out_proj")
        e = i // 2
        if i % 2 == 0:
            h2 = rmsnorm(xf, norm_ffn_g[i])
            act = matmul(h2, [ffn_gate_b, ffn_up_b], _epi_swiglu, out_dtype=BF16, tn=512, w_prefix=(e,),
                         name="ffn_up")
            xf = matmul(act, [ffn_down_b], _epi_residual, extras=(xf,), extra_kinds=("mn",), tk=3584,
                        w_prefix=(e,), name="ffn_down")
        else:
            xf = moe_block(xf, norm_ffn_g[i], moe_w_router[e], moe_b_router[e], moe_w_gate[e], moe_w_up[e],
                           moe_w_down[e])
    return xf.reshape(bsz, seq, d)
```

```python
import functools
import math

import numpy as np
import jax
import jax.numpy as jnp
from jax import lax
from jax.experimental import pallas as pl
from jax.experimental.pallas import tpu as pltpu

F32 = jnp.float32
BF16 = jnp.bfloat16

V7X_VMEM_BYTES = 64 * 1024 * 1024
VMEM_LIMIT = V7X_VMEM_BYTES - 8 * 1024 * 1024
LANES = 128

RMS_EPS = 1e-6
SSD_HEAD_DIM = 64
SSD_N_GROUPS = 4
SSD_D_STATE = 128
SSD_CHUNK = 128
S5_GROUP = 16
S5_STATE = 64
S5_GROUPS_PER_BLOCK = 8
MLA_N_HEADS = 8
MLA_NOPE = 128
MLA_ROPE = 64
MLA_PAD = 256
ROPE_THETA = 10000.0
DIFF_N_HEADS = 8
REL_BUCKETS = 32
REL_MAX_DIST = 128
TOP_K = 2
N_BRANCH = 4
LOG2E = math.log2(math.e)


def _cparams(sem):
    return pltpu.CompilerParams(dimension_semantics=sem, vmem_limit_bytes=VMEM_LIMIT)


def _tile(n, pref):
    t = min(n, pref)
    while n % t:
        t //= 2
    return t


def _split3(a):
    hi = a.astype(BF16)
    r1 = a - hi.astype(F32)
    mid = r1.astype(BF16)
    lo = (r1 - mid.astype(F32)).astype(BF16)
    return hi, mid, lo


def _dot(a, b):
    return jnp.dot(a, b, preferred_element_type=F32)


def _dot_exact_rhs01(a, sel):
    hi, mid, lo = _split3(a)
    return _dot(hi, sel) + _dot(mid, sel) + _dot(lo, sel)


def _dot_exact_lhs01(sel, b):
    hi, mid, lo = _split3(b)
    return _dot(sel, hi) + _dot(sel, mid) + _dot(sel, lo)


def _rmsnorm_kernel(x_ref, g_ref, o_ref):
    x = x_ref[...]
    ms = jnp.mean(x * x, axis=-1, keepdims=True)
    o_ref[...] = (x * lax.rsqrt(ms + RMS_EPS) * g_ref[...]).astype(o_ref.dtype)


def rmsnorm(x, g, tm=256):
    t, d = x.shape
    tm = _tile(t, tm)
    return pl.pallas_call(
        _rmsnorm_kernel,
        grid=(t // tm,),
        in_specs=[pl.BlockSpec((tm, d), lambda i: (i, 0)), pl.BlockSpec((1, d), lambda i: (0, 0))],
        out_specs=pl.BlockSpec((tm, d), lambda i: (i, 0)),
        out_shape=jax.ShapeDtypeStruct((t, d), BF16),
        compiler_params=_cparams(("parallel",)),
        name="rmsnorm",
    )(x, g.reshape(1, d))


def _rmsnorm_router_kernel(x_ref, g_ref, wr_ref, br_ref, hp_ref, route_ref, cnt_ref, run_ref, *, n_experts):
    @pl.when(pl.program_id(0) == 0)
    def _():
        run_ref[...] = jnp.zeros(run_ref.shape, F32)

    x = x_ref[...]
    tm, d = x.shape
    ms = jnp.mean(x * x, axis=-1, keepdims=True)
    h = x * lax.rsqrt(ms + RMS_EPS) * g_ref[...]
    bits = lax.bitcast_convert_type(h.astype(BF16).astype(F32), jnp.uint32)
    hp_ref[...] = (bits[:, :d // 2] >> 16) | (bits[:, d // 2:] & jnp.uint32(0xFFFF0000))
    logits = jnp.dot(h, wr_ref[...], preferred_element_type=F32, precision=lax.Precision.HIGHEST) + br_ref[...]
    lane = lax.broadcasted_iota(jnp.int32, logits.shape, 1)
    neg = jnp.float32(-jnp.inf)
    logits = jnp.where(lane < n_experts, logits, neg)
    m1 = jnp.max(logits, axis=-1, keepdims=True)
    i1 = jnp.min(jnp.where(logits == m1, lane, LANES), axis=-1, keepdims=True)
    rest = jnp.where(lane == i1, neg, logits)
    m2 = jnp.max(rest, axis=-1, keepdims=True)
    i2 = jnp.min(jnp.where(rest == m2, lane, LANES), axis=-1, keepdims=True)
    e2 = jnp.exp(m2 - m1)
    w1 = 1.0 / (1.0 + e2)
    w2 = e2 / (1.0 + e2)
    sel = jnp.where(lane == i1, 1.0, 0.0) + jnp.where(lane == i2, 1.0, 0.0)
    row = lax.broadcasted_iota(jnp.int32, (tm, tm), 0)
    col = lax.broadcasted_iota(jnp.int32, (tm, tm), 1)
    before = jnp.where(col < row, 1.0, 0.0).astype(BF16)
    excl = _dot(before, sel.astype(BF16)) + run_ref[0:1, :]
    run_ref[...] = run_ref[...] + jnp.sum(sel, axis=0, keepdims=True)
    cnt_ref[...] = run_ref[...]
    r1 = jnp.sum(jnp.where(lane == i1, excl, 0.0), axis=-1, keepdims=True)
    r2 = jnp.sum(jnp.where(lane == i2, excl, 0.0), axis=-1, keepdims=True)
    route = jnp.zeros(logits.shape, F32)
    for pos, val in enumerate((i1.astype(F32), i2.astype(F32), r1, r2, w1, w2)):
        route = jnp.where(lane == pos, val, route)
    route_ref[...] = route


def rmsnorm_router(x, g, w_router, b_router, tm=256):
    t, d = x.shape
    ne = w_router.shape[1]
    tm = _tile(t, tm)
    wr = jnp.zeros((d, LANES), F32).at[:, :ne].set(w_router)
    br = jnp.zeros((1, LANES), F32).at[0, :ne].set(b_router)
    return pl.pallas_call(
        functools.partial(_rmsnorm_router_kernel, n_experts=ne),
        grid=(t // tm,),
        in_specs=[pl.BlockSpec((tm, d), lambda i: (i, 0)), pl.BlockSpec((1, d), lambda i: (0, 0)),
                  pl.BlockSpec((d, LANES), lambda i: (0, 0)), pl.BlockSpec((1, LANES), lambda i: (0, 0))],
        out_specs=[pl.BlockSpec((tm, d // 2), lambda i: (i, 0)), pl.BlockSpec((tm, LANES), lambda i: (i, 0)),
                   pl.BlockSpec((8, LANES), lambda i: (0, 0))],
        out_shape=[jax.ShapeDtypeStruct((t, d // 2), jnp.uint32), jax.ShapeDtypeStruct((t, LANES), F32),
                   jax.ShapeDtypeStruct((8, LANES), F32)],
        scratch_shapes=[pltpu.VMEM((8, LANES), F32)],
        compiler_params=_cparams(("arbitrary",)),
        name="rmsnorm_router",
    )(x, g.reshape(1, d), wr, br)


MOE_ROW_TILE = 512
MOE_TOKEN_BLOCK = 256


def _dispatch_kernel(src_ref, h_hbm, o_ref, sem, *, tmg):
    def start(r2, c):
        for p in range(2):
            r = 2 * r2 + p
            pltpu.make_async_copy(h_hbm.at[pl.ds(src_ref[0, 0, r], 1)], o_ref.at[pl.ds(r, 1)], sem).start(
                priority=p)
        return c

    lax.fori_loop(0, tmg // 2, start, 0)

    def wait(r, c):
        pltpu.make_async_copy(h_hbm.at[pl.ds(0, 1)], o_ref.at[pl.ds(0, 1)], sem).wait()
        return c

    lax.fori_loop(0, tmg, wait, 0)


def moe_dispatch(hp, src, n_rows):
    dw = hp.shape[1]
    tmg = MOE_ROW_TILE
    return pl.pallas_call(
        functools.partial(_dispatch_kernel, tmg=tmg),
        grid=(n_rows // tmg,),
        in_specs=[pl.BlockSpec((1, 1, tmg), lambda i: (i, 0, 0), memory_space=pltpu.SMEM),
                  pl.BlockSpec(memory_space=pl.ANY)],
        out_specs=pl.BlockSpec((tmg, dw), lambda i: (i, 0)),
        out_shape=jax.ShapeDtypeStruct((n_rows, dw), hp.dtype),
        scratch_shapes=[pltpu.SemaphoreType.DMA(())],
        compiler_params=_cparams(("arbitrary",)),
        name="moe_dispatch",
    )(src.reshape(n_rows // tmg, 1, tmg), hp)


def _unpack_halves(xp):
    lo = lax.bitcast_convert_type(xp << 16, F32).astype(BF16)
    hi = lax.bitcast_convert_type(xp & jnp.uint32(0xFFFF0000), F32).astype(BF16)
    return lo, hi


def _grouped_kernel(e_tab, jw_tab, rx_tab, j_tab, r_tab, first_tab, valid_tab, x_ref, *rest, n_w, packed,
                    epilogue):
    del e_tab, jw_tab, rx_tab, j_tab, r_tab
    w_refs, o_ref, wb_refs = rest[:n_w], rest[n_w], rest[n_w + 1:]
    s = pl.program_id(0)

    @pl.when(first_tab[s] == 1)
    def _():
        for w, wb in zip(w_refs, wb_refs):
            wb[...] = w[0].astype(BF16)

    @pl.when(valid_tab[s] == 1)
    def _():
        if packed:
            lo, hi = _unpack_halves(x_ref[...])
            half = lo.shape[1]
            parts = [_dot(lo, wb[:half, :]) + _dot(hi, wb[half:, :]) for wb in wb_refs]
        else:
            x = x_ref[...]
            parts = [_dot(x, wb[...]) for wb in wb_refs]
        o_ref[...] = epilogue(parts, []).astype(o_ref.dtype)

    @pl.when(valid_tab[s] == 0)
    def _():
        o_ref[...] = jnp.zeros(o_ref.shape, o_ref.dtype)


def grouped_matmul(x, ws, tabs, epilogue, out_dtype, tn, packed, name):
    n_rows, kx = x.shape
    ne, kdim, n = ws[0].shape
    tmg = MOE_ROW_TILE
    tn = _tile(n, tn)
    n_steps = tabs[0].shape[0]
    grid_spec = pltpu.PrefetchScalarGridSpec(
        num_scalar_prefetch=7,
        grid=(n_steps,),
        in_specs=[pl.BlockSpec((tmg, kx), lambda s, e, jw, rx, j, r, f, v: (rx[s], 0))]
        + [pl.BlockSpec((1, kdim, tn), lambda s, e, jw, rx, j, r, f, v: (e[s], 0, jw[s])) for _ in ws],
        out_specs=pl.BlockSpec((tmg, tn), lambda s, e, jw, rx, j, r, f, v: (r[s], j[s])),
        scratch_shapes=[pltpu.VMEM((kdim, tn), BF16) for _ in ws],
    )
    return pl.pallas_call(
        functools.partial(_grouped_kernel, n_w=len(ws), packed=packed, epilogue=epilogue),
        grid_spec=grid_spec,
        out_shape=jax.ShapeDtypeStruct((n_rows, n), out_dtype),
        compiler_params=_cparams(("arbitrary",)),
        name=name,
    )(*tabs, x, *ws)


def _group_steps(padded, nj, n_tiles):
    tmg = MOE_ROW_TILE
    tiles = padded // tmg
    tile_start = jnp.cumsum(tiles) - tiles
    steps = tiles * nj
    step_end = jnp.cumsum(steps)
    step_start = step_end - steps
    total = step_end[-1]
    used_tiles = jnp.sum(tiles)
    s = jnp.arange(n_tiles * nj, dtype=jnp.int32)
    sc = jnp.minimum(s, total - 1)
    e = jnp.minimum(jnp.searchsorted(step_end, sc, side="right"), tiles.shape[0] - 1).astype(jnp.int32)
    local = sc - step_start[e]
    te = jnp.maximum(tiles[e], 1)
    jw = local // te
    rl = local % te
    rx = tile_start[e] + rl
    valid = s < total
    first = valid & (rl == 0)
    extra = jnp.maximum(s - total, 0)
    j = jnp.where(valid, jw, extra % nj)
    r = jnp.where(valid, rx, used_tiles + extra // nj)
    i32 = lambda a: a.astype(jnp.int32)
    return i32(e), i32(jw), i32(rx), i32(j), i32(r), i32(first), i32(valid)


def _combine_kernel(dest_ref, x_ref, route_ref, ys_hbm, o_ref, buf, sem, *, tb):
    def start(t, c):
        for k in range(TOP_K):
            pltpu.make_async_copy(ys_hbm.at[pl.ds(dest_ref[0, 0, TOP_K * t + k], 1)], buf.at[k, pl.ds(t, 1)],
                                  sem).start(priority=k % 2)
        return c

    lax.fori_loop(0, tb, start, 0)

    def wait(t, c):
        pltpu.make_async_copy(ys_hbm.at[pl.ds(0, 1)], buf.at[0, pl.ds(0, 1)], sem).wait()
        return c

    lax.fori_loop(0, TOP_K * tb, wait, 0)
    route = route_ref[...]
    o_ref[...] = x_ref[...] + route[:, 4:5] * buf[0] + route[:, 5:6] * buf[1]


def moe_combine(x, route, ys, dest):
    t, d = x.shape
    tb = _tile(t, MOE_TOKEN_BLOCK)
    return pl.pallas_call(
        functools.partial(_combine_kernel, tb=tb),
        grid=(t // tb,),
        in_specs=[pl.BlockSpec((1, 1, TOP_K * tb), lambda i: (i, 0, 0), memory_space=pltpu.SMEM),
                  pl.BlockSpec((tb, d), lambda i: (i, 0)), pl.BlockSpec((tb, LANES), lambda i: (i, 0)),
                  pl.BlockSpec(memory_space=pl.ANY)],
        out_specs=pl.BlockSpec((tb, d), lambda i: (i, 0)),
        out_shape=jax.ShapeDtypeStruct((t, d), F32),
        scratch_shapes=[pltpu.VMEM((TOP_K, tb, d), F32), pltpu.SemaphoreType.DMA(())],
        compiler_params=_cparams(("arbitrary",)),
        name="moe_combine",
    )(dest.reshape(t // tb, 1, TOP_K * tb), x, route, ys)


def moe_block(x, norm_g, w_router, b_router, w_gate, w_up, w_down):
    t, d = x.shape
    ne, _, f = w_gate.shape
    tmg = MOE_ROW_TILE
    hp, route, cnt = rmsnorm_router(x, norm_g, w_router, b_router)
    counts = cnt[0, :ne].astype(jnp.int32)
    padded = (counts + tmg - 1) // tmg * tmg
    starts = jnp.cumsum(padded) - padded
    ids = route[:, 0:2].astype(jnp.int32)
    dest = starts[ids] + route[:, 2:4].astype(jnp.int32)
    n_rows = t * TOP_K + ne * tmg
    n_tiles = n_rows // tmg
    src = jnp.zeros((n_rows,), jnp.int32).at[dest.reshape(-1)].set(
        jnp.repeat(jnp.arange(t, dtype=jnp.int32), TOP_K), unique_indices=True)
    xs = moe_dispatch(hp, src, n_rows)
    tn_up, tn_down = _tile(f, 512), _tile(d, 1024)
    act = grouped_matmul(xs, [w_gate, w_up], _group_steps(padded, f // tn_up, n_tiles), _epi_swiglu, BF16,
                         tn_up, True, "moe_up")
    ys = grouped_matmul(act, [w_down], _group_steps(padded, d // tn_down, n_tiles), _epi_plain, F32,
                        tn_down, False, "moe_down")
    return moe_combine(x, route, ys, dest)


def _mm_kernel(*refs, n_w, n_e, nk, epilogue):
    a_ref = refs[0]
    w_refs = refs[1:1 + n_w]
    e_refs = refs[1 + n_w:1 + n_w + n_e]
    o_ref = refs[1 + n_w + n_e]
    acc_refs = refs[2 + n_w + n_e:]
    a = a_ref[...]
    parts = [_dot(a, w[...]) for w in w_refs]
    if nk == 1:
        o_ref[...] = epilogue(parts, [e[...] for e in e_refs]).astype(o_ref.dtype)
        return
    k = pl.program_id(2)

    @pl.when(k == 0)
    def _():
        for acc, p in zip(acc_refs, parts):
            acc[...] = p

    @pl.when(k > 0)
    def _():
        for acc, p in zip(acc_refs, parts):
            acc[...] += p

    @pl.when(k == nk - 1)
    def _():
        o_ref[...] = epilogue([acc[...] for acc in acc_refs], [e[...] for e in e_refs]).astype(o_ref.dtype)


def matmul(a, ws, epilogue, extras=(), extra_kinds=(), out_dtype=F32, tm=1024, tn=1024, tk=4096, w_prefix=(),
           name="matmul"):
    m, kdim = a.shape
    n = ws[0].shape[-1]
    tm, tn, tk = _tile(m, tm), _tile(n, tn), _tile(kdim, tk)
    nk = kdim // tk
    squeezed = (None,) * len(w_prefix)
    in_specs = [pl.BlockSpec((tm, tk), lambda i, j, k: (i, k))]
    in_specs += [pl.BlockSpec(squeezed + (tk, tn), lambda i, j, k: (*w_prefix, k, j)) for _ in ws]
    for kind in extra_kinds:
        if kind == "mn":
            in_specs.append(pl.BlockSpec((tm, tn), lambda i, j, k: (i, j)))
        elif kind == "m":
            in_specs.append(pl.BlockSpec((tm, LANES), lambda i, j, k: (i, 0)))
        else:
            in_specs.append(pl.BlockSpec((1, tn), lambda i, j, k: (0, j)))
    scratch = [pltpu.VMEM((tm, tn), F32) for _ in ws] if nk > 1 else []
    return pl.pallas_call(
        functools.partial(_mm_kernel, n_w=len(ws), n_e=len(extras), nk=nk, epilogue=epilogue),
        grid=(m // tm, n // tn, nk),
        in_specs=in_specs,
        out_specs=pl.BlockSpec((tm, tn), lambda i, j, k: (i, j)),
        out_shape=jax.ShapeDtypeStruct((m, n), out_dtype),
        scratch_shapes=scratch,
        compiler_params=_cparams(("parallel", "parallel", "arbitrary")),
        name=name,
    )(a, *ws, *extras)


def _epi_plain(parts, extras):
    return parts[0]


def _epi_residual(parts, extras):
    return extras[0] + parts[0]


def _epi_swiglu(parts, extras):
    g, u = parts
    return g * jax.nn.sigmoid(g) * u


def _epi_glu(parts, extras):
    y, b = extras
    return y * jax.nn.sigmoid(parts[0] + b)


def _merge_kernel(h_ref, wg_ref, b_ref, y_ref, wb_ref, o_ref, accm_ref, *accg, nk, nb):
    j = pl.program_id(2)
    k = pl.program_id(3)
    part = _dot(h_ref[...], wg_ref[0])

    def finish(pre):
        term = jax.nn.sigmoid(pre + b_ref[0]) * _dot(y_ref[0], wb_ref[0])

        @pl.when(j == 0)
        def _():
            accm_ref[...] = term

        @pl.when((j > 0) & (j < nb - 1))
        def _():
            accm_ref[...] += term

        @pl.when(j == nb - 1)
        def _():
            o_ref[...] = (accm_ref[...] + term).astype(o_ref.dtype)

    if nk == 1:
        finish(part)
        return

    accg_ref, = accg

    @pl.when(k == 0)
    def _():
        accg_ref[...] = part

    @pl.when(k > 0)
    def _():
        accg_ref[...] += part

    @pl.when(k == nk - 1)
    def _():
        finish(accg_ref[...])


def gated_merge(h, w_gate, b_gate, ys, w_branch, layer, tm=1024, tn=512, tk=4096):
    t, d = h.shape
    nb, _, wdt = ys.shape
    tm, tn, tk = _tile(t, tm), _tile(d, tn), _tile(d, tk)
    nk = d // tk
    assert nb >= 2
    return pl.pallas_call(
        functools.partial(_merge_kernel, nk=nk, nb=nb),
        grid=(t // tm, d // tn, nb, nk),
        in_specs=[
            pl.BlockSpec((tm, tk), lambda i, n, j, k: (i, k)),
            pl.BlockSpec((None, 1, tk, tn), lambda i, n, j, k: (layer, j, k, n)),
            pl.BlockSpec((1, 1, tn), lambda i, n, j, k: (j, 0, n)),
            pl.BlockSpec((1, tm, wdt), lambda i, n, j, k: (j, i, 0)),
            pl.BlockSpec((None, 1, wdt, tn), lambda i, n, j, k: (layer, j, 0, n)),
        ],
        out_specs=pl.BlockSpec((tm, tn), lambda i, n, j, k: (i, n)),
        out_shape=jax.ShapeDtypeStruct((t, d), BF16),
        scratch_shapes=[pltpu.VMEM((tm, tn), F32)] * (1 if nk == 1 else 2),
        compiler_params=_cparams(("parallel", "parallel", "arbitrary", "arbitrary")),
        name="gated_merge",
    )(h, w_gate, b_gate, ys, w_branch)


def _softplus(x):
    return jnp.maximum(x, 0.0) + jnp.log1p(jnp.exp(-jnp.abs(x)))


def _ssd_kernel(xbc_ref, z_ref, dt_ref, dtt_ref, cw_ref, cb_ref, dtb_ref, dtbt_ref, alog_ref, alogt_ref,
                dskip_ref, ng_ref, expand_ref, ybuf_ref, o_ref, xe_ref, st_ref, *, n_heads, d_inner):
    del ybuf_ref
    q = SSD_CHUNK
    gn = SSD_N_GROUPS * SSD_D_STATE
    hpg = n_heads // SSD_N_GROUPS
    gw = hpg * SSD_HEAD_DIM
    c = pl.program_id(1)

    @pl.when(c == 0)
    def _():
        xe_ref[0:8, :] = jnp.zeros((8, xe_ref.shape[1]), F32)
        st_ref[...] = jnp.zeros(st_ref.shape, F32)

    xe_ref[8:8 + q, :] = xbc_ref[...]
    acc = cb_ref[...] + cw_ref[3:4, :] * xe_ref[8:8 + q, :]
    for kk in range(3):
        acc = acc + cw_ref[kk:kk + 1, :] * xe_ref[5 + kk:5 + kk + q, :]
    xe_ref[0:8, :] = xe_ref[q:q + 8, :]
    xc = acc * jax.nn.sigmoid(acc)
    xs = xc[:, :d_inner]
    bm = xc[:, d_inner:d_inner + gn].astype(BF16)
    cm = xc[:, d_inner + gn:].astype(BF16)

    dt = _softplus(dt_ref[...] + dtb_ref[...])
    da = dt * (-jnp.exp(alog_ref[...]))
    row = lax.broadcasted_iota(jnp.int32, (q, q), 0)
    col = lax.broadcasted_iota(jnp.int32, (q, q), 1)
    lower = row >= col
    tril = jnp.where(lower, 1.0, 0.0).astype(BF16)
    acs = _dot_exact_lhs01(tril, da)
    dtt = _softplus(dtt_ref[0] + dtbt_ref[...])
    dat = dtt * (-jnp.exp(alogt_ref[...]))
    triu = jnp.where(row <= col, 1.0, 0.0).astype(BF16)
    acst = _dot_exact_rhs01(dat, triu)

    expand = expand_ref[...]
    dt_x = _dot_exact_rhs01(dt, expand)
    acs_x = _dot_exact_rhs01(acs, expand)
    tot_x = acs_x[q - 1:q, :]
    xdt = xs * dt_x
    xdt_b = xdt.astype(BF16)
    xw_b = (xdt * jnp.exp(tot_x - acs_x)).astype(BF16)
    in_decay = jnp.exp(acs_x)
    chunk_decay = jnp.exp(tot_x)

    lane = lax.broadcasted_iota(jnp.int32, (1, gw), 1)
    ys = []
    for g in range(SSD_N_GROUPS):
        cg = cm[:, g * SSD_D_STATE:(g + 1) * SSD_D_STATE]
        bg = bm[:, g * SSD_D_STATE:(g + 1) * SSD_D_STATE]
        sl = slice(g * gw, (g + 1) * gw)
        gmat = lax.dot_general(cg, bg, (((1,), (1,)), ((), ())), preferred_element_type=F32)
        state = st_ref[g]
        y_g = _dot(cg, state.astype(BF16)) * in_decay[:, sl]
        xg = xdt_b[:, sl]
        for hh in range(hpg):
            hd = g * hpg + hh
            seg = acs[:, hd:hd + 1] - acst[hd:hd + 1, :]
            lmat = jnp.exp(jnp.where(lower, seg, -jnp.inf))
            mh = (gmat * lmat).astype(BF16)
            head_cols = (lane >= hh * SSD_HEAD_DIM) & (lane < (hh + 1) * SSD_HEAD_DIM)
            y_g = y_g + _dot(mh, jnp.where(head_cols, xg, jnp.zeros_like(xg)))
        upd = lax.dot_general(bg, xw_b[:, sl], (((0,), (0,)), ((), ())), preferred_element_type=F32)
        st_ref[g] = state * chunk_decay[:, sl] + upd
        ys.append(y_g)
    y = jnp.concatenate(ys, axis=-1) + xs * dskip_ref[...]
    zz = z_ref[...]
    y = y * (zz * jax.nn.sigmoid(zz))
    outs = []
    for g in range(SSD_N_GROUPS):
        yg = y[:, g * gw:(g + 1) * gw]
        ms = jnp.mean(yg * yg, axis=-1, keepdims=True)
        outs.append(yg * lax.rsqrt(ms + RMS_EPS))
    o_ref[0] = (jnp.concatenate(outs, axis=-1) * ng_ref[...]).astype(o_ref.dtype)


def ssd_mixer(proj, dt_t, cols, bsz, seq, conv_w, conv_b, dt_bias, a_log, d_skip, norm_g, ybuf, slot):
    n_heads = dt_bias.shape[0]
    d_inner = n_heads * SSD_HEAD_DIM
    conv_dim = conv_w.shape[1]
    q = SSD_CHUNK
    nc = seq // q
    pad = LANES - n_heads
    expand = jnp.repeat(jnp.eye(LANES, dtype=BF16)[:, :n_heads], SSD_HEAD_DIM, axis=1)
    dtb = jnp.pad(dt_bias, (0, pad)).reshape(1, LANES)
    alog = jnp.pad(a_log, (0, pad)).reshape(1, LANES)
    x_blk, z_blk, dt_blk = cols["xbc"] // conv_dim, cols["z"] // d_inner, cols["dt"] // LANES
    const = lambda b, c: (0, 0)
    return pl.pallas_call(
        functools.partial(_ssd_kernel, n_heads=n_heads, d_inner=d_inner),
        grid=(bsz, nc),
        in_specs=[
            pl.BlockSpec((q, conv_dim), lambda b, c: (b * nc + c, x_blk)),
            pl.BlockSpec((q, d_inner), lambda b, c: (b * nc + c, z_blk)),
            pl.BlockSpec((q, LANES), lambda b, c: (b * nc + c, dt_blk)),
            pl.BlockSpec((1, n_heads, q), lambda b, c: (b, 0, c)),
            pl.BlockSpec((4, conv_dim), const),
            pl.BlockSpec((1, conv_dim), const),
            pl.BlockSpec((1, LANES), const),
            pl.BlockSpec((n_heads, 1), const),
            pl.BlockSpec((1, LANES), const),
            pl.BlockSpec((n_heads, 1), const),
            pl.BlockSpec((1, d_inner), const),
            pl.BlockSpec((1, d_inner), const),
            pl.BlockSpec((LANES, d_inner), const),
            pl.BlockSpec(memory_space=pl.ANY),
        ],
        out_specs=pl.BlockSpec((1, q, d_inner), lambda b, c: (slot, b * nc + c, 0)),
        out_shape=jax.ShapeDtypeStruct(ybuf.shape, ybuf.dtype),
        scratch_shapes=[pltpu.VMEM((q + 8, conv_dim), F32),
                        pltpu.VMEM((SSD_N_GROUPS, SSD_D_STATE, d_inner // SSD_N_GROUPS), F32)],
        input_output_aliases={13: 0},
        compiler_params=_cparams(("parallel", "arbitrary")),
        name="ssd_mixer",
    )(proj, proj, proj, dt_t, conv_w, conv_b.reshape(1, conv_dim), dtb, dt_bias.reshape(n_heads, 1),
      alog, a_log.reshape(n_heads, 1), jnp.repeat(d_skip, SSD_HEAD_DIM).reshape(1, d_inner),
      norm_g.reshape(1, d_inner), expand, ybuf)


S5_ROWS = 8


def _s5_disc_kernel(are_ref, aim_ref, ldt_ref, brt_ref, bit_ref, pr_ref, pi_ref, bbr_ref, bbi_ref):
    dt = jnp.exp(ldt_ref[...])
    ar, ai = are_ref[...], aim_ref[...]
    mag = jnp.exp(ar * dt)
    lr, li = mag * jnp.cos(ai * dt), mag * jnp.sin(ai * dt)
    den = ar * ar + ai * ai
    fr = ((lr - 1.0) * ar + li * ai) / den
    fi = (li * ar - (lr - 1.0) * ai) / den
    pr, pi = lr, li
    for k in range(S5_ROWS):
        pr_ref[k] = pr
        pi_ref[k] = pi
        pr, pi = pr * lr - pi * li, pr * li + pi * lr
    br, bi = brt_ref[...], bit_ref[...]
    bbr_ref[...] = fr[:, None, :] * br - fi[:, None, :] * bi
    bbi_ref[...] = fr[:, None, :] * bi + fi[:, None, :] * br


def s5_discretise(a_re, a_im, log_dt, b_re, b_im):
    g, n = a_re.shape
    c = b_re.shape[2]
    brt, bit = jnp.swapaxes(b_re, 1, 2), jnp.swapaxes(b_im, 1, 2)
    return pl.pallas_call(
        _s5_disc_kernel,
        out_shape=[jax.ShapeDtypeStruct((S5_ROWS, g, n), F32), jax.ShapeDtypeStruct((S5_ROWS, g, n), F32),
                   jax.ShapeDtypeStruct((g, c, n), F32), jax.ShapeDtypeStruct((g, c, n), F32)],
        name="s5_discretise",
    )(a_re, a_im, log_dt.reshape(g, 1), brt, bit)


def _block_diag(blocks, per):
    g, r, c = blocks.shape
    nb = g // per
    eye = jnp.eye(per, dtype=blocks.dtype)
    out = blocks.reshape(nb, per, r, 1, c) * eye[None, :, None, :, None]
    return out.reshape(nb, per * r, per * c)


def _s5_kernel(u_ref, wr_ref, wi_ref, cr_ref, ci_ref, ctab_ref, d_ref, wglu_ref, bglu_ref, ybuf_ref, o_ref,
               xr_scr, xi_scr, carry_ref, y_scr, *, nblk, sw):
    del ybuf_ref
    tm = u_ref.shape[0]

    @pl.when(pl.program_id(1) == 0)
    def _():
        carry_ref[...] = jnp.zeros(carry_ref.shape, F32)

    for blk in range(nblk):
        cs = slice(blk * sw, (blk + 1) * sw)
        ls = slice(blk * LANES, (blk + 1) * LANES)
        u = u_ref[:, ls]
        ub = u.astype(BF16)
        xr_scr[...] = _dot(ub, wr_ref[blk])
        xi_scr[...] = _dot(ub, wi_ref[blk])

        def body(i, carry, cs=cs):
            c_re, c_im = carry
            r0 = pl.multiple_of(i * S5_ROWS, S5_ROWS)
            sr, si = xr_scr[pl.ds(r0, S5_ROWS), :], xi_scr[pl.ds(r0, S5_ROWS), :]
            for lvl, dist in enumerate((1, 2, 4)):
                lr, li = ctab_ref[2 * lvl, :, cs], ctab_ref[2 * lvl + 1, :, cs]
                tr, ti = pltpu.roll(sr, dist, axis=0), pltpu.roll(si, dist, axis=0)
                sr, si = sr + lr * tr - li * ti, si + lr * ti + li * tr
            pr, pi = ctab_ref[6, :, cs], ctab_ref[7, :, cs]
            sr, si = sr + pr * c_re - pi * c_im, si + pr * c_im + pi * c_re
            xr_scr[pl.ds(r0, S5_ROWS), :] = sr
            xi_scr[pl.ds(r0, S5_ROWS), :] = si
            return sr[S5_ROWS - 1:S5_ROWS, :], si[S5_ROWS - 1:S5_ROWS, :]

        c_re, c_im = lax.fori_loop(0, tm // S5_ROWS, body, (carry_ref[0:1, cs], carry_ref[1:2, cs]), unroll=2)
        carry_ref[0:1, cs] = c_re
        carry_ref[1:2, cs] = c_im
        y = _dot(xr_scr[...].astype(BF16), cr_ref[blk]) - _dot(xi_scr[...].astype(BF16), ci_ref[blk])
        y_scr[:, ls] = jax.nn.gelu(y + d_ref[:, ls] * u)
    y = y_scr[...]
    o_ref[0] = (y * jax.nn.sigmoid(_dot(y.astype(BF16), wglu_ref[...]) + bglu_ref[...])).astype(o_ref.dtype)


def s5_mixer(proj, u_col, bsz, seq, a_re, a_im, b_re, b_im, c_re, c_im, log_dt, d_skip, w_glu, b_glu,
             ybuf, slot):
    t = bsz * seq
    g, n = a_re.shape
    width = g * S5_GROUP
    per = S5_GROUPS_PER_BLOCK
    nblk = g // per
    sw = per * n
    ns = g * n
    pw_r, pw_i, bbr, bbi = s5_discretise(a_re, a_im, log_dt, b_re, b_im)
    w_r = _block_diag(bbr, per).astype(BF16)
    w_i = _block_diag(bbi, per).astype(BF16)
    c_r = _block_diag(jnp.swapaxes(c_re, 1, 2), per).astype(BF16)
    c_i = _block_diag(jnp.swapaxes(c_im, 1, 2), per).astype(BF16)
    pw_r, pw_i = pw_r.reshape(S5_ROWS, ns), pw_i.reshape(S5_ROWS, ns)
    sub = jnp.arange(S5_ROWS)[:, None]
    kinds = []
    for dist in (1, 2, 4):
        kinds += [jnp.where(sub >= dist, pw_r[dist - 1][None, :], 0.0),
                  jnp.where(sub >= dist, pw_i[dist - 1][None, :], 0.0)]
    ctab = jnp.stack(kinds + [pw_r, pw_i])
    tm = _tile(seq, 256)
    nt = seq // tm
    const2 = lambda b, c: (0, 0)
    const3 = lambda b, c: (0, 0, 0)
    return pl.pallas_call(
        functools.partial(_s5_kernel, nblk=nblk, sw=sw),
        grid=(bsz, nt),
        in_specs=[pl.BlockSpec((tm, width), lambda b, c: (b * nt + c, u_col // width)),
                  pl.BlockSpec((nblk, LANES, sw), const3), pl.BlockSpec((nblk, LANES, sw), const3),
                  pl.BlockSpec((nblk, sw, LANES), const3), pl.BlockSpec((nblk, sw, LANES), const3),
                  pl.BlockSpec((8, S5_ROWS, ns), const3),
                  pl.BlockSpec((1, width), const2), pl.BlockSpec((width, width), const2),
                  pl.BlockSpec((1, width), const2), pl.BlockSpec(memory_space=pl.ANY)],
        out_specs=pl.BlockSpec((1, tm, width), lambda b, c: (slot, b * nt + c, 0)),
        out_shape=jax.ShapeDtypeStruct(ybuf.shape, ybuf.dtype),
        scratch_shapes=[pltpu.VMEM((tm, sw), F32), pltpu.VMEM((tm, sw), F32), pltpu.VMEM((8, ns), F32),
                        pltpu.VMEM((tm, width), F32)],
        input_output_aliases={9: 0},
        compiler_params=_cparams(("parallel", "arbitrary")),
        name="s5_mixer",
    )(proj, w_r, w_i, c_r, c_i, ctab, d_skip.reshape(1, width), w_glu.astype(BF16), b_glu.reshape(1, width),
      ybuf)


def _flash_kernel(*refs, tq, n_maps, has_bias, finalize):
    if has_bias:
        q_ref, k_ref, v_ref, bias_ref, far_ref, *rest = refs
    else:
        q_ref, k_ref, v_ref, *rest = refs
    fin_refs = rest[:-2]
    o_ref = rest[-1]
    h = pl.program_id(1)
    qi = pl.program_id(2)
    rows = n_maps * tq
    dvo = v_ref.shape[1]
    qv = q_ref[...].reshape(rows, q_ref.shape[-1])

    def tiled(b):
        return jnp.concatenate([b] * n_maps, axis=0) if n_maps > 1 else b

    def block(kb, carry, add, shift):
        m, l, acc = carry
        off = pl.multiple_of(kb * tq, tq)
        s = lax.dot_general(qv, k_ref[pl.ds(off, tq), :], (((1,), (1,)), ((), ())), preferred_element_type=F32)
        sb = s.astype(BF16)
        if add is not None:
            sb = sb + add
        mx = jnp.max(sb, axis=-1, keepdims=True).astype(F32)
        m_hi = jnp.maximum(m, mx if shift is None else mx + shift)
        ref_b = (m_hi if shift is None else m_hi - shift).astype(BF16)
        m_new = ref_b.astype(F32) if shift is None else ref_b.astype(F32) + shift
        alpha = jnp.exp2(m - m_new)
        p = jnp.exp2(sb - ref_b)
        l_new = alpha * l + jnp.sum(p.astype(F32), axis=-1, keepdims=True)
        return m_new, l_new, alpha * acc + _dot(p, v_ref[pl.ds(off, tq), :])

    carry = (jnp.full((rows, 1), -jnp.inf, F32), jnp.zeros((rows, 1), F32), jnp.zeros((rows, dvo), F32))
    if has_bias:
        far = far_ref[h]
        n_far = jnp.maximum(qi - 1, 0)
        carry = lax.fori_loop(0, n_far, lambda kb, c: block(kb, c, None, far), carry)
        carry = lax.fori_loop(n_far, qi, lambda kb, c: block(kb, c, tiled(bias_ref[0, 1]), None), carry)
    else:
        carry = lax.fori_loop(0, qi, lambda kb, c: block(kb, c, None, None), carry)
    row = lax.broadcasted_iota(jnp.int32, (tq, tq), 0)
    col = lax.broadcasted_iota(jnp.int32, (tq, tq), 1)
    diag = jnp.where(row >= col, bias_ref[0, 0].astype(F32) if has_bias else 0.0, -jnp.inf).astype(BF16)
    m, l, acc = block(qi, carry, tiled(diag), None)
    o_ref[0] = finalize(acc / l, [r[...] for r in fin_refs]).astype(o_ref.dtype)


def flash_attention(q, k, v, bsz, seq, n_heads, dq, ybuf, slot, n_maps=1, bias=None, far=None, fin=(),
                    finalize=None, tq=512):
    dv = v.shape[1] // n_heads
    tq = _tile(seq, tq)
    nq = seq // tq
    has_bias = bias is not None
    if finalize is None:
        finalize = lambda o, extras: o
    in_specs = [pl.BlockSpec((n_maps, tq, dq), lambda b, h, i: (0, b * nq + i, h)),
                pl.BlockSpec((seq, dq), lambda b, h, i: (b, h)),
                pl.BlockSpec((seq, dv), lambda b, h, i: (b, h))]
    args = [q, k, v]
    if has_bias:
        in_specs += [pl.BlockSpec((1, 2, tq, tq), lambda b, h, i: (h, 0, 0, 0)),
                     pl.BlockSpec(memory_space=pltpu.SMEM)]
        args += [bias, far]
    for f in fin:
        in_specs.append(pl.BlockSpec(f.shape, lambda b, h, i: (0, 0)))
        args.append(f)
    in_specs.append(pl.BlockSpec(memory_space=pl.ANY))
    args.append(ybuf)
    return pl.pallas_call(
        functools.partial(_flash_kernel, tq=tq, n_maps=n_maps, has_bias=has_bias, finalize=finalize),
        grid=(bsz, n_heads, nq),
        in_specs=in_specs,
        out_specs=pl.BlockSpec((1, tq, dv), lambda b, h, i: (slot, b * nq + i, h)),
        out_shape=jax.ShapeDtypeStruct(ybuf.shape, ybuf.dtype),
        input_output_aliases={len(args) - 1: 0},
        compiler_params=_cparams(("parallel", "parallel", "arbitrary")),
        name="flash_bias" if has_bias else "flash",
    )(*args)


def _mla_proj_kernel(cq_ref, ckv_ref, kpe_ref, cos_ref, sin_ref, qng_ref, kvng_ref, wq_ref, wkv_ref,
                     qg_ref, kg_ref, q_ref, k_ref, v_ref, *, n_heads, scale):
    half = LANES // 2

    def norm(x, g):
        return (x * lax.rsqrt(jnp.mean(x * x, axis=-1, keepdims=True) + RMS_EPS) * g).astype(BF16)

    cq = norm(cq_ref[...], qng_ref[...])
    ckv = norm(ckv_ref[...], kvng_ref[...])
    cos, sin = cos_ref[...], sin_ref[...]
    kpe = kpe_ref[...]
    kpe_ss = jnp.sum(jnp.where(lax.broadcasted_iota(jnp.int32, kpe.shape, 1) < half, kpe * kpe, 0.0),
                     axis=-1, keepdims=True)
    n_qk = float(MLA_NOPE + MLA_ROPE)
    qg, kg = qg_ref[...], kg_ref[...]

    def rotary(hi, gains, inv):
        return inv * (hi * (gains[1:2] * cos) + pltpu.roll(hi, half, axis=1) * (gains[2:3] * sin))

    k_rot_raw = kpe * (kg[1:2] * cos) + pltpu.roll(kpe, half, axis=1) * (kg[2:3] * sin)
    for h in range(n_heads):
        xq = _dot(cq, wq_ref[:, h * MLA_PAD:(h + 1) * MLA_PAD])
        lo, hi = xq[:, :LANES], xq[:, LANES:]
        ss = jnp.sum(lo * lo, axis=-1, keepdims=True) + 0.5 * jnp.sum(hi * hi, axis=-1, keepdims=True)
        inv = lax.rsqrt(ss / n_qk + RMS_EPS) * scale
        q_ref[:, h * MLA_PAD:h * MLA_PAD + LANES] = (lo * inv * qg[0:1]).astype(BF16)
        q_ref[:, h * MLA_PAD + LANES:(h + 1) * MLA_PAD] = rotary(hi, qg, inv).astype(BF16)
        xkv = _dot(ckv, wkv_ref[:, h * MLA_PAD:(h + 1) * MLA_PAD])
        kn = xkv[:, :LANES]
        inv_k = lax.rsqrt((jnp.sum(kn * kn, axis=-1, keepdims=True) + kpe_ss) / n_qk + RMS_EPS)
        k_ref[:, h * MLA_PAD:h * MLA_PAD + LANES] = (kn * inv_k * kg[0:1]).astype(BF16)
        k_ref[:, h * MLA_PAD + LANES:(h + 1) * MLA_PAD] = (k_rot_raw * inv_k).astype(BF16)
        v_ref[:, h * LANES:(h + 1) * LANES] = xkv[:, LANES:].astype(BF16)


def _rope_gain_rows(gain):
    rot = gain[MLA_NOPE:]
    r = MLA_ROPE // 2
    swapped = jnp.concatenate([rot[r:], rot[:r]])
    z = jnp.zeros((LANES - MLA_ROPE,), F32)
    return jnp.stack([gain[:MLA_NOPE], jnp.concatenate([rot, z]), jnp.concatenate([swapped, z])])


def mla_mixer(proj, cols, bsz, seq, q_norm_g, w_q_up, kv_norm_g, w_kv_up, q_gain, k_gain, ybuf, slot):
    t = bsz * seq
    hds = MLA_N_HEADS
    q_rank, kv_rank = w_q_up.shape[0], w_kv_up.shape[0]
    r = MLA_ROPE // 2
    wq = w_q_up.reshape(q_rank, hds, MLA_NOPE + MLA_ROPE)
    wq = jnp.concatenate([wq, wq[:, :, MLA_NOPE + r:], wq[:, :, MLA_NOPE:MLA_NOPE + r]], axis=-1)
    wq = wq.reshape(q_rank, hds * MLA_PAD).astype(BF16)
    wkv = w_kv_up.astype(BF16)
    inv = 1.0 / (ROPE_THETA ** (jnp.arange(0, MLA_ROPE, 2, dtype=F32) / MLA_ROPE))
    ang = jnp.arange(seq, dtype=F32)[:, None] * inv[None, :]
    zer = jnp.zeros((seq, LANES - MLA_ROPE), F32)
    cos_t = jnp.concatenate([jnp.cos(ang), jnp.cos(ang), zer], axis=-1)
    sin_t = jnp.concatenate([-jnp.sin(ang), jnp.sin(ang), zer], axis=-1)
    tm = _tile(seq, 512)
    npos = seq // tm
    const = lambda i: (0, 0)
    q, k, v = pl.pallas_call(
        functools.partial(_mla_proj_kernel, n_heads=hds, scale=LOG2E * float(MLA_NOPE + MLA_ROPE) ** -0.5),
        grid=(t // tm,),
        in_specs=[pl.BlockSpec((tm, q_rank), lambda i: (i, cols["cq"] // q_rank)),
                  pl.BlockSpec((tm, kv_rank), lambda i: (i, cols["ckv"] // kv_rank)),
                  pl.BlockSpec((tm, LANES), lambda i: (i, cols["kpe"] // LANES)),
                  pl.BlockSpec((tm, LANES), lambda i: (i % npos, 0)),
                  pl.BlockSpec((tm, LANES), lambda i: (i % npos, 0)),
                  pl.BlockSpec((1, q_rank), const), pl.BlockSpec((1, kv_rank), const),
                  pl.BlockSpec((q_rank, hds * MLA_PAD), const), pl.BlockSpec((kv_rank, hds * MLA_PAD), const),
                  pl.BlockSpec((3, LANES), const), pl.BlockSpec((3, LANES), const)],
        out_specs=[pl.BlockSpec((tm, hds * MLA_PAD), lambda i: (i, 0)),
                   pl.BlockSpec((tm, hds * MLA_PAD), lambda i: (i, 0)),
                   pl.BlockSpec((tm, hds * LANES), lambda i: (i, 0))],
        out_shape=[jax.ShapeDtypeStruct((t, hds * MLA_PAD), BF16), jax.ShapeDtypeStruct((t, hds * MLA_PAD), BF16),
                   jax.ShapeDtypeStruct((t, hds * LANES), BF16)],
        compiler_params=_cparams(("parallel",)),
        name="mla_proj",
    )(proj, proj, proj, cos_t, sin_t, q_norm_g.reshape(1, q_rank), kv_norm_g.reshape(1, kv_rank), wq, wkv,
      _rope_gain_rows(q_gain), _rope_gain_rows(k_gain))
    return flash_attention(q.reshape(1, t, hds * MLA_PAD), k, v, bsz, seq, hds, MLA_PAD, ybuf, slot, tq=1024)


def _diff_prep_kernel(q_ref, k_ref, v_ref, qg_ref, kg_ref, seg_ref, qo_ref, ko_ref, vo_ref, *, hd, scale):
    seg = seg_ref[...]
    lane = lax.broadcasted_iota(jnp.int32, (1, LANES), 1)
    n_col = q_ref.shape[1] // LANES
    for c in range(n_col):
        sl = slice(c * LANES, (c + 1) * LANES)
        xq, xk = q_ref[:, sl], k_ref[:, sl]
        ssq = _dot_exact_rhs01(xq * xq, seg)
        ssk = _dot_exact_rhs01(xk * xk, seg)
        qn = xq * lax.rsqrt(ssq / hd + RMS_EPS) * (qg_ref[...] * scale)
        kn = xk * lax.rsqrt(ssk / hd + RMS_EPS) * kg_ref[...]
        qo_ref[0, :, sl] = jnp.where(lane < hd, qn, 0.0).astype(BF16)
        qo_ref[1, :, sl] = jnp.where(lane >= hd, qn, 0.0).astype(BF16)
        ko_ref[:, sl] = kn.astype(BF16)
    vo_ref[...] = v_ref[...].astype(BF16)


def _t5_bucket_np(n):
    max_exact = REL_BUCKETS // 2
    nf = np.maximum(n, 1).astype(np.float32)
    large = max_exact + (np.log(nf / np.float32(max_exact)) / np.float32(math.log(REL_MAX_DIST / max_exact))
                         * np.float32(REL_BUCKETS - max_exact)).astype(np.int32)
    large = np.minimum(large, REL_BUCKETS - 1)
    return np.where(n < max_exact, n, large).astype(np.int32)


def _bias_tiles_kernel(bkt_ref, rb_ref, o_ref):
    h = pl.program_id(0)
    for d in range(2):
        bkt = bkt_ref[d]
        tile = jnp.zeros(bkt.shape, F32)
        for b in range(REL_BUCKETS):
            tile = jnp.where(bkt == b, rb_ref[b, h] * LOG2E, tile)
        o_ref[0, d] = tile.astype(o_ref.dtype)


def _diff_finalize(lambda_init, tq):
    def fin(o, extras):
        lq1, lk1, lq2, lk2, sub_g = extras
        lam = (jnp.exp(jnp.sum(lq1 * lk1, axis=-1, keepdims=True))
               - jnp.exp(jnp.sum(lq2 * lk2, axis=-1, keepdims=True)) + lambda_init)
        d = o[:tq] - lam * o[tq:]
        ms = jnp.mean(d * d, axis=-1, keepdims=True)
        return d * lax.rsqrt(ms + RMS_EPS) * (sub_g * (1.0 - lambda_init))
    return fin


def diff_mixer(proj, cols, bsz, seq, q_gain, k_gain, lq1, lk1, lq2, lk2, subln_g, rel_bias, lambda_init,
               ybuf, slot):
    t = bsz * seq
    hds = DIFF_N_HEADS
    hd = q_gain.shape[0]
    width = hds * 2 * hd
    tm = _tile(t, 512)
    seg = (np.arange(LANES)[:, None] // hd == np.arange(LANES)[None, :] // hd)
    seg = jnp.asarray(seg, BF16)
    gq, gk = jnp.tile(q_gain, LANES // hd).reshape(1, LANES), jnp.tile(k_gain, LANES // hd).reshape(1, LANES)
    const = lambda i: (0, 0)
    qd, kd, vd = pl.pallas_call(
        functools.partial(_diff_prep_kernel, hd=hd, scale=LOG2E * float(hd) ** -0.5),
        grid=(t // tm,),
        in_specs=[pl.BlockSpec((tm, width), lambda i: (i, cols["dq"] // width)),
                  pl.BlockSpec((tm, width), lambda i: (i, cols["dk"] // width)),
                  pl.BlockSpec((tm, width), lambda i: (i, cols["dv"] // width)),
                  pl.BlockSpec((1, LANES), const), pl.BlockSpec((1, LANES), const),
                  pl.BlockSpec((LANES, LANES), const)],
        out_specs=[pl.BlockSpec((2, tm, width), lambda i: (0, i, 0)),
                   pl.BlockSpec((tm, width), lambda i: (i, 0)), pl.BlockSpec((tm, width), lambda i: (i, 0))],
        out_shape=[jax.ShapeDtypeStruct((2, t, width), BF16), jax.ShapeDtypeStruct((t, width), BF16),
                   jax.ShapeDtypeStruct((t, width), BF16)],
        compiler_params=_cparams(("parallel",)),
        name="diff_prep",
    )(proj, proj, proj, gq, gk, seg)

    tq = _tile(seq, 512)
    assert tq + 1 > REL_MAX_DIST, "blocks two or more before the diagonal must all be at the far distance"
    idx = np.arange(tq)
    dist0 = np.maximum(idx[:, None] - idx[None, :], 0)
    dist1 = idx[:, None] - idx[None, :] + tq
    bkt = jnp.asarray(np.stack([_t5_bucket_np(dist0), _t5_bucket_np(dist1)]))
    bias = pl.pallas_call(
        _bias_tiles_kernel,
        grid=(hds,),
        in_specs=[pl.BlockSpec((2, tq, tq), lambda h: (0, 0, 0)), pl.BlockSpec(memory_space=pltpu.SMEM)],
        out_specs=pl.BlockSpec((1, 2, tq, tq), lambda h: (h, 0, 0, 0)),
        out_shape=jax.ShapeDtypeStruct((hds, 2, tq, tq), BF16),
        compiler_params=_cparams(("parallel",)),
        name="t5_bias_tiles",
    )(bkt, rel_bias)
    far = rel_bias[REL_BUCKETS - 1] * LOG2E
    fin = tuple(a.reshape(1, -1) for a in (lq1, lk1, lq2, lk2, subln_g))
    return flash_attention(qd, kd, vd, bsz, seq, hds, 2 * hd, ybuf, slot, n_maps=2, bias=bias, far=far, fin=fin,
                           finalize=_diff_finalize(lambda_init, tq), tq=tq)


def _pack_w_in(w_in_l, widths):
    (z_w, xbc_w, dt_w, u_w, cq_w, ckv_w, kpe_w, dq_w, dk_w, dv_w) = widths
    lead = w_in_l.shape[:-1]
    starts = np.cumsum([0] + list(widths))
    seg = {n: w_in_l[..., starts[i]:starts[i + 1]]
           for i, n in enumerate(("z", "xbc", "dt", "u", "cq", "ckv", "kpe", "dq", "dk", "dv"))}
    r = kpe_w // 2
    kpe2 = jnp.concatenate([seg["kpe"], seg["kpe"][..., r:], seg["kpe"][..., :r],
                            jnp.zeros(lead + (LANES - 2 * kpe_w,), F32)], axis=-1)
    dtp = jnp.concatenate([seg["dt"], jnp.zeros(lead + (LANES - dt_w,), F32)], axis=-1)
    order = [("xbc", seg["xbc"]), ("z", seg["z"]), ("u", seg["u"]), ("dq", seg["dq"]), ("dk", seg["dk"]),
             ("dv", seg["dv"]), ("ckv", seg["ckv"]), ("cq", seg["cq"]), ("kpe", kpe2), ("dt", dtp)]
    cols, off = {}, 0
    for name, w in order:
        assert off % w.shape[-1] == 0, (name, off, w.shape[-1])
        cols[name] = off
        off += w.shape[-1]
    return jnp.concatenate([w for _, w in order], axis=-1).astype(BF16), cols


def kernel(x, norm_mix_g, w_in, ssd_conv_w, ssd_conv_b, ssd_dt_bias, ssd_a_log, ssd_d, ssd_norm_g, s5_a_re, s5_a_im, s5_b_re, s5_b_im, s5_c_re, s5_c_im, s5_log_dt, s5_d, s5_w_glu, s5_b_glu, mla_q_norm_g, mla_w_q_up, mla_kv_norm_g, mla_w_kv_up, mla_q_gain, mla_k_gain, diff_q_gain, diff_k_gain, diff_lq1, diff_lk1, diff_lq2, diff_lk2, diff_subln_g, rel_bias, w_gate, b_gate, w_branch, w_out, norm_ffn_g, ffn_w_gate, ffn_w_up, ffn_w_down, moe_w_router, moe_b_router, moe_w_gate, moe_w_up, moe_w_down):
    bsz, seq, d = x.shape
    t = bsz * seq
    depth = w_in.shape[0]
    mix = d // 4
    n_ssd_heads = ssd_dt_bias.shape[1]
    widths = (mix, ssd_conv_w.shape[2], n_ssd_heads, mix, mla_w_q_up.shape[1], mla_w_kv_up.shape[1], MLA_ROPE,
              mix, mix, mix)
    xf = x.reshape(t, d)
    w_in_b, cols = _pack_w_in(w_in, widths)
    w_gate_b, w_branch_b, w_out_b = w_gate.astype(BF16), w_branch.astype(BF16), w_out.astype(BF16)
    ffn_gate_b, ffn_up_b, ffn_down_b = ffn_w_gate.astype(BF16), ffn_w_up.astype(BF16), ffn_w_down.astype(BF16)
    for i in range(depth):
        h = rmsnorm(xf, norm_mix_g[i])
        proj = matmul(h, [w_in_b], _epi_plain, out_dtype=F32, tn=512, w_prefix=(i,), name="in_proj")
        dt_t = jnp.swapaxes(proj[:, cols["dt"]:cols["dt"] + n_ssd_heads].reshape(bsz, seq, n_ssd_heads), 1, 2)
        ys = ssd_mixer(proj, dt_t, cols, bsz, seq, ssd_conv_w[i], ssd_conv_b[i], ssd_dt_bias[i], ssd_a_log[i],
                       ssd_d[i], ssd_norm_g[i], jnp.zeros((N_BRANCH, t, mix), BF16), 0)
        ys = s5_mixer(proj, cols["u"], bsz, seq, s5_a_re[i], s5_a_im[i], s5_b_re[i], s5_b_im[i], s5_c_re[i],
                      s5_c_im[i], s5_log_dt[i], s5_d[i], s5_w_glu[i], s5_b_glu[i], ys, 1)
        ys = mla_mixer(proj, cols, bsz, seq, mla_q_norm_g[i], mla_w_q_up[i], mla_kv_norm_g[i], mla_w_kv_up[i],
                       mla_q_gain[i], mla_k_gain[i], ys, 2)
        lambda_init = 0.8 - 0.6 * math.exp(-0.3 * i)
        ys = diff_mixer(proj, cols, bsz, seq, diff_q_gain[i], diff_k_gain[i], diff_lq1[i], diff_lk1[i],
                        diff_lq2[i], diff_lk2[i], diff_subln_g[i], rel_bias, lambda_init, ys, 3)
        merged = gated_merge(h, w_gate_b, b_gate[i][:, None, :], ys, w_branch_b, i)
        xf = matmul(merged, [w_out_b], _epi_residual, extras=(xf,), extra_kinds=("mn",), tk=2048, w_prefix=(i,),
                    name="out_proj")
        e = i // 2
        if i % 2 == 0:
            h2 = rmsnorm(xf, norm_ffn_g[i])
            act = matmul(h2, [ffn_gate_b, ffn_up_b], _epi_swiglu, out_dtype=BF16, tn=512, w_prefix=(e,),
                         name="ffn_up")
            xf = matmul(act, [ffn_down_b], _epi_residual, extras=(xf,), extra_kinds=("mn",), tk=3584,
                        w_prefix=(e,), name="ffn_down")
        else:
            xf = moe_block(xf, norm_ffn_g[i], moe_w_router[e], moe_b_router[e], moe_w_gate[e], moe_w_up[e],
                           moe_w_down[e])
    return xf.reshape(bsz, seq, d)
```

```python
import functools
import math

import numpy as np
import jax
import jax.numpy as jnp
from jax import lax
from jax.experimental import pallas as pl
from jax.experimental.pallas import tpu as pltpu

F32 = jnp.float32
BF16 = jnp.bfloat16

V7X_VMEM_BYTES = 64 * 1024 * 1024
VMEM_LIMIT = V7X_VMEM_BYTES - 8 * 1024 * 1024
LANES = 128

RMS_EPS = 1e-6
SSD_HEAD_DIM = 64
SSD_N_GROUPS = 4
SSD_D_STATE = 128
SSD_CHUNK = 128
S5_GROUP = 16
S5_STATE = 64
S5_GROUPS_PER_BLOCK = 8
MLA_N_HEADS = 8
MLA_NOPE = 128
MLA_ROPE = 64
MLA_PAD = 256
ROPE_THETA = 10000.0
DIFF_N_HEADS = 8
REL_BUCKETS = 32
REL_MAX_DIST = 128
TOP_K = 2
N_BRANCH = 4
LOG2E = math.log2(math.e)


def _cparams(sem):
    return pltpu.CompilerParams(dimension_semantics=sem, vmem_limit_bytes=VMEM_LIMIT)


def _tile(n, pref):
    t = min(n, pref)
    while n % t:
        t //= 2
    return t


def _split3(a):
    hi = a.astype(BF16)
    r1 = a - hi.astype(F32)
    mid = r1.astype(BF16)
    lo = (r1 - mid.astype(F32)).astype(BF16)
    return hi, mid, lo


def _dot(a, b):
    return jnp.dot(a, b, preferred_element_type=F32)


def _dot_exact_rhs01(a, sel):
    hi, mid, lo = _split3(a)
    return _dot(hi, sel) + _dot(mid, sel) + _dot(lo, sel)


def _dot_exact_lhs01(sel, b):
    hi, mid, lo = _split3(b)
    return _dot(sel, hi) + _dot(sel, mid) + _dot(sel, lo)


def _rmsnorm_kernel(x_ref, g_ref, o_ref):
    x = x_ref[...]
    ms = jnp.mean(x * x, axis=-1, keepdims=True)
    o_ref[...] = (x * lax.rsqrt(ms + RMS_EPS) * g_ref[...]).astype(o_ref.dtype)


def rmsnorm(x, g, tm=256):
    t, d = x.shape
    tm = _tile(t, tm)
    return pl.pallas_call(
        _rmsnorm_kernel,
        grid=(t // tm,),
        in_specs=[pl.BlockSpec((tm, d), lambda i: (i, 0)), pl.BlockSpec((1, d), lambda i: (0, 0))],
        out_specs=pl.BlockSpec((tm, d), lambda i: (i, 0)),
        out_shape=jax.ShapeDtypeStruct((t, d), BF16),
        compiler_params=_cparams(("parallel",)),
        name="rmsnorm",
    )(x, g.reshape(1, d))


def _rmsnorm_router_kernel(x_ref, g_ref, wr_ref, br_ref, hp_ref, route_ref, cnt_ref, run_ref, *, n_experts):
    @pl.when(pl.program_id(0) == 0)
    def _():
        run_ref[...] = jnp.zeros(run_ref.shape, F32)

    x = x_ref[...]
    tm, d = x.shape
    ms = jnp.mean(x * x, axis=-1, keepdims=True)
    h = x * lax.rsqrt(ms + RMS_EPS) * g_ref[...]
    bits = lax.bitcast_convert_type(h.astype(BF16).astype(F32), jnp.uint32)
    hp_ref[...] = (bits[:, :d // 2] >> 16) | (bits[:, d // 2:] & jnp.uint32(0xFFFF0000))
    logits = jnp.dot(h, wr_ref[...], preferred_element_type=F32, precision=lax.Precision.HIGHEST) + br_ref[...]
    lane = lax.broadcasted_iota(jnp.int32, logits.shape, 1)
    neg = jnp.float32(-jnp.inf)
    logits = jnp.where(lane < n_experts, logits, neg)
    m1 = jnp.max(logits, axis=-1, keepdims=True)
    i1 = jnp.min(jnp.where(logits == m1, lane, LANES), axis=-1, keepdims=True)
    rest = jnp.where(lane == i1, neg, logits)
    m2 = jnp.max(rest, axis=-1, keepdims=True)
    i2 = jnp.min(jnp.where(rest == m2, lane, LANES), axis=-1, keepdims=True)
    e2 = jnp.exp(m2 - m1)
    w1 = 1.0 / (1.0 + e2)
    w2 = e2 / (1.0 + e2)
    sel = jnp.where(lane == i1, 1.0, 0.0) + jnp.where(lane == i2, 1.0, 0.0)
    row = lax.broadcasted_iota(jnp.int32, (tm, tm), 0)
    col = lax.broadcasted_iota(jnp.int32, (tm, tm), 1)
    before = jnp.where(col < row, 1.0, 0.0).astype(BF16)
    excl = _dot(before, sel.astype(BF16)) + run_ref[0:1, :]
    run_ref[...] = run_ref[...] + jnp.sum(sel, axis=0, keepdims=True)
    cnt_ref[...] = run_ref[...]
    r1 = jnp.sum(jnp.where(lane == i1, excl, 0.0), axis=-1, keepdims=True)
    r2 = jnp.sum(jnp.where(lane == i2, excl, 0.0), axis=-1, keepdims=True)
    route = jnp.zeros(logits.shape, F32)
    for pos, val in enumerate((i1.astype(F32), i2.astype(F32), r1, r2, w1, w2)):
        route = jnp.where(lane == pos, val, route)
    route_ref[...] = route


def rmsnorm_router(x, g, w_router, b_router, tm=256):
    t, d = x.shape
    ne = w_router.shape[1]
    tm = _tile(t, tm)
    wr = jnp.zeros((d, LANES), F32).at[:, :ne].set(w_router)
    br = jnp.zeros((1, LANES), F32).at[0, :ne].set(b_router)
    return pl.pallas_call(
        functools.partial(_rmsnorm_router_kernel, n_experts=ne),
        grid=(t // tm,),
        in_specs=[pl.BlockSpec((tm, d), lambda i: (i, 0)), pl.BlockSpec((1, d), lambda i: (0, 0)),
                  pl.BlockSpec((d, LANES), lambda i: (0, 0)), pl.BlockSpec((1, LANES), lambda i: (0, 0))],
        out_specs=[pl.BlockSpec((tm, d // 2), lambda i: (i, 0)), pl.BlockSpec((tm, LANES), lambda i: (i, 0)),
                   pl.BlockSpec((8, LANES), lambda i: (0, 0))],
        out_shape=[jax.ShapeDtypeStruct((t, d // 2), jnp.uint32), jax.ShapeDtypeStruct((t, LANES), F32),
                   jax.ShapeDtypeStruct((8, LANES), F32)],
        scratch_shapes=[pltpu.VMEM((8, LANES), F32)],
        compiler_params=_cparams(("arbitrary",)),
        name="rmsnorm_router",
    )(x, g.reshape(1, d), wr, br)


MOE_ROW_TILE = 512
MOE_TOKEN_BLOCK = 256


def _unpack_halves(xp):
    lo = lax.bitcast_convert_type(xp << 16, F32).astype(BF16)
    hi = lax.bitcast_convert_type(xp & jnp.uint32(0xFFFF0000), F32).astype(BF16)
    return lo, hi


def _dispatch_kernel(src_ref, h_hbm, o_ref, buf, sem, *, tmg):
    def start(r2, c):
        for p in range(2):
            r = 2 * r2 + p
            pltpu.make_async_copy(h_hbm.at[pl.ds(src_ref[0, 0, r], 1)], buf.at[pl.ds(r, 1)], sem).start(
                priority=p)
        return c

    lax.fori_loop(0, tmg // 2, start, 0)

    def wait(r, c):
        pltpu.make_async_copy(h_hbm.at[pl.ds(0, 1)], buf.at[pl.ds(0, 1)], sem).wait()
        return c

    lax.fori_loop(0, tmg, wait, 0)
    lo, hi = _unpack_halves(buf[...])
    half = lo.shape[1]
    o_ref[:, :half] = lo
    o_ref[:, half:] = hi


def moe_dispatch(hp, src, n_rows):
    dw = hp.shape[1]
    tmg = MOE_ROW_TILE
    return pl.pallas_call(
        functools.partial(_dispatch_kernel, tmg=tmg),
        grid=(n_rows // tmg,),
        in_specs=[pl.BlockSpec((1, 1, tmg), lambda i: (i, 0, 0), memory_space=pltpu.SMEM),
                  pl.BlockSpec(memory_space=pl.ANY)],
        out_specs=pl.BlockSpec((tmg, 2 * dw), lambda i: (i, 0)),
        out_shape=jax.ShapeDtypeStruct((n_rows, 2 * dw), BF16),
        scratch_shapes=[pltpu.VMEM((tmg, dw), hp.dtype), pltpu.SemaphoreType.DMA(())],
        compiler_params=_cparams(("arbitrary",)),
        name="moe_dispatch",
    )(src.reshape(n_rows // tmg, 1, tmg), hp)


def _grouped_kernel(e_tab, jw_tab, rx_tab, j_tab, r_tab, first_tab, valid_tab, x_ref, *rest, n_w, epilogue):
    del e_tab, jw_tab, rx_tab, j_tab, r_tab
    w_refs, o_ref, wb_refs = rest[:n_w], rest[n_w], rest[n_w + 1:]
    s = pl.program_id(0)

    @pl.when(first_tab[s] == 1)
    def _():
        for w, wb in zip(w_refs, wb_refs):
            wb[...] = w[0].astype(BF16)

    @pl.when(valid_tab[s] == 1)
    def _():
        x = x_ref[...]
        o_ref[...] = epilogue([_dot(x, wb[...]) for wb in wb_refs], []).astype(o_ref.dtype)

    @pl.when(valid_tab[s] == 0)
    def _():
        o_ref[...] = jnp.zeros(o_ref.shape, o_ref.dtype)


def grouped_matmul(x, ws, tabs, epilogue, out_dtype, tn, name):
    n_rows, kx = x.shape
    ne, kdim, n = ws[0].shape
    tmg = MOE_ROW_TILE
    tn = _tile(n, tn)
    n_steps = tabs[0].shape[0]
    grid_spec = pltpu.PrefetchScalarGridSpec(
        num_scalar_prefetch=7,
        grid=(n_steps,),
        in_specs=[pl.BlockSpec((tmg, kx), lambda s, e, jw, rx, j, r, f, v: (rx[s], 0))]
        + [pl.BlockSpec((1, kdim, tn), lambda s, e, jw, rx, j, r, f, v: (e[s], 0, jw[s])) for _ in ws],
        out_specs=pl.BlockSpec((tmg, tn), lambda s, e, jw, rx, j, r, f, v: (r[s], j[s])),
        scratch_shapes=[pltpu.VMEM((kdim, tn), BF16) for _ in ws],
    )
    return pl.pallas_call(
        functools.partial(_grouped_kernel, n_w=len(ws), epilogue=epilogue),
        grid_spec=grid_spec,
        out_shape=jax.ShapeDtypeStruct((n_rows, n), out_dtype),
        compiler_params=_cparams(("arbitrary",)),
        name=name,
    )(*tabs, x, *ws)


def _group_steps(padded, nj, n_tiles):
    tmg = MOE_ROW_TILE
    tiles = padded // tmg
    tile_start = jnp.cumsum(tiles) - tiles
    steps = tiles * nj
    step_end = jnp.cumsum(steps)
    step_start = step_end - steps
    total = step_end[-1]
    used_tiles = jnp.sum(tiles)
    s = jnp.arange(n_tiles * nj, dtype=jnp.int32)
    sc = jnp.minimum(s, total - 1)
    e = jnp.minimum(jnp.searchsorted(step_end, sc, side="right"), tiles.shape[0] - 1).astype(jnp.int32)
    local = sc - step_start[e]
    te = jnp.maximum(tiles[e], 1)
    jw = local // te
    rl = local % te
    rx = tile_start[e] + rl
    valid = s < total
    first = valid & (rl == 0)
    extra = jnp.maximum(s - total, 0)
    j = jnp.where(valid, jw, extra % nj)
    r = jnp.where(valid, rx, used_tiles + extra // nj)
    i32 = lambda a: a.astype(jnp.int32)
    return i32(e), i32(jw), i32(rx), i32(j), i32(r), i32(first), i32(valid)


def _combine_kernel(dest_ref, x_ref, route_ref, ys_hbm, o_ref, buf, sem, *, tb):
    def start(t, c):
        for k in range(TOP_K):
            pltpu.make_async_copy(ys_hbm.at[pl.ds(dest_ref[0, 0, TOP_K * t + k], 1)], buf.at[k, pl.ds(t, 1)],
                                  sem).start(priority=k % 2)
        return c

    lax.fori_loop(0, tb, start, 0)

    def wait(t, c):
        pltpu.make_async_copy(ys_hbm.at[pl.ds(0, 1)], buf.at[0, pl.ds(0, 1)], sem).wait()
        return c

    lax.fori_loop(0, TOP_K * tb, wait, 0)
    route = route_ref[...]
    o_ref[...] = x_ref[...] + route[:, 4:5] * buf[0] + route[:, 5:6] * buf[1]


def moe_combine(x, route, ys, dest):
    t, d = x.shape
    tb = _tile(t, MOE_TOKEN_BLOCK)
    return pl.pallas_call(
        functools.partial(_combine_kernel, tb=tb),
        grid=(t // tb,),
        in_specs=[pl.BlockSpec((1, 1, TOP_K * tb), lambda i: (i, 0, 0), memory_space=pltpu.SMEM),
                  pl.BlockSpec((tb, d), lambda i: (i, 0)), pl.BlockSpec((tb, LANES), lambda i: (i, 0)),
                  pl.BlockSpec(memory_space=pl.ANY)],
        out_specs=pl.BlockSpec((tb, d), lambda i: (i, 0)),
        out_shape=jax.ShapeDtypeStruct((t, d), F32),
        scratch_shapes=[pltpu.VMEM((TOP_K, tb, d), F32), pltpu.SemaphoreType.DMA(())],
        compiler_params=_cparams(("arbitrary",)),
        name="moe_combine",
    )(dest.reshape(t // tb, 1, TOP_K * tb), x, route, ys)


def moe_block(x, norm_g, w_router, b_router, w_gate, w_up, w_down):
    t, d = x.shape
    ne, _, f = w_gate.shape
    tmg = MOE_ROW_TILE
    hp, route, cnt = rmsnorm_router(x, norm_g, w_router, b_router)
    counts = cnt[0, :ne].astype(jnp.int32)
    padded = (counts + tmg - 1) // tmg * tmg
    starts = jnp.cumsum(padded) - padded
    ids = route[:, 0:2].astype(jnp.int32)
    dest = starts[ids] + route[:, 2:4].astype(jnp.int32)
    n_rows = t * TOP_K + ne * tmg
    n_tiles = n_rows // tmg
    src = jnp.zeros((n_rows,), jnp.int32).at[dest.reshape(-1)].set(
        jnp.repeat(jnp.arange(t, dtype=jnp.int32), TOP_K), unique_indices=True)
    xs = moe_dispatch(hp, src, n_rows)
    tn_up, tn_down = _tile(f, 512), _tile(d, 1024)
    act = grouped_matmul(xs, [w_gate, w_up], _group_steps(padded, f // tn_up, n_tiles), _epi_swiglu, BF16,
                         tn_up, "moe_up")
    ys = grouped_matmul(act, [w_down], _group_steps(padded, d // tn_down, n_tiles), _epi_plain, F32,
                        tn_down, "moe_down")
    return moe_combine(x, route, ys, dest)


def _mm_kernel(*refs, n_w, n_e, nk, epilogue):
    a_ref = refs[0]
    w_refs = refs[1:1 + n_w]
    e_refs = refs[1 + n_w:1 + n_w + n_e]
    o_ref = refs[1 + n_w + n_e]
    acc_refs = refs[2 + n_w + n_e:]
    a = a_ref[...]
    parts = [_dot(a, w[...]) for w in w_refs]
    if nk == 1:
        o_ref[...] = epilogue(parts, [e[...] for e in e_refs]).astype(o_ref.dtype)
        return
    k = pl.program_id(2)

    @pl.when(k == 0)
    def _():
        for acc, p in zip(acc_refs, parts):
            acc[...] = p

    @pl.when(k > 0)
    def _():
        for acc, p in zip(acc_refs, parts):
            acc[...] += p

    @pl.when(k == nk - 1)
    def _():
        o_ref[...] = epilogue([acc[...] for acc in acc_refs], [e[...] for e in e_refs]).astype(o_ref.dtype)


def matmul(a, ws, epilogue, extras=(), extra_kinds=(), out_dtype=F32, tm=1024, tn=1024, tk=4096, w_prefix=(),
           name="matmul"):
    m, kdim = a.shape
    n = ws[0].shape[-1]
    tm, tn, tk = _tile(m, tm), _tile(n, tn), _tile(kdim, tk)
    nk = kdim // tk
    squeezed = (None,) * len(w_prefix)
    in_specs = [pl.BlockSpec((tm, tk), lambda i, j, k: (i, k))]
    in_specs += [pl.BlockSpec(squeezed + (tk, tn), lambda i, j, k: (*w_prefix, k, j)) for _ in ws]
    for kind in extra_kinds:
        if kind == "mn":
            in_specs.append(pl.BlockSpec((tm, tn), lambda i, j, k: (i, j)))
        elif kind == "m":
            in_specs.append(pl.BlockSpec((tm, LANES), lambda i, j, k: (i, 0)))
        else:
            in_specs.append(pl.BlockSpec((1, tn), lambda i, j, k: (0, j)))
    scratch = [pltpu.VMEM((tm, tn), F32) for _ in ws] if nk > 1 else []
    return pl.pallas_call(
        functools.partial(_mm_kernel, n_w=len(ws), n_e=len(extras), nk=nk, epilogue=epilogue),
        grid=(m // tm, n // tn, nk),
        in_specs=in_specs,
        out_specs=pl.BlockSpec((tm, tn), lambda i, j, k: (i, j)),
        out_shape=jax.ShapeDtypeStruct((m, n), out_dtype),
        scratch_shapes=scratch,
        compiler_params=_cparams(("parallel", "parallel", "arbitrary")),
        name=name,
    )(a, *ws, *extras)


def _epi_plain(parts, extras):
    return parts[0]


def _epi_residual(parts, extras):
    return extras[0] + parts[0]


def _epi_swiglu(parts, extras):
    g, u = parts
    return g * jax.nn.sigmoid(g) * u


def _epi_glu(parts, extras):
    y, b = extras
    return y * jax.nn.sigmoid(parts[0] + b)


def _merge_kernel(h_ref, wg_ref, b_ref, y_ref, wb_ref, o_ref, accm_ref, *accg, nk, nb):
    j = pl.program_id(2)
    k = pl.program_id(3)

    @pl.when((pl.program_id(0) == 0) & (pl.program_id(1) == 0) & (j == 0) & (k == 0))
    def _():
        accm_ref[...] = jnp.zeros(accm_ref.shape, F32)

    part = _dot(h_ref[...], wg_ref[0])

    def finish(pre):
        term = jax.nn.sigmoid(pre + b_ref[0]) * _dot(y_ref[0], wb_ref[0])
        total = jnp.where(j == 0, term, accm_ref[...] + term)
        accm_ref[...] = total
        o_ref[...] = total.astype(o_ref.dtype)

    if nk == 1:
        finish(part)
        return

    accg_ref, = accg

    @pl.when(k == 0)
    def _():
        accg_ref[...] = part

    @pl.when(k > 0)
    def _():
        accg_ref[...] += part

    @pl.when(k == nk - 1)
    def _():
        finish(accg_ref[...])


def gated_merge(h, w_gate, b_gate, ys, w_branch, layer, tm=1024, tn=512, tk=4096):
    t, d = h.shape
    nb, _, wdt = ys.shape
    tm, tn, tk = _tile(t, tm), _tile(d, tn), _tile(d, tk)
    nk = d // tk
    assert nb >= 2
    return pl.pallas_call(
        functools.partial(_merge_kernel, nk=nk, nb=nb),
        grid=(t // tm, d // tn, nb, nk),
        in_specs=[
            pl.BlockSpec((tm, tk), lambda i, n, j, k: (i, k)),
            pl.BlockSpec((None, 1, tk, tn), lambda i, n, j, k: (layer, j, k, n)),
            pl.BlockSpec((1, 1, tn), lambda i, n, j, k: (j, 0, n)),
            pl.BlockSpec((1, tm, wdt), lambda i, n, j, k: (j, i, 0)),
            pl.BlockSpec((None, 1, wdt, tn), lambda i, n, j, k: (layer, j, 0, n)),
        ],
        out_specs=pl.BlockSpec((tm, tn), lambda i, n, j, k: (i, n)),
        out_shape=jax.ShapeDtypeStruct((t, d), BF16),
        scratch_shapes=[pltpu.VMEM((tm, tn), F32)] * (1 if nk == 1 else 2),
        compiler_params=_cparams(("arbitrary", "arbitrary", "arbitrary", "arbitrary")),
        name="gated_merge",
    )(h, w_gate, b_gate, ys, w_branch)


def _softplus(x):
    return jnp.maximum(x, 0.0) + jnp.log1p(jnp.exp(-jnp.abs(x)))


def _ssd_kernel(xbc_ref, z_ref, dt_ref, dtt_ref, cw_ref, cb_ref, dtb_ref, dtbt_ref, alog_ref, alogt_ref,
                dskip_ref, ng_ref, expand_ref, ybuf_ref, o_ref, xe_ref, st_ref, *, n_heads, d_inner):
    del ybuf_ref
    q = SSD_CHUNK
    gn = SSD_N_GROUPS * SSD_D_STATE
    hpg = n_heads // SSD_N_GROUPS
    gw = hpg * SSD_HEAD_DIM
    c = pl.program_id(1)

    @pl.when(c == 0)
    def _():
        xe_ref[0:8, :] = jnp.zeros((8, xe_ref.shape[1]), F32)
        st_ref[...] = jnp.zeros(st_ref.shape, F32)

    xe_ref[8:8 + q, :] = xbc_ref[...]
    acc = cb_ref[...] + cw_ref[3:4, :] * xe_ref[8:8 + q, :]
    for kk in range(3):
        acc = acc + cw_ref[kk:kk + 1, :] * xe_ref[5 + kk:5 + kk + q, :]
    xe_ref[0:8, :] = xe_ref[q:q + 8, :]
    xc = acc * jax.nn.sigmoid(acc)
    xs = xc[:, :d_inner]
    bm = xc[:, d_inner:d_inner + gn].astype(BF16)
    cm = xc[:, d_inner + gn:].astype(BF16)

    dt = _softplus(dt_ref[...] + dtb_ref[...])
    da = dt * (-jnp.exp(alog_ref[...]))
    row = lax.broadcasted_iota(jnp.int32, (q, q), 0)
    col = lax.broadcasted_iota(jnp.int32, (q, q), 1)
    lower = row >= col
    tril = jnp.where(lower, 1.0, 0.0).astype(BF16)
    acs = _dot_exact_lhs01(tril, da)
    dtt = _softplus(dtt_ref[0] + dtbt_ref[...])
    dat = dtt * (-jnp.exp(alogt_ref[...]))
    triu = jnp.where(row <= col, 1.0, 0.0).astype(BF16)
    acst = _dot_exact_rhs01(dat, triu)

    expand = expand_ref[...]
    dt_x = _dot_exact_rhs01(dt, expand)
    acs_x = _dot_exact_rhs01(acs, expand)
    tot_x = acs_x[q - 1:q, :]
    xdt = xs * dt_x
    xdt_b = xdt.astype(BF16)
    xw_b = (xdt * jnp.exp(tot_x - acs_x)).astype(BF16)
    in_decay = jnp.exp(acs_x)
    chunk_decay = jnp.exp(tot_x)

    lane = lax.broadcasted_iota(jnp.int32, (1, gw), 1)
    ys = []
    for g in range(SSD_N_GROUPS):
        cg = cm[:, g * SSD_D_STATE:(g + 1) * SSD_D_STATE]
        bg = bm[:, g * SSD_D_STATE:(g + 1) * SSD_D_STATE]
        sl = slice(g * gw, (g + 1) * gw)
        gmat = lax.dot_general(cg, bg, (((1,), (1,)), ((), ())), preferred_element_type=F32)
        state = st_ref[g]
        y_g = _dot(cg, state.astype(BF16)) * in_decay[:, sl]
        xg = xdt_b[:, sl]
        for hh in range(hpg):
            hd = g * hpg + hh
            seg = acs[:, hd:hd + 1] - acst[hd:hd + 1, :]
            lmat = jnp.exp(jnp.where(lower, seg, -jnp.inf))
            mh = (gmat * lmat).astype(BF16)
            head_cols = (lane >= hh * SSD_HEAD_DIM) & (lane < (hh + 1) * SSD_HEAD_DIM)
            y_g = y_g + _dot(mh, jnp.where(head_cols, xg, jnp.zeros_like(xg)))
        upd = lax.dot_general(bg, xw_b[:, sl], (((0,), (0,)), ((), ())), preferred_element_type=F32)
        st_ref[g] = state * chunk_decay[:, sl] + upd
        ys.append(y_g)
    y = jnp.concatenate(ys, axis=-1) + xs * dskip_ref[...]
    zz = z_ref[...]
    y = y * (zz * jax.nn.sigmoid(zz))
    outs = []
    for g in range(SSD_N_GROUPS):
        yg = y[:, g * gw:(g + 1) * gw]
        ms = jnp.mean(yg * yg, axis=-1, keepdims=True)
        outs.append(yg * lax.rsqrt(ms + RMS_EPS))
    o_ref[0] = (jnp.concatenate(outs, axis=-1) * ng_ref[...]).astype(o_ref.dtype)


def ssd_mixer(proj, dt_t, cols, bsz, seq, conv_w, conv_b, dt_bias, a_log, d_skip, norm_g, ybuf, slot):
    n_heads = dt_bias.shape[0]
    d_inner = n_heads * SSD_HEAD_DIM
    conv_dim = conv_w.shape[1]
    q = SSD_CHUNK
    nc = seq // q
    pad = LANES - n_heads
    expand = jnp.repeat(jnp.eye(LANES, dtype=BF16)[:, :n_heads], SSD_HEAD_DIM, axis=1)
    dtb = jnp.pad(dt_bias, (0, pad)).reshape(1, LANES)
    alog = jnp.pad(a_log, (0, pad)).reshape(1, LANES)
    x_blk, z_blk, dt_blk = cols["xbc"] // conv_dim, cols["z"] // d_inner, cols["dt"] // LANES
    const = lambda b, c: (0, 0)
    return pl.pallas_call(
        functools.partial(_ssd_kernel, n_heads=n_heads, d_inner=d_inner),
        grid=(bsz, nc),
        in_specs=[
            pl.BlockSpec((q, conv_dim), lambda b, c: (b * nc + c, x_blk)),
            pl.BlockSpec((q, d_inner), lambda b, c: (b * nc + c, z_blk)),
            pl.BlockSpec((q, LANES), lambda b, c: (b * nc + c, dt_blk)),
            pl.BlockSpec((1, n_heads, q), lambda b, c: (b, 0, c)),
            pl.BlockSpec((4, conv_dim), const),
            pl.BlockSpec((1, conv_dim), const),
            pl.BlockSpec((1, LANES), const),
            pl.BlockSpec((n_heads, 1), const),
            pl.BlockSpec((1, LANES), const),
            pl.BlockSpec((n_heads, 1), const),
            pl.BlockSpec((1, d_inner), const),
            pl.BlockSpec((1, d_inner), const),
            pl.BlockSpec((LANES, d_inner), const),
            pl.BlockSpec(memory_space=pl.ANY),
        ],
        out_specs=pl.BlockSpec((1, q, d_inner), lambda b, c: (slot, b * nc + c, 0)),
        out_shape=jax.ShapeDtypeStruct(ybuf.shape, ybuf.dtype),
        scratch_shapes=[pltpu.VMEM((q + 8, conv_dim), F32),
                        pltpu.VMEM((SSD_N_GROUPS, SSD_D_STATE, d_inner // SSD_N_GROUPS), F32)],
        input_output_aliases={13: 0},
        compiler_params=_cparams(("parallel", "arbitrary")),
        name="ssd_mixer",
    )(proj, proj, proj, dt_t, conv_w, conv_b.reshape(1, conv_dim), dtb, dt_bias.reshape(n_heads, 1),
      alog, a_log.reshape(n_heads, 1), jnp.repeat(d_skip, SSD_HEAD_DIM).reshape(1, d_inner),
      norm_g.reshape(1, d_inner), expand, ybuf)


S5_ROWS = 8


def _s5_disc_kernel(are_ref, aim_ref, ldt_ref, brt_ref, bit_ref, pr_ref, pi_ref, bbr_ref, bbi_ref):
    dt = jnp.exp(ldt_ref[...])
    ar, ai = are_ref[...], aim_ref[...]
    mag = jnp.exp(ar * dt)
    lr, li = mag * jnp.cos(ai * dt), mag * jnp.sin(ai * dt)
    den = ar * ar + ai * ai
    fr = ((lr - 1.0) * ar + li * ai) / den
    fi = (li * ar - (lr - 1.0) * ai) / den
    pr, pi = lr, li
    for k in range(S5_ROWS):
        pr_ref[k] = pr
        pi_ref[k] = pi
        pr, pi = pr * lr - pi * li, pr * li + pi * lr
    br, bi = brt_ref[...], bit_ref[...]
    bbr_ref[...] = fr[:, None, :] * br - fi[:, None, :] * bi
    bbi_ref[...] = fr[:, None, :] * bi + fi[:, None, :] * br


def s5_discretise(a_re, a_im, log_dt, b_re, b_im):
    g, n = a_re.shape
    c = b_re.shape[2]
    brt, bit = jnp.swapaxes(b_re, 1, 2), jnp.swapaxes(b_im, 1, 2)
    return pl.pallas_call(
        _s5_disc_kernel,
        out_shape=[jax.ShapeDtypeStruct((S5_ROWS, g, n), F32), jax.ShapeDtypeStruct((S5_ROWS, g, n), F32),
                   jax.ShapeDtypeStruct((g, c, n), F32), jax.ShapeDtypeStruct((g, c, n), F32)],
        name="s5_discretise",
    )(a_re, a_im, log_dt.reshape(g, 1), brt, bit)


def _block_diag(blocks, per):
    g, r, c = blocks.shape
    nb = g // per
    eye = jnp.eye(per, dtype=blocks.dtype)
    out = blocks.reshape(nb, per, r, 1, c) * eye[None, :, None, :, None]
    return out.reshape(nb, per * r, per * c)


def _s5_kernel(u_ref, wr_ref, wi_ref, cr_ref, ci_ref, ctab_ref, d_ref, wglu_ref, bglu_ref, ybuf_ref, o_ref,
               xr_scr, xi_scr, carry_ref, y_scr, *, nblk, sw):
    del ybuf_ref
    tm = u_ref.shape[0]

    @pl.when(pl.program_id(1) == 0)
    def _():
        carry_ref[...] = jnp.zeros(carry_ref.shape, F32)

    for blk in range(nblk):
        cs = slice(blk * sw, (blk + 1) * sw)
        ls = slice(blk * LANES, (blk + 1) * LANES)
        u = u_ref[:, ls]
        ub = u.astype(BF16)
        xr_scr[...] = _dot(ub, wr_ref[blk])
        xi_scr[...] = _dot(ub, wi_ref[blk])

        def body(i, carry, cs=cs):
            c_re, c_im = carry
            r0 = pl.multiple_of(i * S5_ROWS, S5_ROWS)
            sr, si = xr_scr[pl.ds(r0, S5_ROWS), :], xi_scr[pl.ds(r0, S5_ROWS), :]
            for lvl, dist in enumerate((1, 2, 4)):
                lr, li = ctab_ref[2 * lvl, :, cs], ctab_ref[2 * lvl + 1, :, cs]
                tr, ti = pltpu.roll(sr, dist, axis=0), pltpu.roll(si, dist, axis=0)
                sr, si = sr + lr * tr - li * ti, si + lr * ti + li * tr
            pr, pi = ctab_ref[6, :, cs], ctab_ref[7, :, cs]
            sr, si = sr + pr * c_re - pi * c_im, si + pr * c_im + pi * c_re
            xr_scr[pl.ds(r0, S5_ROWS), :] = sr
            xi_scr[pl.ds(r0, S5_ROWS), :] = si
            return sr[S5_ROWS - 1:S5_ROWS, :], si[S5_ROWS - 1:S5_ROWS, :]

        c_re, c_im = lax.fori_loop(0, tm // S5_ROWS, body, (carry_ref[0:1, cs], carry_ref[1:2, cs]), unroll=2)
        carry_ref[0:1, cs] = c_re
        carry_ref[1:2, cs] = c_im
        y = _dot(xr_scr[...].astype(BF16), cr_ref[blk]) - _dot(xi_scr[...].astype(BF16), ci_ref[blk])
        y_scr[:, ls] = jax.nn.gelu(y + d_ref[:, ls] * u)
    y = y_scr[...]
    o_ref[0] = (y * jax.nn.sigmoid(_dot(y.astype(BF16), wglu_ref[...]) + bglu_ref[...])).astype(o_ref.dtype)


def s5_mixer(proj, u_col, bsz, seq, a_re, a_im, b_re, b_im, c_re, c_im, log_dt, d_skip, w_glu, b_glu,
             ybuf, slot):
    t = bsz * seq
    g, n = a_re.shape
    width = g * S5_GROUP
    per = S5_GROUPS_PER_BLOCK
    nblk = g // per
    sw = per * n
    ns = g * n
    pw_r, pw_i, bbr, bbi = s5_discretise(a_re, a_im, log_dt, b_re, b_im)
    w_r = _block_diag(bbr, per).astype(BF16)
    w_i = _block_diag(bbi, per).astype(BF16)
    c_r = _block_diag(jnp.swapaxes(c_re, 1, 2), per).astype(BF16)
    c_i = _block_diag(jnp.swapaxes(c_im, 1, 2), per).astype(BF16)
    pw_r, pw_i = pw_r.reshape(S5_ROWS, ns), pw_i.reshape(S5_ROWS, ns)
    sub = jnp.arange(S5_ROWS)[:, None]
    kinds = []
    for dist in (1, 2, 4):
        kinds += [jnp.where(sub >= dist, pw_r[dist - 1][None, :], 0.0),
                  jnp.where(sub >= dist, pw_i[dist - 1][None, :], 0.0)]
    ctab = jnp.stack(kinds + [pw_r, pw_i])
    tm = _tile(seq, 256)
    nt = seq // tm
    const2 = lambda b, c: (0, 0)
    const3 = lambda b, c: (0, 0, 0)
    return pl.pallas_call(
        functools.partial(_s5_kernel, nblk=nblk, sw=sw),
        grid=(bsz, nt),
        in_specs=[pl.BlockSpec((tm, width), lambda b, c: (b * nt + c, u_col // width)),
                  pl.BlockSpec((nblk, LANES, sw), const3), pl.BlockSpec((nblk, LANES, sw), const3),
                  pl.BlockSpec((nblk, sw, LANES), const3), pl.BlockSpec((nblk, sw, LANES), const3),
                  pl.BlockSpec((8, S5_ROWS, ns), const3),
                  pl.BlockSpec((1, width), const2), pl.BlockSpec((width, width), const2),
                  pl.BlockSpec((1, width), const2), pl.BlockSpec(memory_space=pl.ANY)],
        out_specs=pl.BlockSpec((1, tm, width), lambda b, c: (slot, b * nt + c, 0)),
        out_shape=jax.ShapeDtypeStruct(ybuf.shape, ybuf.dtype),
        scratch_shapes=[pltpu.VMEM((tm, sw), F32), pltpu.VMEM((tm, sw), F32), pltpu.VMEM((8, ns), F32),
                        pltpu.VMEM((tm, width), F32)],
        input_output_aliases={9: 0},
        compiler_params=_cparams(("parallel", "arbitrary")),
        name="s5_mixer",
    )(proj, w_r, w_i, c_r, c_i, ctab, d_skip.reshape(1, width), w_glu.astype(BF16), b_glu.reshape(1, width),
      ybuf)


def _flash_kernel(*refs, tq, n_maps, has_bias, finalize):
    if has_bias:
        q_ref, k_ref, v_ref, bias_ref, far_ref, *rest = refs
    else:
        q_ref, k_ref, v_ref, *rest = refs
    fin_refs = rest[:-2]
    o_ref = rest[-1]
    h = pl.program_id(1)
    qi = pl.program_id(2)
    rows = n_maps * tq
    dvo = v_ref.shape[1]
    qv = q_ref[...].reshape(rows, q_ref.shape[-1])

    def tiled(b):
        return jnp.concatenate([b] * n_maps, axis=0) if n_maps > 1 else b

    def block(kb, carry, add, shift):
        m, l, acc = carry
        off = pl.multiple_of(kb * tq, tq)
        s = lax.dot_general(qv, k_ref[pl.ds(off, tq), :], (((1,), (1,)), ((), ())), preferred_element_type=F32)
        sb = s.astype(BF16)
        if add is not None:
            sb = sb + add
        mx = jnp.max(sb, axis=-1, keepdims=True).astype(F32)
        m_hi = jnp.maximum(m, mx if shift is None else mx + shift)
        ref_b = (m_hi if shift is None else m_hi - shift).astype(BF16)
        m_new = ref_b.astype(F32) if shift is None else ref_b.astype(F32) + shift
        alpha = jnp.exp2(m - m_new)
        p = jnp.exp2(sb - ref_b)
        l_new = alpha * l + jnp.sum(p.astype(F32), axis=-1, keepdims=True)
        return m_new, l_new, alpha * acc + _dot(p, v_ref[pl.ds(off, tq), :])

    carry = (jnp.full((rows, 1), -jnp.inf, F32), jnp.zeros((rows, 1), F32), jnp.zeros((rows, dvo), F32))
    if has_bias:
        far = far_ref[h]
        n_far = jnp.maximum(qi - 1, 0)
        carry = lax.fori_loop(0, n_far, lambda kb, c: block(kb, c, None, far), carry)
        carry = lax.fori_loop(n_far, qi, lambda kb, c: block(kb, c, tiled(bias_ref[0, 1]), None), carry)
    else:
        carry = lax.fori_loop(0, qi, lambda kb, c: block(kb, c, None, None), carry)
    row = lax.broadcasted_iota(jnp.int32, (tq, tq), 0)
    col = lax.broadcasted_iota(jnp.int32, (tq, tq), 1)
    diag = jnp.where(row >= col, bias_ref[0, 0].astype(F32) if has_bias else 0.0, -jnp.inf).astype(BF16)
    m, l, acc = block(qi, carry, tiled(diag), None)
    o_ref[0] = finalize(acc / l, [r[...] for r in fin_refs]).astype(o_ref.dtype)


def flash_attention(q, k, v, bsz, seq, n_heads, dq, ybuf, slot, n_maps=1, bias=None, far=None, fin=(),
                    finalize=None, tq=512):
    dv = v.shape[1] // n_heads
    tq = _tile(seq, tq)
    nq = seq // tq
    has_bias = bias is not None
    if finalize is None:
        finalize = lambda o, extras: o
    in_specs = [pl.BlockSpec((n_maps, tq, dq), lambda b, h, i: (0, b * nq + i, h)),
                pl.BlockSpec((seq, dq), lambda b, h, i: (b, h)),
                pl.BlockSpec((seq, dv), lambda b, h, i: (b, h))]
    args = [q, k, v]
    if has_bias:
        in_specs += [pl.BlockSpec((1, 2, tq, tq), lambda b, h, i: (h, 0, 0, 0)),
                     pl.BlockSpec(memory_space=pltpu.SMEM)]
        args += [bias, far]
    for f in fin:
        in_specs.append(pl.BlockSpec(f.shape, lambda b, h, i: (0, 0)))
        args.append(f)
    in_specs.append(pl.BlockSpec(memory_space=pl.ANY))
    args.append(ybuf)
    return pl.pallas_call(
        functools.partial(_flash_kernel, tq=tq, n_maps=n_maps, has_bias=has_bias, finalize=finalize),
        grid=(bsz, n_heads, nq),
        in_specs=in_specs,
        out_specs=pl.BlockSpec((1, tq, dv), lambda b, h, i: (slot, b * nq + i, h)),
        out_shape=jax.ShapeDtypeStruct(ybuf.shape, ybuf.dtype),
        input_output_aliases={len(args) - 1: 0},
        compiler_params=_cparams(("parallel", "parallel", "arbitrary")),
        name="flash_bias" if has_bias else "flash",
    )(*args)


def _mla_proj_kernel(cq_ref, ckv_ref, kpe_ref, cos_ref, sin_ref, qng_ref, kvng_ref, wq_ref, wkv_ref,
                     qg_ref, kg_ref, q_ref, k_ref, v_ref, *, n_heads, scale):
    half = LANES // 2

    def norm(x, g):
        return (x * lax.rsqrt(jnp.mean(x * x, axis=-1, keepdims=True) + RMS_EPS) * g).astype(BF16)

    cq = norm(cq_ref[...], qng_ref[...])
    ckv = norm(ckv_ref[...], kvng_ref[...])
    cos, sin = cos_ref[...], sin_ref[...]
    kpe = kpe_ref[...]
    kpe_ss = jnp.sum(jnp.where(lax.broadcasted_iota(jnp.int32, kpe.shape, 1) < half, kpe * kpe, 0.0),
                     axis=-1, keepdims=True)
    n_qk = float(MLA_NOPE + MLA_ROPE)
    qg, kg = qg_ref[...], kg_ref[...]

    def rotary(hi, gains, inv):
        return inv * (hi * (gains[1:2] * cos) + pltpu.roll(hi, half, axis=1) * (gains[2:3] * sin))

    k_rot_raw = kpe * (kg[1:2] * cos) + pltpu.roll(kpe, half, axis=1) * (kg[2:3] * sin)
    for h in range(n_heads):
        xq = _dot(cq, wq_ref[:, h * MLA_PAD:(h + 1) * MLA_PAD])
        lo, hi = xq[:, :LANES], xq[:, LANES:]
        ss = jnp.sum(lo * lo, axis=-1, keepdims=True) + 0.5 * jnp.sum(hi * hi, axis=-1, keepdims=True)
        inv = lax.rsqrt(ss / n_qk + RMS_EPS) * scale
        q_ref[:, h * MLA_PAD:h * MLA_PAD + LANES] = (lo * inv * qg[0:1]).astype(BF16)
        q_ref[:, h * MLA_PAD + LANES:(h + 1) * MLA_PAD] = rotary(hi, qg, inv).astype(BF16)
        xkv = _dot(ckv, wkv_ref[:, h * MLA_PAD:(h + 1) * MLA_PAD])
        kn = xkv[:, :LANES]
        inv_k = lax.rsqrt((jnp.sum(kn * kn, axis=-1, keepdims=True) + kpe_ss) / n_qk + RMS_EPS)
        k_ref[:, h * MLA_PAD:h * MLA_PAD + LANES] = (kn * inv_k * kg[0:1]).astype(BF16)
        k_ref[:, h * MLA_PAD + LANES:(h + 1) * MLA_PAD] = (k_rot_raw * inv_k).astype(BF16)
        v_ref[:, h * LANES:(h + 1) * LANES] = xkv[:, LANES:].astype(BF16)


def _rope_gain_rows(gain):
    rot = gain[MLA_NOPE:]
    r = MLA_ROPE // 2
    swapped = jnp.concatenate([rot[r:], rot[:r]])
    z = jnp.zeros((LANES - MLA_ROPE,), F32)
    return jnp.stack([gain[:MLA_NOPE], jnp.concatenate([rot, z]), jnp.concatenate([swapped, z])])


def mla_mixer(proj, cols, bsz, seq, q_norm_g, w_q_up, kv_norm_g, w_kv_up, q_gain, k_gain, ybuf, slot):
    t = bsz * seq
    hds = MLA_N_HEADS
    q_rank, kv_rank = w_q_up.shape[0], w_kv_up.shape[0]
    r = MLA_ROPE // 2
    wq = w_q_up.reshape(q_rank, hds, MLA_NOPE + MLA_ROPE)
    wq = jnp.concatenate([wq, wq[:, :, MLA_NOPE + r:], wq[:, :, MLA_NOPE:MLA_NOPE + r]], axis=-1)
    wq = wq.reshape(q_rank, hds * MLA_PAD).astype(BF16)
    wkv = w_kv_up.astype(BF16)
    inv = 1.0 / (ROPE_THETA ** (jnp.arange(0, MLA_ROPE, 2, dtype=F32) / MLA_ROPE))
    ang = jnp.arange(seq, dtype=F32)[:, None] * inv[None, :]
    zer = jnp.zeros((seq, LANES - MLA_ROPE), F32)
    cos_t = jnp.concatenate([jnp.cos(ang), jnp.cos(ang), zer], axis=-1)
    sin_t = jnp.concatenate([-jnp.sin(ang), jnp.sin(ang), zer], axis=-1)
    tm = _tile(seq, 512)
    npos = seq // tm
    const = lambda i: (0, 0)
    q, k, v = pl.pallas_call(
        functools.partial(_mla_proj_kernel, n_heads=hds, scale=LOG2E * float(MLA_NOPE + MLA_ROPE) ** -0.5),
        grid=(t // tm,),
        in_specs=[pl.BlockSpec((tm, q_rank), lambda i: (i, cols["cq"] // q_rank)),
                  pl.BlockSpec((tm, kv_rank), lambda i: (i, cols["ckv"] // kv_rank)),
                  pl.BlockSpec((tm, LANES), lambda i: (i, cols["kpe"] // LANES)),
                  pl.BlockSpec((tm, LANES), lambda i: (i % npos, 0)),
                  pl.BlockSpec((tm, LANES), lambda i: (i % npos, 0)),
                  pl.BlockSpec((1, q_rank), const), pl.BlockSpec((1, kv_rank), const),
                  pl.BlockSpec((q_rank, hds * MLA_PAD), const), pl.BlockSpec((kv_rank, hds * MLA_PAD), const),
                  pl.BlockSpec((3, LANES), const), pl.BlockSpec((3, LANES), const)],
        out_specs=[pl.BlockSpec((tm, hds * MLA_PAD), lambda i: (i, 0)),
                   pl.BlockSpec((tm, hds * MLA_PAD), lambda i: (i, 0)),
                   pl.BlockSpec((tm, hds * LANES), lambda i: (i, 0))],
        out_shape=[jax.ShapeDtypeStruct((t, hds * MLA_PAD), BF16), jax.ShapeDtypeStruct((t, hds * MLA_PAD), BF16),
                   jax.ShapeDtypeStruct((t, hds * LANES), BF16)],
        compiler_params=_cparams(("parallel",)),
        name="mla_proj",
    )(proj, proj, proj, cos_t, sin_t, q_norm_g.reshape(1, q_rank), kv_norm_g.reshape(1, kv_rank), wq, wkv,
      _rope_gain_rows(q_gain), _rope_gain_rows(k_gain))
    return flash_attention(q.reshape(1, t, hds * MLA_PAD), k, v, bsz, seq, hds, MLA_PAD, ybuf, slot, tq=1024)


def _diff_prep_kernel(q_ref, k_ref, v_ref, qg_ref, kg_ref, seg_ref, qo_ref, ko_ref, vo_ref, *, hd, scale):
    seg = seg_ref[...]
    lane = lax.broadcasted_iota(jnp.int32, (1, LANES), 1)
    n_col = q_ref.shape[1] // LANES
    for c in range(n_col):
        sl = slice(c * LANES, (c + 1) * LANES)
        xq, xk = q_ref[:, sl], k_ref[:, sl]
        ssq = _dot_exact_rhs01(xq * xq, seg)
        ssk = _dot_exact_rhs01(xk * xk, seg)
        qn = xq * lax.rsqrt(ssq / hd + RMS_EPS) * (qg_ref[...] * scale)
        kn = xk * lax.rsqrt(ssk / hd + RMS_EPS) * kg_ref[...]
        qo_ref[0, :, sl] = jnp.where(lane < hd, qn, 0.0).astype(BF16)
        qo_ref[1, :, sl] = jnp.where(lane >= hd, qn, 0.0).astype(BF16)
        ko_ref[:, sl] = kn.astype(BF16)
    vo_ref[...] = v_ref[...].astype(BF16)


def _t5_bucket_np(n):
    max_exact = REL_BUCKETS // 2
    nf = np.maximum(n, 1).astype(np.float32)
    large = max_exact + (np.log(nf / np.float32(max_exact)) / np.float32(math.log(REL_MAX_DIST / max_exact))
                         * np.float32(REL_BUCKETS - max_exact)).astype(np.int32)
    large = np.minimum(large, REL_BUCKETS - 1)
    return np.where(n < max_exact, n, large).astype(np.int32)


def _bias_tiles_kernel(bkt_ref, rb_ref, o_ref):
    h = pl.program_id(0)
    for d in range(2):
        bkt = bkt_ref[d]
        tile = jnp.zeros(bkt.shape, F32)
        for b in range(REL_BUCKETS):
            tile = jnp.where(bkt == b, rb_ref[b, h] * LOG2E, tile)
        o_ref[0, d] = tile.astype(o_ref.dtype)


def _diff_finalize(lambda_init, tq):
    def fin(o, extras):
        lq1, lk1, lq2, lk2, sub_g = extras
        lam = (jnp.exp(jnp.sum(lq1 * lk1, axis=-1, keepdims=True))
               - jnp.exp(jnp.sum(lq2 * lk2, axis=-1, keepdims=True)) + lambda_init)
        d = o[:tq] - lam * o[tq:]
        ms = jnp.mean(d * d, axis=-1, keepdims=True)
        return d * lax.rsqrt(ms + RMS_EPS) * (sub_g * (1.0 - lambda_init))
    return fin


def diff_mixer(proj, cols, bsz, seq, q_gain, k_gain, lq1, lk1, lq2, lk2, subln_g, rel_bias, lambda_init,
               ybuf, slot):
    t = bsz * seq
    hds = DIFF_N_HEADS
    hd = q_gain.shape[0]
    width = hds * 2 * hd
    tm = _tile(t, 512)
    seg = (np.arange(LANES)[:, None] // hd == np.arange(LANES)[None, :] // hd)
    seg = jnp.asarray(seg, BF16)
    gq, gk = jnp.tile(q_gain, LANES // hd).reshape(1, LANES), jnp.tile(k_gain, LANES // hd).reshape(1, LANES)
    const = lambda i: (0, 0)
    qd, kd, vd = pl.pallas_call(
        functools.partial(_diff_prep_kernel, hd=hd, scale=LOG2E * float(hd) ** -0.5),
        grid=(t // tm,),
        in_specs=[pl.BlockSpec((tm, width), lambda i: (i, cols["dq"] // width)),
                  pl.BlockSpec((tm, width), lambda i: (i, cols["dk"] // width)),
                  pl.BlockSpec((tm, width), lambda i: (i, cols["dv"] // width)),
                  pl.BlockSpec((1, LANES), const), pl.BlockSpec((1, LANES), const),
                  pl.BlockSpec((LANES, LANES), const)],
        out_specs=[pl.BlockSpec((2, tm, width), lambda i: (0, i, 0)),
                   pl.BlockSpec((tm, width), lambda i: (i, 0)), pl.BlockSpec((tm, width), lambda i: (i, 0))],
        out_shape=[jax.ShapeDtypeStruct((2, t, width), BF16), jax.ShapeDtypeStruct((t, width), BF16),
                   jax.ShapeDtypeStruct((t, width), BF16)],
        compiler_params=_cparams(("parallel",)),
        name="diff_prep",
    )(proj, proj, proj, gq, gk, seg)

    tq = _tile(seq, 512)
    assert tq + 1 > REL_MAX_DIST, "blocks two or more before the diagonal must all be at the far distance"
    idx = np.arange(tq)
    dist0 = np.maximum(idx[:, None] - idx[None, :], 0)
    dist1 = idx[:, None] - idx[None, :] + tq
    bkt = jnp.asarray(np.stack([_t5_bucket_np(dist0), _t5_bucket_np(dist1)]))
    bias = pl.pallas_call(
        _bias_tiles_kernel,
        grid=(hds,),
        in_specs=[pl.BlockSpec((2, tq, tq), lambda h: (0, 0, 0)), pl.BlockSpec(memory_space=pltpu.SMEM)],
        out_specs=pl.BlockSpec((1, 2, tq, tq), lambda h: (h, 0, 0, 0)),
        out_shape=jax.ShapeDtypeStruct((hds, 2, tq, tq), BF16),
        compiler_params=_cparams(("parallel",)),
        name="t5_bias_tiles",
    )(bkt, rel_bias)
    far = rel_bias[REL_BUCKETS - 1] * LOG2E
    fin = tuple(a.reshape(1, -1) for a in (lq1, lk1, lq2, lk2, subln_g))
    return flash_attention(qd, kd, vd, bsz, seq, hds, 2 * hd, ybuf, slot, n_maps=2, bias=bias, far=far, fin=fin,
                           finalize=_diff_finalize(lambda_init, tq), tq=tq)


def _pack_w_in(w_in_l, widths):
    (z_w, xbc_w, dt_w, u_w, cq_w, ckv_w, kpe_w, dq_w, dk_w, dv_w) = widths
    lead = w_in_l.shape[:-1]
    starts = np.cumsum([0] + list(widths))
    seg = {n: w_in_l[..., starts[i]:starts[i + 1]]
           for i, n in enumerate(("z", "xbc", "dt", "u", "cq", "ckv", "kpe", "dq", "dk", "dv"))}
    r = kpe_w // 2
    kpe2 = jnp.concatenate([seg["kpe"], seg["kpe"][..., r:], seg["kpe"][..., :r],
                            jnp.zeros(lead + (LANES - 2 * kpe_w,), F32)], axis=-1)
    dtp = jnp.concatenate([seg["dt"], jnp.zeros(lead + (LANES - dt_w,), F32)], axis=-1)
    order = [("xbc", seg["xbc"]), ("z", seg["z"]), ("u", seg["u"]), ("dq", seg["dq"]), ("dk", seg["dk"]),
             ("dv", seg["dv"]), ("ckv", seg["ckv"]), ("cq", seg["cq"]), ("kpe", kpe2), ("dt", dtp)]
    cols, off = {}, 0
    for name, w in order:
        assert off % w.shape[-1] == 0, (name, off, w.shape[-1])
        cols[name] = off
        off += w.shape[-1]
    return jnp.concatenate([w for _, w in order], axis=-1).astype(BF16), cols


def kernel(x, norm_mix_g, w_in, ssd_conv_w, ssd_conv_b, ssd_dt_bias, ssd_a_log, ssd_d, ssd_norm_g, s5_a_re, s5_a_im, s5_b_re, s5_b_im, s5_c_re, s5_c_im, s5_log_dt, s5_d, s5_w_glu, s5_b_glu, mla_q_norm_g, mla_w_q_up, mla_kv_norm_g, mla_w_kv_up, mla_q_gain, mla_k_gain, diff_q_gain, diff_k_gain, diff_lq1, diff_lk1, diff_lq2, diff_lk2, diff_subln_g, rel_bias, w_gate, b_gate, w_branch, w_out, norm_ffn_g, ffn_w_gate, ffn_w_up, ffn_w_down, moe_w_router, moe_b_router, moe_w_gate, moe_w_up, moe_w_down):
    bsz, seq, d = x.shape
    t = bsz * seq
    depth = w_in.shape[0]
    mix = d // 4
    n_ssd_heads = ssd_dt_bias.shape[1]
    widths = (mix, ssd_conv_w.shape[2], n_ssd_heads, mix, mla_w_q_up.shape[1], mla_w_kv_up.shape[1], MLA_ROPE,
              mix, mix, mix)
    xf = x.reshape(t, d)
    w_in_b, cols = _pack_w_in(w_in, widths)
    w_gate_b, w_branch_b, w_out_b = w_gate.astype(BF16), w_branch.astype(BF16), w_out.astype(BF16)
    ffn_gate_b, ffn_up_b, ffn_down_b = ffn_w_gate.astype(BF16), ffn_w_up.astype(BF16), ffn_w_down.astype(BF16)
    for i in range(depth):
        h = rmsnorm(xf, norm_mix_g[i])
        proj = matmul(h, [w_in_b], _epi_plain, out_dtype=F32, tn=512, w_prefix=(i,), name="in_proj")
        dt_t = jnp.swapaxes(proj[:, cols["dt"]:cols["dt"] + n_ssd_heads].reshape(bsz, seq, n_ssd_heads), 1, 2)
        ys = ssd_mixer(proj, dt_t, cols, bsz, seq, ssd_conv_w[i], ssd_conv_b[i], ssd_dt_bias[i], ssd_a_log[i],
                       ssd_d[i], ssd_norm_g[i], jnp.zeros((N_BRANCH, t, mix), BF16), 0)
        ys = s5_mixer(proj, cols["u"], bsz, seq, s5_a_re[i], s5_a_im[i], s5_b_re[i], s5_b_im[i], s5_c_re[i],
                      s5_c_im[i], s5_log_dt[i], s5_d[i], s5_w_glu[i], s5_b_glu[i], ys, 1)
        ys = mla_mixer(proj, cols, bsz, seq, mla_q_norm_g[i], mla_w_q_up[i], mla_kv_norm_g[i], mla_w_kv_up[i],
                       mla_q_gain[i], mla_k_gain[i], ys, 2)
        lambda_init = 0.8 - 0.6 * math.exp(-0.3 * i)
        ys = diff_mixer(proj, cols, bsz, seq, diff_q_gain[i], diff_k_gain[i], diff_lq1[i], diff_lk1[i],
                        diff_lq2[i], diff_lk2[i], diff_subln_g[i], rel_bias, lambda_init, ys, 3)
        merged = gated_merge(h, w_gate_b, b_gate[i][:, None, :], ys, w_branch_b, i)
        xf = matmul(merged, [w_out_b], _epi_residual, extras=(xf,), extra_kinds=("mn",), tk=2048, w_prefix=(i,),
                    name="out_proj")
        e = i // 2
        if i % 2 == 0:
            h2 = rmsnorm(xf, norm_ffn_g[i])
            act = matmul(h2, [ffn_gate_b, ffn_up_b], _epi_swiglu, out_dtype=BF16, tn=512, w_prefix=(e,),
                         name="ffn_up")
            xf = matmul(act, [ffn_down_b], _epi_residual, extras=(xf,), extra_kinds=("mn",), tk=3584,
                        w_prefix=(e,), name="ffn_down")
        else:
            xf = moe_block(xf, norm_ffn_g[i], moe_w_router[e], moe_b_router[e], moe_w_gate[e], moe_w_up[e],
                           moe_w_down[e])
    return xf.reshape(bsz, seq, d)
```

```python
import functools
import math

import numpy as np
import jax
import jax.numpy as jnp
from jax import lax
from jax.experimental import pallas as pl
from jax.experimental.pallas import tpu as pltpu

F32 = jnp.float32
BF16 = jnp.bfloat16

V7X_VMEM_BYTES = 64 * 1024 * 1024
VMEM_LIMIT = V7X_VMEM_BYTES - 8 * 1024 * 1024
LANES = 128

RMS_EPS = 1e-6
SSD_HEAD_DIM = 64
SSD_N_GROUPS = 4
SSD_D_STATE = 128
SSD_CHUNK = 128
S5_GROUP = 16
S5_STATE = 64
S5_GROUPS_PER_BLOCK = 8
MLA_N_HEADS = 8
MLA_NOPE = 128
MLA_ROPE = 64
MLA_PAD = 256
ROPE_THETA = 10000.0
DIFF_N_HEADS = 8
REL_BUCKETS = 32
REL_MAX_DIST = 128
TOP_K = 2
N_BRANCH = 4
LOG2E = math.log2(math.e)


def _cparams(sem):
    return pltpu.CompilerParams(dimension_semantics=sem, vmem_limit_bytes=VMEM_LIMIT)


def _tile(n, pref):
    t = min(n, pref)
    while n % t:
        t //= 2
    return t


def _split3(a):
    hi = a.astype(BF16)
    r1 = a - hi.astype(F32)
    mid = r1.astype(BF16)
    lo = (r1 - mid.astype(F32)).astype(BF16)
    return hi, mid, lo


def _dot(a, b):
    return jnp.dot(a, b, preferred_element_type=F32)


def _dot_exact_rhs01(a, sel):
    hi, mid, lo = _split3(a)
    return _dot(hi, sel) + _dot(mid, sel) + _dot(lo, sel)


def _dot_exact_lhs01(sel, b):
    hi, mid, lo = _split3(b)
    return _dot(sel, hi) + _dot(sel, mid) + _dot(sel, lo)


def _rmsnorm_kernel(x_ref, g_ref, o_ref):
    x = x_ref[...]
    ms = jnp.mean(x * x, axis=-1, keepdims=True)
    o_ref[...] = (x * lax.rsqrt(ms + RMS_EPS) * g_ref[...]).astype(o_ref.dtype)


def rmsnorm(x, g, tm=256):
    t, d = x.shape
    tm = _tile(t, tm)
    return pl.pallas_call(
        _rmsnorm_kernel,
        grid=(t // tm,),
        in_specs=[pl.BlockSpec((tm, d), lambda i: (i, 0)), pl.BlockSpec((1, d), lambda i: (0, 0))],
        out_specs=pl.BlockSpec((tm, d), lambda i: (i, 0)),
        out_shape=jax.ShapeDtypeStruct((t, d), BF16),
        compiler_params=_cparams(("parallel",)),
        name="rmsnorm",
    )(x, g.reshape(1, d))


def _rmsnorm_router_kernel(x_ref, g_ref, wr_ref, br_ref, hp_ref, route_ref, cnt_ref, run_ref, *, n_experts):
    @pl.when(pl.program_id(0) == 0)
    def _():
        run_ref[...] = jnp.zeros(run_ref.shape, F32)

    x = x_ref[...]
    tm, d = x.shape
    ms = jnp.mean(x * x, axis=-1, keepdims=True)
    h = x * lax.rsqrt(ms + RMS_EPS) * g_ref[...]
    bits = lax.bitcast_convert_type(h.astype(BF16).astype(F32), jnp.uint32)
    hp_ref[...] = (bits[:, :d // 2] >> 16) | (bits[:, d // 2:] & jnp.uint32(0xFFFF0000))
    logits = jnp.dot(h, wr_ref[...], preferred_element_type=F32, precision=lax.Precision.HIGHEST) + br_ref[...]
    lane = lax.broadcasted_iota(jnp.int32, logits.shape, 1)
    neg = jnp.float32(-jnp.inf)
    logits = jnp.where(lane < n_experts, logits, neg)
    m1 = jnp.max(logits, axis=-1, keepdims=True)
    i1 = jnp.min(jnp.where(logits == m1, lane, LANES), axis=-1, keepdims=True)
    rest = jnp.where(lane == i1, neg, logits)
    m2 = jnp.max(rest, axis=-1, keepdims=True)
    i2 = jnp.min(jnp.where(rest == m2, lane, LANES), axis=-1, keepdims=True)
    e2 = jnp.exp(m2 - m1)
    w1 = 1.0 / (1.0 + e2)
    w2 = e2 / (1.0 + e2)
    sel = jnp.where(lane == i1, 1.0, 0.0) + jnp.where(lane == i2, 1.0, 0.0)
    row = lax.broadcasted_iota(jnp.int32, (tm, tm), 0)
    col = lax.broadcasted_iota(jnp.int32, (tm, tm), 1)
    before = jnp.where(col < row, 1.0, 0.0).astype(BF16)
    excl = _dot(before, sel.astype(BF16)) + run_ref[0:1, :]
    run_ref[...] = run_ref[...] + jnp.sum(sel, axis=0, keepdims=True)
    cnt_ref[...] = run_ref[...]
    r1 = jnp.sum(jnp.where(lane == i1, excl, 0.0), axis=-1, keepdims=True)
    r2 = jnp.sum(jnp.where(lane == i2, excl, 0.0), axis=-1, keepdims=True)
    route = jnp.zeros(logits.shape, F32)
    for pos, val in enumerate((i1.astype(F32), i2.astype(F32), r1, r2, w1, w2)):
        route = jnp.where(lane == pos, val, route)
    route_ref[...] = route


def rmsnorm_router(x, g, w_router, b_router, tm=256):
    t, d = x.shape
    ne = w_router.shape[1]
    tm = _tile(t, tm)
    wr = jnp.zeros((d, LANES), F32).at[:, :ne].set(w_router)
    br = jnp.zeros((1, LANES), F32).at[0, :ne].set(b_router)
    return pl.pallas_call(
        functools.partial(_rmsnorm_router_kernel, n_experts=ne),
        grid=(t // tm,),
        in_specs=[pl.BlockSpec((tm, d), lambda i: (i, 0)), pl.BlockSpec((1, d), lambda i: (0, 0)),
                  pl.BlockSpec((d, LANES), lambda i: (0, 0)), pl.BlockSpec((1, LANES), lambda i: (0, 0))],
        out_specs=[pl.BlockSpec((tm, d // 2), lambda i: (i, 0)), pl.BlockSpec((tm, LANES), lambda i: (i, 0)),
                   pl.BlockSpec((8, LANES), lambda i: (0, 0))],
        out_shape=[jax.ShapeDtypeStruct((t, d // 2), jnp.uint32), jax.ShapeDtypeStruct((t, LANES), F32),
                   jax.ShapeDtypeStruct((8, LANES), F32)],
        scratch_shapes=[pltpu.VMEM((8, LANES), F32)],
        compiler_params=_cparams(("arbitrary",)),
        name="rmsnorm_router",
    )(x, g.reshape(1, d), wr, br)


MOE_ROW_TILE = 512
MOE_TOKEN_BLOCK = 256


def _unpack_halves(xp):
    lo = lax.bitcast_convert_type(xp << 16, F32).astype(BF16)
    hi = lax.bitcast_convert_type(xp & jnp.uint32(0xFFFF0000), F32).astype(BF16)
    return lo, hi


def _dispatch_kernel(src_ref, h_hbm, o_ref, buf, sem, *, tmg):
    def start(r2, c):
        for p in range(2):
            r = 2 * r2 + p
            pltpu.make_async_copy(h_hbm.at[pl.ds(src_ref[0, 0, r], 1)], buf.at[pl.ds(r, 1)], sem).start(
                priority=p)
        return c

    lax.fori_loop(0, tmg // 2, start, 0)

    def wait(r, c):
        pltpu.make_async_copy(h_hbm.at[pl.ds(0, 1)], buf.at[pl.ds(0, 1)], sem).wait()
        return c

    lax.fori_loop(0, tmg, wait, 0)
    lo, hi = _unpack_halves(buf[...])
    half = lo.shape[1]
    o_ref[:, :half] = lo
    o_ref[:, half:] = hi


def moe_dispatch(hp, src, n_rows):
    dw = hp.shape[1]
    tmg = MOE_ROW_TILE
    return pl.pallas_call(
        functools.partial(_dispatch_kernel, tmg=tmg),
        grid=(n_rows // tmg,),
        in_specs=[pl.BlockSpec((1, 1, tmg), lambda i: (i, 0, 0), memory_space=pltpu.SMEM),
                  pl.BlockSpec(memory_space=pl.ANY)],
        out_specs=pl.BlockSpec((tmg, 2 * dw), lambda i: (i, 0)),
        out_shape=jax.ShapeDtypeStruct((n_rows, 2 * dw), BF16),
        scratch_shapes=[pltpu.VMEM((tmg, dw), hp.dtype), pltpu.SemaphoreType.DMA(())],
        compiler_params=_cparams(("arbitrary",)),
        name="moe_dispatch",
    )(src.reshape(n_rows // tmg, 1, tmg), hp)


def _grouped_kernel(e_tab, jw_tab, rx_tab, j_tab, r_tab, first_tab, valid_tab, x_ref, *rest, n_w, epilogue):
    del e_tab, jw_tab, rx_tab, j_tab, r_tab
    w_refs, o_ref, wb_refs = rest[:n_w], rest[n_w], rest[n_w + 1:]
    s = pl.program_id(0)

    @pl.when(first_tab[s] == 1)
    def _():
        for w, wb in zip(w_refs, wb_refs):
            wb[...] = w[0].astype(BF16)

    @pl.when(valid_tab[s] == 1)
    def _():
        x = x_ref[...]
        o_ref[...] = epilogue([_dot(x, wb[...]) for wb in wb_refs], []).astype(o_ref.dtype)

    @pl.when(valid_tab[s] == 0)
    def _():
        o_ref[...] = jnp.zeros(o_ref.shape, o_ref.dtype)


def grouped_matmul(x, ws, tabs, epilogue, out_dtype, tn, name):
    n_rows, kx = x.shape
    ne, kdim, n = ws[0].shape
    tmg = MOE_ROW_TILE
    tn = _tile(n, tn)
    n_steps = tabs[0].shape[0]
    grid_spec = pltpu.PrefetchScalarGridSpec(
        num_scalar_prefetch=7,
        grid=(n_steps,),
        in_specs=[pl.BlockSpec((tmg, kx), lambda s, e, jw, rx, j, r, f, v: (rx[s], 0))]
        + [pl.BlockSpec((1, kdim, tn), lambda s, e, jw, rx, j, r, f, v: (e[s], 0, jw[s])) for _ in ws],
        out_specs=pl.BlockSpec((tmg, tn), lambda s, e, jw, rx, j, r, f, v: (r[s], j[s])),
        scratch_shapes=[pltpu.VMEM((kdim, tn), BF16) for _ in ws],
    )
    return pl.pallas_call(
        functools.partial(_grouped_kernel, n_w=len(ws), epilogue=epilogue),
        grid_spec=grid_spec,
        out_shape=jax.ShapeDtypeStruct((n_rows, n), out_dtype),
        compiler_params=_cparams(("arbitrary",)),
        name=name,
    )(*tabs, x, *ws)


def _group_steps(padded, nj, n_tiles):
    tmg = MOE_ROW_TILE
    tiles = padded // tmg
    tile_start = jnp.cumsum(tiles) - tiles
    steps = tiles * nj
    step_end = jnp.cumsum(steps)
    step_start = step_end - steps
    total = step_end[-1]
    used_tiles = jnp.sum(tiles)
    s = jnp.arange(n_tiles * nj, dtype=jnp.int32)
    sc = jnp.minimum(s, total - 1)
    e = jnp.minimum(jnp.searchsorted(step_end, sc, side="right"), tiles.shape[0] - 1).astype(jnp.int32)
    local = sc - step_start[e]
    te = jnp.maximum(tiles[e], 1)
    jw = local // te
    rl = local % te
    rx = tile_start[e] + rl
    valid = s < total
    first = valid & (rl == 0)
    extra = jnp.maximum(s - total, 0)
    j = jnp.where(valid, jw, extra % nj)
    r = jnp.where(valid, rx, used_tiles + extra // nj)
    i32 = lambda a: a.astype(jnp.int32)
    return i32(e), i32(jw), i32(rx), i32(j), i32(r), i32(first), i32(valid)


def _combine_kernel(dest_ref, x_ref, route_ref, ys_hbm, o_ref, buf, sem, *, tb):
    def start(t, c):
        for k in range(TOP_K):
            pltpu.make_async_copy(ys_hbm.at[pl.ds(dest_ref[0, 0, TOP_K * t + k], 1)], buf.at[k, pl.ds(t, 1)],
                                  sem).start(priority=k % 2)
        return c

    lax.fori_loop(0, tb, start, 0)

    def wait(t, c):
        pltpu.make_async_copy(ys_hbm.at[pl.ds(0, 1)], buf.at[0, pl.ds(0, 1)], sem).wait()
        return c

    lax.fori_loop(0, TOP_K * tb, wait, 0)
    route = route_ref[...]
    o_ref[...] = x_ref[...] + route[:, 4:5] * buf[0] + route[:, 5:6] * buf[1]


def moe_combine(x, route, ys, dest):
    t, d = x.shape
    tb = _tile(t, MOE_TOKEN_BLOCK)
    return pl.pallas_call(
        functools.partial(_combine_kernel, tb=tb),
        grid=(t // tb,),
        in_specs=[pl.BlockSpec((1, 1, TOP_K * tb), lambda i: (i, 0, 0), memory_space=pltpu.SMEM),
                  pl.BlockSpec((tb, d), lambda i: (i, 0)), pl.BlockSpec((tb, LANES), lambda i: (i, 0)),
                  pl.BlockSpec(memory_space=pl.ANY)],
        out_specs=pl.BlockSpec((tb, d), lambda i: (i, 0)),
        out_shape=jax.ShapeDtypeStruct((t, d), F32),
        scratch_shapes=[pltpu.VMEM((TOP_K, tb, d), F32), pltpu.SemaphoreType.DMA(())],
        compiler_params=_cparams(("arbitrary",)),
        name="moe_combine",
    )(dest.reshape(t // tb, 1, TOP_K * tb), x, route, ys)


def moe_block(x, norm_g, w_router, b_router, w_gate, w_up, w_down):
    t, d = x.shape
    ne, _, f = w_gate.shape
    tmg = MOE_ROW_TILE
    hp, route, cnt = rmsnorm_router(x, norm_g, w_router, b_router)
    counts = cnt[0, :ne].astype(jnp.int32)
    padded = (counts + tmg - 1) // tmg * tmg
    starts = jnp.cumsum(padded) - padded
    ids = route[:, 0:2].astype(jnp.int32)
    dest = starts[ids] + route[:, 2:4].astype(jnp.int32)
    n_rows = t * TOP_K + ne * tmg
    n_tiles = n_rows // tmg
    src = jnp.zeros((n_rows,), jnp.int32).at[dest.reshape(-1)].set(
        jnp.repeat(jnp.arange(t, dtype=jnp.int32), TOP_K), unique_indices=True)
    xs = moe_dispatch(hp, src, n_rows)
    tn_up, tn_down = _tile(f, 512), _tile(d, 1024)
    act = grouped_matmul(xs, [w_gate, w_up], _group_steps(padded, f // tn_up, n_tiles), _epi_swiglu, BF16,
                         tn_up, "moe_up")
    ys = grouped_matmul(act, [w_down], _group_steps(padded, d // tn_down, n_tiles), _epi_plain, F32,
                        tn_down, "moe_down")
    return moe_combine(x, route, ys, dest)


def _mm_kernel(*refs, n_w, n_e, nk, epilogue):
    a_ref = refs[0]
    w_refs = refs[1:1 + n_w]
    e_refs = refs[1 + n_w:1 + n_w + n_e]
    o_ref = refs[1 + n_w + n_e]
    acc_refs = refs[2 + n_w + n_e:]
    a = a_ref[...]
    parts = [_dot(a, w[...]) for w in w_refs]
    if nk == 1:
        o_ref[...] = epilogue(parts, [e[...] for e in e_refs]).astype(o_ref.dtype)
        return
    k = pl.program_id(2)

    @pl.when(k == 0)
    def _():
        for acc, p in zip(acc_refs, parts):
            acc[...] = p

    @pl.when(k > 0)
    def _():
        for acc, p in zip(acc_refs, parts):
            acc[...] += p

    @pl.when(k == nk - 1)
    def _():
        o_ref[...] = epilogue([acc[...] for acc in acc_refs], [e[...] for e in e_refs]).astype(o_ref.dtype)


def matmul(a, ws, epilogue, extras=(), extra_kinds=(), out_dtype=F32, tm=1024, tn=1024, tk=4096, w_prefix=(),
           name="matmul"):
    m, kdim = a.shape
    n = ws[0].shape[-1]
    tm, tn, tk = _tile(m, tm), _tile(n, tn), _tile(kdim, tk)
    nk = kdim // tk
    squeezed = (None,) * len(w_prefix)
    in_specs = [pl.BlockSpec((tm, tk), lambda i, j, k: (i, k))]
    in_specs += [pl.BlockSpec(squeezed + (tk, tn), lambda i, j, k: (*w_prefix, k, j)) for _ in ws]
    for kind in extra_kinds:
        if kind == "mn":
            in_specs.append(pl.BlockSpec((tm, tn), lambda i, j, k: (i, j)))
        elif kind == "m":
            in_specs.append(pl.BlockSpec((tm, LANES), lambda i, j, k: (i, 0)))
        else:
            in_specs.append(pl.BlockSpec((1, tn), lambda i, j, k: (0, j)))
    scratch = [pltpu.VMEM((tm, tn), F32) for _ in ws] if nk > 1 else []
    return pl.pallas_call(
        functools.partial(_mm_kernel, n_w=len(ws), n_e=len(extras), nk=nk, epilogue=epilogue),
        grid=(m // tm, n // tn, nk),
        in_specs=in_specs,
        out_specs=pl.BlockSpec((tm, tn), lambda i, j, k: (i, j)),
        out_shape=jax.ShapeDtypeStruct((m, n), out_dtype),
        scratch_shapes=scratch,
        compiler_params=_cparams(("parallel", "parallel", "arbitrary")),
        name=name,
    )(a, *ws, *extras)


def _epi_plain(parts, extras):
    return parts[0]


def _epi_residual(parts, extras):
    return extras[0] + parts[0]


def _epi_swiglu(parts, extras):
    g, u = parts
    return g * jax.nn.sigmoid(g) * u


def _epi_glu(parts, extras):
    y, b = extras
    return y * jax.nn.sigmoid(parts[0] + b)


def _merge_kernel(h_ref, wg_ref, b_ref, y_ref, wb_ref, o_ref, accm_ref, *accg, nk, nb):
    j = pl.program_id(2)
    k = pl.program_id(3)

    @pl.when((pl.program_id(0) == 0) & (pl.program_id(1) == 0) & (j == 0) & (k == 0))
    def _():
        accm_ref[...] = jnp.zeros(accm_ref.shape, F32)

    part = _dot(h_ref[...], wg_ref[0])

    def finish(pre):
        term = jax.nn.sigmoid(pre + b_ref[0]) * _dot(y_ref[0], wb_ref[0])
        total = jnp.where(j == 0, term, accm_ref[...] + term)
        accm_ref[...] = total
        o_ref[...] = total.astype(o_ref.dtype)

    if nk == 1:
        finish(part)
        return

    accg_ref, = accg

    @pl.when(k == 0)
    def _():
        accg_ref[...] = part

    @pl.when(k > 0)
    def _():
        accg_ref[...] += part

    @pl.when(k == nk - 1)
    def _():
        finish(accg_ref[...])


def gated_merge(h, w_gate, b_gate, ys, w_branch, layer, tm=1024, tn=512, tk=4096):
    t, d = h.shape
    nb, _, wdt = ys.shape
    tm, tn, tk = _tile(t, tm), _tile(d, tn), _tile(d, tk)
    nk = d // tk
    assert nb >= 2
    return pl.pallas_call(
        functools.partial(_merge_kernel, nk=nk, nb=nb),
        grid=(t // tm, d // tn, nb, nk),
        in_specs=[
            pl.BlockSpec((tm, tk), lambda i, n, j, k: (i, k)),
            pl.BlockSpec((None, 1, tk, tn), lambda i, n, j, k: (layer, j, k, n)),
            pl.BlockSpec((1, 1, tn), lambda i, n, j, k: (j, 0, n)),
            pl.BlockSpec((1, tm, wdt), lambda i, n, j, k: (j, i, 0)),
            pl.BlockSpec((None, 1, wdt, tn), lambda i, n, j, k: (layer, j, 0, n)),
        ],
        out_specs=pl.BlockSpec((tm, tn), lambda i, n, j, k: (i, n)),
        out_shape=jax.ShapeDtypeStruct((t, d), BF16),
        scratch_shapes=[pltpu.VMEM((tm, tn), F32)] * (1 if nk == 1 else 2),
        compiler_params=_cparams(("arbitrary", "arbitrary", "arbitrary", "arbitrary")),
        name="gated_merge",
    )(h, w_gate, b_gate, ys, w_branch)


def _softplus(x):
    return jnp.maximum(x, 0.0) + jnp.log1p(jnp.exp(-jnp.abs(x)))


def _ssd_kernel(xbc_ref, z_ref, dt_ref, dtt_ref, cw_ref, cb_ref, dtb_ref, dtbt_ref, alog_ref, alogt_ref,
                dskip_ref, ng_ref, expand_ref, ybuf_ref, o_ref, xe_ref, st_ref, *, n_heads, d_inner):
    del ybuf_ref
    q = SSD_CHUNK
    gn = SSD_N_GROUPS * SSD_D_STATE
    hpg = n_heads // SSD_N_GROUPS
    gw = hpg * SSD_HEAD_DIM
    c = pl.program_id(1)

    @pl.when(c == 0)
    def _():
        xe_ref[0:8, :] = jnp.zeros((8, xe_ref.shape[1]), F32)
        st_ref[...] = jnp.zeros(st_ref.shape, F32)

    xe_ref[8:8 + q, :] = xbc_ref[...]
    acc = cb_ref[...] + cw_ref[3:4, :] * xe_ref[8:8 + q, :]
    for kk in range(3):
        acc = acc + cw_ref[kk:kk + 1, :] * xe_ref[5 + kk:5 + kk + q, :]
    xe_ref[0:8, :] = xe_ref[q:q + 8, :]
    xc = acc * jax.nn.sigmoid(acc)
    xs = xc[:, :d_inner]
    bm = xc[:, d_inner:d_inner + gn].astype(BF16)
    cm = xc[:, d_inner + gn:].astype(BF16)

    dt = _softplus(dt_ref[...] + dtb_ref[...])
    da = dt * (-jnp.exp(alog_ref[...]))
    row = lax.broadcasted_iota(jnp.int32, (q, q), 0)
    col = lax.broadcasted_iota(jnp.int32, (q, q), 1)
    lower = row >= col
    tril = jnp.where(lower, 1.0, 0.0).astype(BF16)
    acs = _dot_exact_lhs01(tril, da)
    dtt = _softplus(dtt_ref[0] + dtbt_ref[...])
    dat = dtt * (-jnp.exp(alogt_ref[...]))
    triu = jnp.where(row <= col, 1.0, 0.0).astype(BF16)
    acst = _dot_exact_rhs01(dat, triu)

    expand = expand_ref[...]
    dt_x = _dot_exact_rhs01(dt, expand)
    acs_x = _dot_exact_rhs01(acs, expand)
    tot_x = acs_x[q - 1:q, :]
    xdt = xs * dt_x
    xdt_b = xdt.astype(BF16)
    xw_b = (xdt * jnp.exp(tot_x - acs_x)).astype(BF16)
    in_decay = jnp.exp(acs_x)
    chunk_decay = jnp.exp(tot_x)

    lane = lax.broadcasted_iota(jnp.int32, (1, gw), 1)
    ys = []
    for g in range(SSD_N_GROUPS):
        cg = cm[:, g * SSD_D_STATE:(g + 1) * SSD_D_STATE]
        bg = bm[:, g * SSD_D_STATE:(g + 1) * SSD_D_STATE]
        sl = slice(g * gw, (g + 1) * gw)
        gmat = lax.dot_general(cg, bg, (((1,), (1,)), ((), ())), preferred_element_type=F32)
        state = st_ref[g]
        y_g = _dot(cg, state.astype(BF16)) * in_decay[:, sl]
        xg = xdt_b[:, sl]
        for hh in range(hpg):
            hd = g * hpg + hh
            seg = acs[:, hd:hd + 1] - acst[hd:hd + 1, :]
            lmat = jnp.exp(jnp.where(lower, seg, -jnp.inf))
            mh = (gmat * lmat).astype(BF16)
            head_cols = (lane >= hh * SSD_HEAD_DIM) & (lane < (hh + 1) * SSD_HEAD_DIM)
            y_g = y_g + _dot(mh, jnp.where(head_cols, xg, jnp.zeros_like(xg)))
        upd = lax.dot_general(bg, xw_b[:, sl], (((0,), (0,)), ((), ())), preferred_element_type=F32)
        st_ref[g] = state * chunk_decay[:, sl] + upd
        ys.append(y_g)
    y = jnp.concatenate(ys, axis=-1) + xs * dskip_ref[...]
    zz = z_ref[...]
    y = y * (zz * jax.nn.sigmoid(zz))
    outs = []
    for g in range(SSD_N_GROUPS):
        yg = y[:, g * gw:(g + 1) * gw]
        ms = jnp.mean(yg * yg, axis=-1, keepdims=True)
        outs.append(yg * lax.rsqrt(ms + RMS_EPS))
    o_ref[0] = (jnp.concatenate(outs, axis=-1) * ng_ref[...]).astype(o_ref.dtype)


def ssd_mixer(proj, dt_t, cols, bsz, seq, conv_w, conv_b, dt_bias, a_log, d_skip, norm_g, ybuf, slot):
    n_heads = dt_bias.shape[0]
    d_inner = n_heads * SSD_HEAD_DIM
    conv_dim = conv_w.shape[1]
    q = SSD_CHUNK
    nc = seq // q
    pad = LANES - n_heads
    expand = jnp.repeat(jnp.eye(LANES, dtype=BF16)[:, :n_heads], SSD_HEAD_DIM, axis=1)
    dtb = jnp.pad(dt_bias, (0, pad)).reshape(1, LANES)
    alog = jnp.pad(a_log, (0, pad)).reshape(1, LANES)
    x_blk, z_blk, dt_blk = cols["xbc"] // conv_dim, cols["z"] // d_inner, cols["dt"] // LANES
    const = lambda b, c: (0, 0)
    return pl.pallas_call(
        functools.partial(_ssd_kernel, n_heads=n_heads, d_inner=d_inner),
        grid=(bsz, nc),
        in_specs=[
            pl.BlockSpec((q, conv_dim), lambda b, c: (b * nc + c, x_blk)),
            pl.BlockSpec((q, d_inner), lambda b, c: (b * nc + c, z_blk)),
            pl.BlockSpec((q, LANES), lambda b, c: (b * nc + c, dt_blk)),
            pl.BlockSpec((1, n_heads, q), lambda b, c: (b, 0, c)),
            pl.BlockSpec((4, conv_dim), const),
            pl.BlockSpec((1, conv_dim), const),
            pl.BlockSpec((1, LANES), const),
            pl.BlockSpec((n_heads, 1), const),
            pl.BlockSpec((1, LANES), const),
            pl.BlockSpec((n_heads, 1), const),
            pl.BlockSpec((1, d_inner), const),
            pl.BlockSpec((1, d_inner), const),
            pl.BlockSpec((LANES, d_inner), const),
            pl.BlockSpec(memory_space=pl.ANY),
        ],
        out_specs=pl.BlockSpec((1, q, d_inner), lambda b, c: (slot, b * nc + c, 0)),
        out_shape=jax.ShapeDtypeStruct(ybuf.shape, ybuf.dtype),
        scratch_shapes=[pltpu.VMEM((q + 8, conv_dim), F32),
                        pltpu.VMEM((SSD_N_GROUPS, SSD_D_STATE, d_inner // SSD_N_GROUPS), F32)],
        input_output_aliases={13: 0},
        compiler_params=_cparams(("parallel", "arbitrary")),
        name="ssd_mixer",
    )(proj, proj, proj, dt_t, conv_w, conv_b.reshape(1, conv_dim), dtb, dt_bias.reshape(n_heads, 1),
      alog, a_log.reshape(n_heads, 1), jnp.repeat(d_skip, SSD_HEAD_DIM).reshape(1, d_inner),
      norm_g.reshape(1, d_inner), expand, ybuf)


S5_ROWS = 8


def _s5_disc_kernel(are_ref, aim_ref, ldt_ref, brt_ref, bit_ref, pr_ref, pi_ref, bbr_ref, bbi_ref):
    dt = jnp.exp(ldt_ref[...])
    ar, ai = are_ref[...], aim_ref[...]
    mag = jnp.exp(ar * dt)
    lr, li = mag * jnp.cos(ai * dt), mag * jnp.sin(ai * dt)
    den = ar * ar + ai * ai
    fr = ((lr - 1.0) * ar + li * ai) / den
    fi = (li * ar - (lr - 1.0) * ai) / den
    pr, pi = lr, li
    for k in range(S5_ROWS):
        pr_ref[k] = pr
        pi_ref[k] = pi
        pr, pi = pr * lr - pi * li, pr * li + pi * lr
    br, bi = brt_ref[...], bit_ref[...]
    bbr_ref[...] = fr[:, None, :] * br - fi[:, None, :] * bi
    bbi_ref[...] = fr[:, None, :] * bi + fi[:, None, :] * br


def s5_discretise(a_re, a_im, log_dt, b_re, b_im):
    g, n = a_re.shape
    c = b_re.shape[2]
    brt, bit = jnp.swapaxes(b_re, 1, 2), jnp.swapaxes(b_im, 1, 2)
    return pl.pallas_call(
        _s5_disc_kernel,
        out_shape=[jax.ShapeDtypeStruct((S5_ROWS, g, n), F32), jax.ShapeDtypeStruct((S5_ROWS, g, n), F32),
                   jax.ShapeDtypeStruct((g, c, n), F32), jax.ShapeDtypeStruct((g, c, n), F32)],
        name="s5_discretise",
    )(a_re, a_im, log_dt.reshape(g, 1), brt, bit)


def _block_diag(blocks, per):
    g, r, c = blocks.shape
    nb = g // per
    eye = jnp.eye(per, dtype=blocks.dtype)
    out = blocks.reshape(nb, per, r, 1, c) * eye[None, :, None, :, None]
    return out.reshape(nb, per * r, per * c)


def _s5_kernel(u_ref, wr_ref, wi_ref, cr_ref, ci_ref, ctab_ref, d_ref, wglu_ref, bglu_ref, ybuf_ref, o_ref,
               xr_scr, xi_scr, carry_ref, y_scr, *, nblk, sw):
    del ybuf_ref
    tm = u_ref.shape[0]

    @pl.when(pl.program_id(1) == 0)
    def _():
        carry_ref[...] = jnp.zeros(carry_ref.shape, F32)

    for blk in range(nblk):
        cs = slice(blk * sw, (blk + 1) * sw)
        ls = slice(blk * LANES, (blk + 1) * LANES)
        u = u_ref[:, ls]
        ub = u.astype(BF16)
        xr_scr[...] = _dot(ub, wr_ref[blk])
        xi_scr[...] = _dot(ub, wi_ref[blk])

        def body(i, carry, cs=cs):
            c_re, c_im = carry
            r0 = pl.multiple_of(i * S5_ROWS, S5_ROWS)
            sr, si = xr_scr[pl.ds(r0, S5_ROWS), :], xi_scr[pl.ds(r0, S5_ROWS), :]
            for lvl, dist in enumerate((1, 2, 4)):
                lr, li = ctab_ref[2 * lvl, :, cs], ctab_ref[2 * lvl + 1, :, cs]
                tr, ti = pltpu.roll(sr, dist, axis=0), pltpu.roll(si, dist, axis=0)
                sr, si = sr + lr * tr - li * ti, si + lr * ti + li * tr
            pr, pi = ctab_ref[6, :, cs], ctab_ref[7, :, cs]
            sr, si = sr + pr * c_re - pi * c_im, si + pr * c_im + pi * c_re
            xr_scr[pl.ds(r0, S5_ROWS), :] = sr
            xi_scr[pl.ds(r0, S5_ROWS), :] = si
            return sr[S5_ROWS - 1:S5_ROWS, :], si[S5_ROWS - 1:S5_ROWS, :]

        c_re, c_im = lax.fori_loop(0, tm // S5_ROWS, body, (carry_ref[0:1, cs], carry_ref[1:2, cs]), unroll=2)
        carry_ref[0:1, cs] = c_re
        carry_ref[1:2, cs] = c_im
        y = _dot(xr_scr[...].astype(BF16), cr_ref[blk]) - _dot(xi_scr[...].astype(BF16), ci_ref[blk])
        y_scr[:, ls] = jax.nn.gelu(y + d_ref[:, ls] * u)
    y = y_scr[...]
    o_ref[0] = (y * jax.nn.sigmoid(_dot(y.astype(BF16), wglu_ref[...]) + bglu_ref[...])).astype(o_ref.dtype)


def s5_mixer(proj, u_col, bsz, seq, a_re, a_im, b_re, b_im, c_re, c_im, log_dt, d_skip, w_glu, b_glu,
             ybuf, slot):
    t = bsz * seq
    g, n = a_re.shape
    width = g * S5_GROUP
    per = S5_GROUPS_PER_BLOCK
    nblk = g // per
    sw = per * n
    ns = g * n
    pw_r, pw_i, bbr, bbi = s5_discretise(a_re, a_im, log_dt, b_re, b_im)
    w_r = _block_diag(bbr, per).astype(BF16)
    w_i = _block_diag(bbi, per).astype(BF16)
    c_r = _block_diag(jnp.swapaxes(c_re, 1, 2), per).astype(BF16)
    c_i = _block_diag(jnp.swapaxes(c_im, 1, 2), per).astype(BF16)
    pw_r, pw_i = pw_r.reshape(S5_ROWS, ns), pw_i.reshape(S5_ROWS, ns)
    sub = jnp.arange(S5_ROWS)[:, None]
    kinds = []
    for dist in (1, 2, 4):
        kinds += [jnp.where(sub >= dist, pw_r[dist - 1][None, :], 0.0),
                  jnp.where(sub >= dist, pw_i[dist - 1][None, :], 0.0)]
    ctab = jnp.stack(kinds + [pw_r, pw_i])
    tm = _tile(seq, 256)
    nt = seq // tm
    const2 = lambda b, c: (0, 0)
    const3 = lambda b, c: (0, 0, 0)
    return pl.pallas_call(
        functools.partial(_s5_kernel, nblk=nblk, sw=sw),
        grid=(bsz, nt),
        in_specs=[pl.BlockSpec((tm, width), lambda b, c: (b * nt + c, u_col // width)),
                  pl.BlockSpec((nblk, LANES, sw), const3), pl.BlockSpec((nblk, LANES, sw), const3),
                  pl.BlockSpec((nblk, sw, LANES), const3), pl.BlockSpec((nblk, sw, LANES), const3),
                  pl.BlockSpec((8, S5_ROWS, ns), const3),
                  pl.BlockSpec((1, width), const2), pl.BlockSpec((width, width), const2),
                  pl.BlockSpec((1, width), const2), pl.BlockSpec(memory_space=pl.ANY)],
        out_specs=pl.BlockSpec((1, tm, width), lambda b, c: (slot, b * nt + c, 0)),
        out_shape=jax.ShapeDtypeStruct(ybuf.shape, ybuf.dtype),
        scratch_shapes=[pltpu.VMEM((tm, sw), F32), pltpu.VMEM((tm, sw), F32), pltpu.VMEM((8, ns), F32),
                        pltpu.VMEM((tm, width), F32)],
        input_output_aliases={9: 0},
        compiler_params=_cparams(("parallel", "arbitrary")),
        name="s5_mixer",
    )(proj, w_r, w_i, c_r, c_i, ctab, d_skip.reshape(1, width), w_glu.astype(BF16), b_glu.reshape(1, width),
      ybuf)


def _flash_kernel(*refs, tq, n_maps, has_bias, finalize):
    if has_bias:
        q_ref, k_ref, v_ref, bias_ref, far_ref, *rest = refs
    else:
        q_ref, k_ref, v_ref, *rest = refs
    fin_refs = rest[:-2]
    o_ref = rest[-1]
    h = pl.program_id(1)
    qi = pl.program_id(2)
    rows = n_maps * tq
    dvo = v_ref.shape[1]
    qv = q_ref[...].reshape(rows, q_ref.shape[-1])

    def tiled(b):
        return jnp.concatenate([b] * n_maps, axis=0) if n_maps > 1 else b

    def block(kb, carry, add, shift):
        m, l, acc = carry
        off = pl.multiple_of(kb * tq, tq)
        s = lax.dot_general(qv, k_ref[pl.ds(off, tq), :], (((1,), (1,)), ((), ())), preferred_element_type=F32)
        sb = s.astype(BF16)
        if add is not None:
            sb = sb + add
        mx = jnp.max(sb, axis=-1, keepdims=True).astype(F32)
        m_hi = jnp.maximum(m, mx if shift is None else mx + shift)
        ref_b = (m_hi if shift is None else m_hi - shift).astype(BF16)
        m_new = ref_b.astype(F32) if shift is None else ref_b.astype(F32) + shift
        alpha = jnp.exp2(m - m_new)
        p = jnp.exp2(sb - ref_b)
        l_new = alpha * l + jnp.sum(p.astype(F32), axis=-1, keepdims=True)
        return m_new, l_new, alpha * acc + _dot(p, v_ref[pl.ds(off, tq), :])

    carry = (jnp.full((rows, 1), -jnp.inf, F32), jnp.zeros((rows, 1), F32), jnp.zeros((rows, dvo), F32))
    if has_bias:
        far = far_ref[h]
        n_far = jnp.maximum(qi - 1, 0)
        carry = lax.fori_loop(0, n_far, lambda kb, c: block(kb, c, None, far), carry)
        carry = lax.fori_loop(n_far, qi, lambda kb, c: block(kb, c, tiled(bias_ref[0, 1]), None), carry)
    else:
        carry = lax.fori_loop(0, qi, lambda kb, c: block(kb, c, None, None), carry)
    row = lax.broadcasted_iota(jnp.int32, (tq, tq), 0)
    col = lax.broadcasted_iota(jnp.int32, (tq, tq), 1)
    diag = jnp.where(row >= col, bias_ref[0, 0].astype(F32) if has_bias else 0.0, -jnp.inf).astype(BF16)
    m, l, acc = block(qi, carry, tiled(diag), None)
    o_ref[0] = finalize(acc / l, [r[...] for r in fin_refs]).astype(o_ref.dtype)


def flash_attention(q, k, v, bsz, seq, n_heads, dq, ybuf, slot, n_maps=1, bias=None, far=None, fin=(),
                    finalize=None, tq=512):
    dv = v.shape[1] // n_heads
    tq = _tile(seq, tq)
    nq = seq // tq
    has_bias = bias is not None
    if finalize is None:
        finalize = lambda o, extras: o
    in_specs = [pl.BlockSpec((n_maps, tq, dq), lambda b, h, i: (0, b * nq + i, h)),
                pl.BlockSpec((seq, dq), lambda b, h, i: (b, h)),
                pl.BlockSpec((seq, dv), lambda b, h, i: (b, h))]
    args = [q, k, v]
    if has_bias:
        in_specs += [pl.BlockSpec((1, 2, tq, tq), lambda b, h, i: (h, 0, 0, 0)),
                     pl.BlockSpec(memory_space=pltpu.SMEM)]
        args += [bias, far]
    for f in fin:
        in_specs.append(pl.BlockSpec(f.shape, lambda b, h, i: (0, 0)))
        args.append(f)
    in_specs.append(pl.BlockSpec(memory_space=pl.ANY))
    args.append(ybuf)
    return pl.pallas_call(
        functools.partial(_flash_kernel, tq=tq, n_maps=n_maps, has_bias=has_bias, finalize=finalize),
        grid=(bsz, n_heads, nq),
        in_specs=in_specs,
        out_specs=pl.BlockSpec((1, tq, dv), lambda b, h, i: (slot, b * nq + i, h)),
        out_shape=jax.ShapeDtypeStruct(ybuf.shape, ybuf.dtype),
        input_output_aliases={len(args) - 1: 0},
        compiler_params=_cparams(("parallel", "parallel", "arbitrary")),
        name="flash_bias" if has_bias else "flash",
    )(*args)


def _mla_proj_kernel(cq_ref, ckv_ref, kpe_ref, cos_ref, sin_ref, qng_ref, kvng_ref, wq_ref, wkv_ref,
                     qg_ref, kg_ref, q_ref, k_ref, v_ref, *, n_heads, scale):
    half = LANES // 2

    def norm(x, g):
        return (x * lax.rsqrt(jnp.mean(x * x, axis=-1, keepdims=True) + RMS_EPS) * g).astype(BF16)

    cq = norm(cq_ref[...], qng_ref[...])
    ckv = norm(ckv_ref[...], kvng_ref[...])
    cos, sin = cos_ref[...], sin_ref[...]
    kpe = kpe_ref[...]
    kpe_ss = jnp.sum(jnp.where(lax.broadcasted_iota(jnp.int32, kpe.shape, 1) < half, kpe * kpe, 0.0),
                     axis=-1, keepdims=True)
    n_qk = float(MLA_NOPE + MLA_ROPE)
    qg, kg = qg_ref[...], kg_ref[...]

    def rotary(hi, gains, inv):
        return inv * (hi * (gains[1:2] * cos) + pltpu.roll(hi, half, axis=1) * (gains[2:3] * sin))

    k_rot_raw = kpe * (kg[1:2] * cos) + pltpu.roll(kpe, half, axis=1) * (kg[2:3] * sin)
    for h in range(n_heads):
        xq = _dot(cq, wq_ref[:, h * MLA_PAD:(h + 1) * MLA_PAD])
        lo, hi = xq[:, :LANES], xq[:, LANES:]
        ss = jnp.sum(lo * lo, axis=-1, keepdims=True) + 0.5 * jnp.sum(hi * hi, axis=-1, keepdims=True)
        inv = lax.rsqrt(ss / n_qk + RMS_EPS) * scale
        q_ref[:, h * MLA_PAD:h * MLA_PAD + LANES] = (lo * inv * qg[0:1]).astype(BF16)
        q_ref[:, h * MLA_PAD + LANES:(h + 1) * MLA_PAD] = rotary(hi, qg, inv).astype(BF16)
        xkv = _dot(ckv, wkv_ref[:, h * MLA_PAD:(h + 1) * MLA_PAD])
        kn = xkv[:, :LANES]
        inv_k = lax.rsqrt((jnp.sum(kn * kn, axis=-1, keepdims=True) + kpe_ss) / n_qk + RMS_EPS)
        k_ref[:, h * MLA_PAD:h * MLA_PAD + LANES] = (kn * inv_k * kg[0:1]).astype(BF16)
        k_ref[:, h * MLA_PAD + LANES:(h + 1) * MLA_PAD] = (k_rot_raw * inv_k).astype(BF16)
        v_ref[:, h * LANES:(h + 1) * LANES] = xkv[:, LANES:].astype(BF16)


def _rope_gain_rows(gain):
    rot = gain[MLA_NOPE:]
    r = MLA_ROPE // 2
    swapped = jnp.concatenate([rot[r:], rot[:r]])
    z = jnp.zeros((LANES - MLA_ROPE,), F32)
    return jnp.stack([gain[:MLA_NOPE], jnp.concatenate([rot, z]), jnp.concatenate([swapped, z])])


def mla_mixer(proj, cols, bsz, seq, q_norm_g, w_q_up, kv_norm_g, w_kv_up, q_gain, k_gain, ybuf, slot):
    t = bsz * seq
    hds = MLA_N_HEADS
    q_rank, kv_rank = w_q_up.shape[0], w_kv_up.shape[0]
    r = MLA_ROPE // 2
    wq = w_q_up.reshape(q_rank, hds, MLA_NOPE + MLA_ROPE)
    wq = jnp.concatenate([wq, wq[:, :, MLA_NOPE + r:], wq[:, :, MLA_NOPE:MLA_NOPE + r]], axis=-1)
    wq = wq.reshape(q_rank, hds * MLA_PAD).astype(BF16)
    wkv = w_kv_up.astype(BF16)
    inv = 1.0 / (ROPE_THETA ** (jnp.arange(0, MLA_ROPE, 2, dtype=F32) / MLA_ROPE))
    ang = jnp.arange(seq, dtype=F32)[:, None] * inv[None, :]
    zer = jnp.zeros((seq, LANES - MLA_ROPE), F32)
    cos_t = jnp.concatenate([jnp.cos(ang), jnp.cos(ang), zer], axis=-1)
    sin_t = jnp.concatenate([-jnp.sin(ang), jnp.sin(ang), zer], axis=-1)
    tm = _tile(seq, 512)
    npos = seq // tm
    const = lambda i: (0, 0)
    q, k, v = pl.pallas_call(
        functools.partial(_mla_proj_kernel, n_heads=hds, scale=LOG2E * float(MLA_NOPE + MLA_ROPE) ** -0.5),
        grid=(t // tm,),
        in_specs=[pl.BlockSpec((tm, q_rank), lambda i: (i, cols["cq"] // q_rank)),
                  pl.BlockSpec((tm, kv_rank), lambda i: (i, cols["ckv"] // kv_rank)),
                  pl.BlockSpec((tm, LANES), lambda i: (i, cols["kpe"] // LANES)),
                  pl.BlockSpec((tm, LANES), lambda i: (i % npos, 0)),
                  pl.BlockSpec((tm, LANES), lambda i: (i % npos, 0)),
                  pl.BlockSpec((1, q_rank), const), pl.BlockSpec((1, kv_rank), const),
                  pl.BlockSpec((q_rank, hds * MLA_PAD), const), pl.BlockSpec((kv_rank, hds * MLA_PAD), const),
                  pl.BlockSpec((3, LANES), const), pl.BlockSpec((3, LANES), const)],
        out_specs=[pl.BlockSpec((tm, hds * MLA_PAD), lambda i: (i, 0)),
                   pl.BlockSpec((tm, hds * MLA_PAD), lambda i: (i, 0)),
                   pl.BlockSpec((tm, hds * LANES), lambda i: (i, 0))],
        out_shape=[jax.ShapeDtypeStruct((t, hds * MLA_PAD), BF16), jax.ShapeDtypeStruct((t, hds * MLA_PAD), BF16),
                   jax.ShapeDtypeStruct((t, hds * LANES), BF16)],
        compiler_params=_cparams(("parallel",)),
        name="mla_proj",
    )(proj, proj, proj, cos_t, sin_t, q_norm_g.reshape(1, q_rank), kv_norm_g.reshape(1, kv_rank), wq, wkv,
      _rope_gain_rows(q_gain), _rope_gain_rows(k_gain))
    return flash_attention(q.reshape(1, t, hds * MLA_PAD), k, v, bsz, seq, hds, MLA_PAD, ybuf, slot, tq=1024)


def _diff_prep_kernel(q_ref, k_ref, v_ref, qg_ref, kg_ref, seg_ref, qo_ref, ko_ref, vo_ref, *, hd, scale):
    seg = seg_ref[...]
    lane = lax.broadcasted_iota(jnp.int32, (1, LANES), 1)
    n_col = q_ref.shape[1] // LANES
    for c in range(n_col):
        sl = slice(c * LANES, (c + 1) * LANES)
        xq, xk = q_ref[:, sl], k_ref[:, sl]
        ssq = _dot_exact_rhs01(xq * xq, seg)
        ssk = _dot_exact_rhs01(xk * xk, seg)
        qn = xq * lax.rsqrt(ssq / hd + RMS_EPS) * (qg_ref[...] * scale)
        kn = xk * lax.rsqrt(ssk / hd + RMS_EPS) * kg_ref[...]
        qo_ref[0, :, sl] = jnp.where(lane < hd, qn, 0.0).astype(BF16)
        qo_ref[1, :, sl] = jnp.where(lane >= hd, qn, 0.0).astype(BF16)
        ko_ref[:, sl] = kn.astype(BF16)
    vo_ref[...] = v_ref[...].astype(BF16)


def _t5_bucket_np(n):
    max_exact = REL_BUCKETS // 2
    nf = np.maximum(n, 1).astype(np.float32)
    large = max_exact + (np.log(nf / np.float32(max_exact)) / np.float32(math.log(REL_MAX_DIST / max_exact))
                         * np.float32(REL_BUCKETS - max_exact)).astype(np.int32)
    large = np.minimum(large, REL_BUCKETS - 1)
    return np.where(n < max_exact, n, large).astype(np.int32)


def _bias_tiles_kernel(bkt_ref, rb_ref, o_ref):
    h = pl.program_id(0)
    for d in range(2):
        bkt = bkt_ref[d]
        tile = jnp.zeros(bkt.shape, F32)
        for b in range(REL_BUCKETS):
            tile = jnp.where(bkt == b, rb_ref[b, h] * LOG2E, tile)
        o_ref[0, d] = tile.astype(o_ref.dtype)


def _diff_finalize(lambda_init, tq):
    def fin(o, extras):
        lq1, lk1, lq2, lk2, sub_g = extras
        lam = (jnp.exp(jnp.sum(lq1 * lk1, axis=-1, keepdims=True))
               - jnp.exp(jnp.sum(lq2 * lk2, axis=-1, keepdims=True)) + lambda_init)
        d = o[:tq] - lam * o[tq:]
        ms = jnp.mean(d * d, axis=-1, keepdims=True)
        return d * lax.rsqrt(ms + RMS_EPS) * (sub_g * (1.0 - lambda_init))
    return fin


def diff_mixer(proj, cols, bsz, seq, q_gain, k_gain, lq1, lk1, lq2, lk2, subln_g, rel_bias, lambda_init,
               ybuf, slot):
    t = bsz * seq
    hds = DIFF_N_HEADS
    hd = q_gain.shape[0]
    width = hds * 2 * hd
    tm = _tile(t, 512)
    seg = (np.arange(LANES)[:, None] // hd == np.arange(LANES)[None, :] // hd)
    seg = jnp.asarray(seg, BF16)
    gq, gk = jnp.tile(q_gain, LANES // hd).reshape(1, LANES), jnp.tile(k_gain, LANES // hd).reshape(1, LANES)
    const = lambda i: (0, 0)
    qd, kd, vd = pl.pallas_call(
        functools.partial(_diff_prep_kernel, hd=hd, scale=LOG2E * float(hd) ** -0.5),
        grid=(t // tm,),
        in_specs=[pl.BlockSpec((tm, width), lambda i: (i, cols["dq"] // width)),
                  pl.BlockSpec((tm, width), lambda i: (i, cols["dk"] // width)),
                  pl.BlockSpec((tm, width), lambda i: (i, cols["dv"] // width)),
                  pl.BlockSpec((1, LANES), const), pl.BlockSpec((1, LANES), const),
                  pl.BlockSpec((LANES, LANES), const)],
        out_specs=[pl.BlockSpec((2, tm, width), lambda i: (0, i, 0)),
                   pl.BlockSpec((tm, width), lambda i: (i, 0)), pl.BlockSpec((tm, width), lambda i: (i, 0))],
        out_shape=[jax.ShapeDtypeStruct((2, t, width), BF16), jax.ShapeDtypeStruct((t, width), BF16),
                   jax.ShapeDtypeStruct((t, width), BF16)],
        compiler_params=_cparams(("parallel",)),
        name="diff_prep",
    )(proj, proj, proj, gq, gk, seg)

    tq = _tile(seq, 512)
    assert tq + 1 > REL_MAX_DIST, "blocks two or more before the diagonal must all be at the far distance"
    idx = np.arange(tq)
    dist0 = np.maximum(idx[:, None] - idx[None, :], 0)
    dist1 = idx[:, None] - idx[None, :] + tq
    bkt = jnp.asarray(np.stack([_t5_bucket_np(dist0), _t5_bucket_np(dist1)]))
    bias = pl.pallas_call(
        _bias_tiles_kernel,
        grid=(hds,),
        in_specs=[pl.BlockSpec((2, tq, tq), lambda h: (0, 0, 0)), pl.BlockSpec(memory_space=pltpu.SMEM)],
        out_specs=pl.BlockSpec((1, 2, tq, tq), lambda h: (h, 0, 0, 0)),
        out_shape=jax.ShapeDtypeStruct((hds, 2, tq, tq), BF16),
        compiler_params=_cparams(("parallel",)),
        name="t5_bias_tiles",
    )(bkt, rel_bias)
    far = rel_bias[REL_BUCKETS - 1] * LOG2E
    fin = tuple(a.reshape(1, -1) for a in (lq1, lk1, lq2, lk2, subln_g))
    return flash_attention(qd, kd, vd, bsz, seq, hds, 2 * hd, ybuf, slot, n_maps=2, bias=bias, far=far, fin=fin,
                           finalize=_diff_finalize(lambda_init, tq), tq=tq)


def _pack_w_in(w_in_l, widths):
    names = ("z", "xbc", "dt", "u", "cq", "ckv", "kpe", "dq", "dk", "dv")
    starts = dict(zip(names, np.cumsum([0] + list(widths))[:-1]))
    width = dict(zip(names, widths))
    nl, d, n_in = w_in_l.shape
    kpe_w, dt_w, r = width["kpe"], width["dt"], width["kpe"] // 2
    order = ("xbc", "z", "u", "dq", "dk", "dv", "ckv", "cq", "kpe", "dt")
    cols, off = {}, 0
    for name in order:
        wd = LANES if name in ("kpe", "dt") else width[name]
        assert off % wd == 0, (name, off, wd)
        cols[name] = off
        off += wd
    n_out = off

    def pack_kernel(w_ref, o_ref):
        w = w_ref[0]
        rows = w.shape[0]
        for name in order:
            s0, c0 = int(starts[name]), cols[name]
            if name == "kpe":
                parts = [w[:, s0:s0 + kpe_w], w[:, s0 + r:s0 + kpe_w], w[:, s0:s0 + r]]
                if LANES > 2 * kpe_w:
                    parts.append(jnp.zeros((rows, LANES - 2 * kpe_w), F32))
                blk = jnp.concatenate(parts, axis=-1)
            elif name == "dt":
                blk = jnp.concatenate([w[:, s0:s0 + dt_w], jnp.zeros((rows, LANES - dt_w), F32)], axis=-1)
            else:
                blk = w[:, s0:s0 + width[name]]
            o_ref[0, :, c0:c0 + blk.shape[1]] = blk.astype(BF16)

    tk = _tile(d, 256)
    packed = pl.pallas_call(
        pack_kernel,
        grid=(nl, d // tk),
        in_specs=[pl.BlockSpec((1, tk, n_in), lambda l, k: (l, k, 0))],
        out_specs=pl.BlockSpec((1, tk, n_out), lambda l, k: (l, k, 0)),
        out_shape=jax.ShapeDtypeStruct((nl, d, n_out), BF16),
        compiler_params=_cparams(("parallel", "parallel")),
        name="pack_w_in",
    )(w_in_l)
    return packed, cols


def kernel(x, norm_mix_g, w_in, ssd_conv_w, ssd_conv_b, ssd_dt_bias, ssd_a_log, ssd_d, ssd_norm_g, s5_a_re, s5_a_im, s5_b_re, s5_b_im, s5_c_re, s5_c_im, s5_log_dt, s5_d, s5_w_glu, s5_b_glu, mla_q_norm_g, mla_w_q_up, mla_kv_norm_g, mla_w_kv_up, mla_q_gain, mla_k_gain, diff_q_gain, diff_k_gain, diff_lq1, diff_lk1, diff_lq2, diff_lk2, diff_subln_g, rel_bias, w_gate, b_gate, w_branch, w_out, norm_ffn_g, ffn_w_gate, ffn_w_up, ffn_w_down, moe_w_router, moe_b_router, moe_w_gate, moe_w_up, moe_w_down):
    bsz, seq, d = x.shape
    t = bsz * seq
    depth = w_in.shape[0]
    mix = d // 4
    n_ssd_heads = ssd_dt_bias.shape[1]
    widths = (mix, ssd_conv_w.shape[2], n_ssd_heads, mix, mla_w_q_up.shape[1], mla_w_kv_up.shape[1], MLA_ROPE,
              mix, mix, mix)
    xf = x.reshape(t, d)
    w_in_b, cols = _pack_w_in(w_in, widths)
    w_gate_b, w_branch_b, w_out_b = w_gate.astype(BF16), w_branch.astype(BF16), w_out.astype(BF16)
    ffn_gate_b, ffn_up_b, ffn_down_b = ffn_w_gate.astype(BF16), ffn_w_up.astype(BF16), ffn_w_down.astype(BF16)
    ys = jnp.zeros((N_BRANCH, t, mix), BF16)
    for i in range(depth):
        h = rmsnorm(xf, norm_mix_g[i])
        proj = matmul(h, [w_in_b], _epi_plain, out_dtype=F32, tn=512, w_prefix=(i,), name="in_proj")
        dt_t = jnp.swapaxes(proj[:, cols["dt"]:cols["dt"] + n_ssd_heads].reshape(bsz, seq, n_ssd_heads), 1, 2)
        ys = ssd_mixer(proj, dt_t, cols, bsz, seq, ssd_conv_w[i], ssd_conv_b[i], ssd_dt_bias[i], ssd_a_log[i],
                       ssd_d[i], ssd_norm_g[i], ys, 0)
        ys = s5_mixer(proj, cols["u"], bsz, seq, s5_a_re[i], s5_a_im[i], s5_b_re[i], s5_b_im[i], s5_c_re[i],
                      s5_c_im[i], s5_log_dt[i], s5_d[i], s5_w_glu[i], s5_b_glu[i], ys, 1)
        ys = mla_mixer(proj, cols, bsz, seq, mla_q_norm_g[i], mla_w_q_up[i], mla_kv_norm_g[i], mla_w_kv_up[i],
                       mla_q_gain[i], mla_k_gain[i], ys, 2)
        lambda_init = 0.8 - 0.6 * math.exp(-0.3 * i)
        ys = diff_mixer(proj, cols, bsz, seq, diff_q_gain[i], diff_k_gain[i], diff_lq1[i], diff_lk1[i],
                        diff_lq2[i], diff_lk2[i], diff_subln_g[i], rel_bias, lambda_init, ys, 3)
        merged = gated_merge(h, w_gate_b, b_gate[i][:, None, :], ys, w_branch_b, i)
        xf = matmul(merged, [w_out_b], _epi_residual, extras=(xf,), extra_kinds=("mn",), tk=2048, w_prefix=(i,),
                    name="out_proj")
        e = i // 2
        if i % 2 == 0:
            h2 = rmsnorm(xf, norm_ffn_g[i])
            act = matmul(h2, [ffn_gate_b, ffn_up_b], _epi_swiglu, out_dtype=BF16, tn=512, w_prefix=(e,),
                         name="ffn_up")
            xf = matmul(act, [ffn_down_b], _epi_residual, extras=(xf,), extra_kinds=("mn",), tk=3584,
                        w_prefix=(e,), name="ffn_down")
        else:
            xf = moe_block(xf, norm_ffn_g[i], moe_w_router[e], moe_b_router[e], moe_w_gate[e], moe_w_up[e],
                           moe_w_down[e])
    return xf.reshape(bsz, seq, d)
```

```python
import functools
import math

import numpy as np
import jax
import jax.numpy as jnp
from jax import lax
from jax.experimental import pallas as pl
from jax.experimental.pallas import tpu as pltpu

F32 = jnp.float32
BF16 = jnp.bfloat16

V7X_VMEM_BYTES = 64 * 1024 * 1024
VMEM_LIMIT = V7X_VMEM_BYTES - 8 * 1024 * 1024
LANES = 128

RMS_EPS = 1e-6
SSD_HEAD_DIM = 64
SSD_N_GROUPS = 4
SSD_D_STATE = 128
SSD_CHUNK = 128
S5_GROUP = 16
S5_STATE = 64
S5_GROUPS_PER_BLOCK = 8
MLA_N_HEADS = 8
MLA_NOPE = 128
MLA_ROPE = 64
MLA_PAD = 256
ROPE_THETA = 10000.0
DIFF_N_HEADS = 8
REL_BUCKETS = 32
REL_MAX_DIST = 128
TOP_K = 2
N_BRANCH = 4
LOG2E = math.log2(math.e)


def _cparams(sem):
    return pltpu.CompilerParams(dimension_semantics=sem, vmem_limit_bytes=VMEM_LIMIT)


def _tile(n, pref):
    t = min(n, pref)
    while n % t:
        t //= 2
    return t


def _split3(a):
    hi = a.astype(BF16)
    r1 = a - hi.astype(F32)
    mid = r1.astype(BF16)
    lo = (r1 - mid.astype(F32)).astype(BF16)
    return hi, mid, lo


def _dot(a, b):
    return jnp.dot(a, b, preferred_element_type=F32)


def _dot_exact_rhs01(a, sel):
    hi, mid, lo = _split3(a)
    return _dot(hi, sel) + _dot(mid, sel) + _dot(lo, sel)


def _dot_exact_lhs01(sel, b):
    hi, mid, lo = _split3(b)
    return _dot(sel, hi) + _dot(sel, mid) + _dot(sel, lo)


def _rmsnorm_kernel(x_ref, g_ref, o_ref):
    x = x_ref[...]
    ms = jnp.mean(x * x, axis=-1, keepdims=True)
    o_ref[...] = (x * lax.rsqrt(ms + RMS_EPS) * g_ref[...]).astype(o_ref.dtype)


def rmsnorm(x, g, tm=256):
    t, d = x.shape
    tm = _tile(t, tm)
    return pl.pallas_call(
        _rmsnorm_kernel,
        grid=(t // tm,),
        in_specs=[pl.BlockSpec((tm, d), lambda i: (i, 0)), pl.BlockSpec((1, d), lambda i: (0, 0))],
        out_specs=pl.BlockSpec((tm, d), lambda i: (i, 0)),
        out_shape=jax.ShapeDtypeStruct((t, d), BF16),
        compiler_params=_cparams(("parallel",)),
        name="rmsnorm",
    )(x, g.reshape(1, d))


def _rmsnorm_router_kernel(x_ref, g_ref, wr_ref, br_ref, hp_ref, route_ref, cnt_ref, run_ref, *, n_experts):
    @pl.when(pl.program_id(0) == 0)
    def _():
        run_ref[...] = jnp.zeros(run_ref.shape, F32)

    x = x_ref[...]
    tm, d = x.shape
    ms = jnp.mean(x * x, axis=-1, keepdims=True)
    h = x * lax.rsqrt(ms + RMS_EPS) * g_ref[...]
    bits = lax.bitcast_convert_type(h.astype(BF16).astype(F32), jnp.uint32)
    hp_ref[...] = (bits[:, :d // 2] >> 16) | (bits[:, d // 2:] & jnp.uint32(0xFFFF0000))
    logits = jnp.dot(h, wr_ref[...], preferred_element_type=F32, precision=lax.Precision.HIGHEST) + br_ref[...]
    lane = lax.broadcasted_iota(jnp.int32, logits.shape, 1)
    neg = jnp.float32(-jnp.inf)
    logits = jnp.where(lane < n_experts, logits, neg)
    m1 = jnp.max(logits, axis=-1, keepdims=True)
    i1 = jnp.min(jnp.where(logits == m1, lane, LANES), axis=-1, keepdims=True)
    rest = jnp.where(lane == i1, neg, logits)
    m2 = jnp.max(rest, axis=-1, keepdims=True)
    i2 = jnp.min(jnp.where(rest == m2, lane, LANES), axis=-1, keepdims=True)
    e2 = jnp.exp(m2 - m1)
    w1 = 1.0 / (1.0 + e2)
    w2 = e2 / (1.0 + e2)
    sel = jnp.where(lane == i1, 1.0, 0.0) + jnp.where(lane == i2, 1.0, 0.0)
    row = lax.broadcasted_iota(jnp.int32, (tm, tm), 0)
    col = lax.broadcasted_iota(jnp.int32, (tm, tm), 1)
    before = jnp.where(col < row, 1.0, 0.0).astype(BF16)
    excl = _dot(before, sel.astype(BF16)) + run_ref[0:1, :]
    run_ref[...] = run_ref[...] + jnp.sum(sel, axis=0, keepdims=True)
    cnt_ref[...] = run_ref[...]
    r1 = jnp.sum(jnp.where(lane == i1, excl, 0.0), axis=-1, keepdims=True)
    r2 = jnp.sum(jnp.where(lane == i2, excl, 0.0), axis=-1, keepdims=True)
    route = jnp.zeros(logits.shape, F32)
    for pos, val in enumerate((i1.astype(F32), i2.astype(F32), r1, r2, w1, w2)):
        route = jnp.where(lane == pos, val, route)
    route_ref[...] = route


def rmsnorm_router(x, g, w_router, b_router, tm=256):
    t, d = x.shape
    ne = w_router.shape[1]
    tm = _tile(t, tm)
    wr = jnp.zeros((d, LANES), F32).at[:, :ne].set(w_router)
    br = jnp.zeros((1, LANES), F32).at[0, :ne].set(b_router)
    return pl.pallas_call(
        functools.partial(_rmsnorm_router_kernel, n_experts=ne),
        grid=(t // tm,),
        in_specs=[pl.BlockSpec((tm, d), lambda i: (i, 0)), pl.BlockSpec((1, d), lambda i: (0, 0)),
                  pl.BlockSpec((d, LANES), lambda i: (0, 0)), pl.BlockSpec((1, LANES), lambda i: (0, 0))],
        out_specs=[pl.BlockSpec((tm, d // 2), lambda i: (i, 0)), pl.BlockSpec((tm, LANES), lambda i: (i, 0)),
                   pl.BlockSpec((8, LANES), lambda i: (0, 0))],
        out_shape=[jax.ShapeDtypeStruct((t, d // 2), jnp.uint32), jax.ShapeDtypeStruct((t, LANES), F32),
                   jax.ShapeDtypeStruct((8, LANES), F32)],
        scratch_shapes=[pltpu.VMEM((8, LANES), F32)],
        compiler_params=_cparams(("arbitrary",)),
        name="rmsnorm_router",
    )(x, g.reshape(1, d), wr, br)


MOE_ROW_TILE = 512
MOE_TOKEN_BLOCK = 256


def _unpack_halves(xp):
    lo = lax.bitcast_convert_type(xp << 16, F32).astype(BF16)
    hi = lax.bitcast_convert_type(xp & jnp.uint32(0xFFFF0000), F32).astype(BF16)
    return lo, hi


def _dispatch_kernel(src_ref, h_hbm, o_ref, buf, sem, *, tmg):
    def start(r2, c):
        for p in range(2):
            r = 2 * r2 + p
            pltpu.make_async_copy(h_hbm.at[pl.ds(src_ref[0, 0, r], 1)], buf.at[pl.ds(r, 1)], sem).start(
                priority=p)
        return c

    lax.fori_loop(0, tmg // 2, start, 0)

    def wait(r, c):
        pltpu.make_async_copy(h_hbm.at[pl.ds(0, 1)], buf.at[pl.ds(0, 1)], sem).wait()
        return c

    lax.fori_loop(0, tmg, wait, 0)
    lo, hi = _unpack_halves(buf[...])
    half = lo.shape[1]
    o_ref[:, :half] = lo
    o_ref[:, half:] = hi


def moe_dispatch(hp, src, n_rows):
    dw = hp.shape[1]
    tmg = MOE_ROW_TILE
    return pl.pallas_call(
        functools.partial(_dispatch_kernel, tmg=tmg),
        grid=(n_rows // tmg,),
        in_specs=[pl.BlockSpec((1, 1, tmg), lambda i: (i, 0, 0), memory_space=pltpu.SMEM),
                  pl.BlockSpec(memory_space=pl.ANY)],
        out_specs=pl.BlockSpec((tmg, 2 * dw), lambda i: (i, 0)),
        out_shape=jax.ShapeDtypeStruct((n_rows, 2 * dw), BF16),
        scratch_shapes=[pltpu.VMEM((tmg, dw), hp.dtype), pltpu.SemaphoreType.DMA(())],
        compiler_params=_cparams(("arbitrary",)),
        name="moe_dispatch",
    )(src.reshape(n_rows // tmg, 1, tmg), hp)


def _grouped_kernel(e_tab, jw_tab, rx_tab, j_tab, r_tab, first_tab, valid_tab, x_ref, *rest, n_w, epilogue):
    del e_tab, jw_tab, rx_tab, j_tab, r_tab
    w_refs, o_ref, wb_refs = rest[:n_w], rest[n_w], rest[n_w + 1:]
    s = pl.program_id(0)

    @pl.when(first_tab[s] == 1)
    def _():
        for w, wb in zip(w_refs, wb_refs):
            wb[...] = w[0].astype(BF16)

    @pl.when(valid_tab[s] == 1)
    def _():
        x = x_ref[...]
        o_ref[...] = epilogue([_dot(x, wb[...]) for wb in wb_refs], []).astype(o_ref.dtype)

    @pl.when(valid_tab[s] == 0)
    def _():
        o_ref[...] = jnp.zeros(o_ref.shape, o_ref.dtype)


def grouped_matmul(x, ws, tabs, epilogue, out_dtype, tn, name):
    n_rows, kx = x.shape
    ne, kdim, n = ws[0].shape
    tmg = MOE_ROW_TILE
    tn = _tile(n, tn)
    n_steps = tabs[0].shape[0]
    grid_spec = pltpu.PrefetchScalarGridSpec(
        num_scalar_prefetch=7,
        grid=(n_steps,),
        in_specs=[pl.BlockSpec((tmg, kx), lambda s, e, jw, rx, j, r, f, v: (rx[s], 0))]
        + [pl.BlockSpec((1, kdim, tn), lambda s, e, jw, rx, j, r, f, v: (e[s], 0, jw[s])) for _ in ws],
        out_specs=pl.BlockSpec((tmg, tn), lambda s, e, jw, rx, j, r, f, v: (r[s], j[s])),
        scratch_shapes=[pltpu.VMEM((kdim, tn), BF16) for _ in ws],
    )
    return pl.pallas_call(
        functools.partial(_grouped_kernel, n_w=len(ws), epilogue=epilogue),
        grid_spec=grid_spec,
        out_shape=jax.ShapeDtypeStruct((n_rows, n), out_dtype),
        compiler_params=_cparams(("arbitrary",)),
        name=name,
    )(*tabs, x, *ws)


def _group_steps(padded, nj, n_tiles):
    tmg = MOE_ROW_TILE
    tiles = padded // tmg
    tile_start = jnp.cumsum(tiles) - tiles
    steps = tiles * nj
    step_end = jnp.cumsum(steps)
    step_start = step_end - steps
    total = step_end[-1]
    used_tiles = jnp.sum(tiles)
    s = jnp.arange(n_tiles * nj, dtype=jnp.int32)
    sc = jnp.minimum(s, total - 1)
    e = jnp.minimum(jnp.searchsorted(step_end, sc, side="right"), tiles.shape[0] - 1).astype(jnp.int32)
    local = sc - step_start[e]
    te = jnp.maximum(tiles[e], 1)
    jw = local // te
    rl = local % te
    rx = tile_start[e] + rl
    valid = s < total
    first = valid & (rl == 0)
    extra = jnp.maximum(s - total, 0)
    j = jnp.where(valid, jw, extra % nj)
    r = jnp.where(valid, rx, used_tiles + extra // nj)
    i32 = lambda a: a.astype(jnp.int32)
    return i32(e), i32(jw), i32(rx), i32(j), i32(r), i32(first), i32(valid)


def _combine_kernel(dest_ref, x_ref, route_ref, ys_hbm, o_ref, buf, sem, *, tb):
    def start(t, c):
        for k in range(TOP_K):
            pltpu.make_async_copy(ys_hbm.at[pl.ds(dest_ref[0, 0, TOP_K * t + k], 1)], buf.at[k, pl.ds(t, 1)],
                                  sem).start(priority=k % 2)
        return c

    lax.fori_loop(0, tb, start, 0)

    def wait(t, c):
        pltpu.make_async_copy(ys_hbm.at[pl.ds(0, 1)], buf.at[0, pl.ds(0, 1)], sem).wait()
        return c

    lax.fori_loop(0, TOP_K * tb, wait, 0)
    route = route_ref[...]
    o_ref[...] = x_ref[...] + route[:, 4:5] * buf[0] + route[:, 5:6] * buf[1]


def moe_combine(x, route, ys, dest):
    t, d = x.shape
    tb = _tile(t, MOE_TOKEN_BLOCK)
    return pl.pallas_call(
        functools.partial(_combine_kernel, tb=tb),
        grid=(t // tb,),
        in_specs=[pl.BlockSpec((1, 1, TOP_K * tb), lambda i: (i, 0, 0), memory_space=pltpu.SMEM),
                  pl.BlockSpec((tb, d), lambda i: (i, 0)), pl.BlockSpec((tb, LANES), lambda i: (i, 0)),
                  pl.BlockSpec(memory_space=pl.ANY)],
        out_specs=pl.BlockSpec((tb, d), lambda i: (i, 0)),
        out_shape=jax.ShapeDtypeStruct((t, d), F32),
        scratch_shapes=[pltpu.VMEM((TOP_K, tb, d), F32), pltpu.SemaphoreType.DMA(())],
        compiler_params=_cparams(("arbitrary",)),
        name="moe_combine",
    )(dest.reshape(t // tb, 1, TOP_K * tb), x, route, ys)


def moe_block(x, norm_g, w_router, b_router, w_gate, w_up, w_down):
    t, d = x.shape
    ne, _, f = w_gate.shape
    tmg = MOE_ROW_TILE
    hp, route, cnt = rmsnorm_router(x, norm_g, w_router, b_router)
    counts = cnt[0, :ne].astype(jnp.int32)
    padded = (counts + tmg - 1) // tmg * tmg
    starts = jnp.cumsum(padded) - padded
    ids = route[:, 0:2].astype(jnp.int32)
    dest = starts[ids] + route[:, 2:4].astype(jnp.int32)
    n_rows = t * TOP_K + ne * tmg
    n_tiles = n_rows // tmg
    src = jnp.zeros((n_rows,), jnp.int32).at[dest.reshape(-1)].set(
        jnp.repeat(jnp.arange(t, dtype=jnp.int32), TOP_K), unique_indices=True)
    xs = moe_dispatch(hp, src, n_rows)
    tn_up, tn_down = _tile(f, 512), _tile(d, 1024)
    act = grouped_matmul(xs, [w_gate, w_up], _group_steps(padded, f // tn_up, n_tiles), _epi_swiglu, BF16,
                         tn_up, "moe_up")
    ys = grouped_matmul(act, [w_down], _group_steps(padded, d // tn_down, n_tiles), _epi_plain, F32,
                        tn_down, "moe_down")
    return moe_combine(x, route, ys, dest)


def _mm_kernel(*refs, n_w, n_e, nk, epilogue):
    a_ref = refs[0]
    w_refs = refs[1:1 + n_w]
    e_refs = refs[1 + n_w:1 + n_w + n_e]
    o_ref = refs[1 + n_w + n_e]
    acc_refs = refs[2 + n_w + n_e:]
    a = a_ref[...]
    parts = [_dot(a, w[...]) for w in w_refs]
    if nk == 1:
        o_ref[...] = epilogue(parts, [e[...] for e in e_refs]).astype(o_ref.dtype)
        return
    k = pl.program_id(2)

    @pl.when(k == 0)
    def _():
        for acc, p in zip(acc_refs, parts):
            acc[...] = p

    @pl.when(k > 0)
    def _():
        for acc, p in zip(acc_refs, parts):
            acc[...] += p

    @pl.when(k == nk - 1)
    def _():
        o_ref[...] = epilogue([acc[...] for acc in acc_refs], [e[...] for e in e_refs]).astype(o_ref.dtype)


def matmul(a, ws, epilogue, extras=(), extra_kinds=(), out_dtype=F32, tm=1024, tn=1024, tk=4096, w_prefix=(),
           name="matmul"):
    m, kdim = a.shape
    n = ws[0].shape[-1]
    tm, tn, tk = _tile(m, tm), _tile(n, tn), _tile(kdim, tk)
    nk = kdim // tk
    squeezed = (None,) * len(w_prefix)
    in_specs = [pl.BlockSpec((tm, tk), lambda i, j, k: (i, k))]
    in_specs += [pl.BlockSpec(squeezed + (tk, tn), lambda i, j, k: (*w_prefix, k, j)) for _ in ws]
    for kind in extra_kinds:
        if kind == "mn":
            in_specs.append(pl.BlockSpec((tm, tn), lambda i, j, k: (i, j)))
        elif kind == "m":
            in_specs.append(pl.BlockSpec((tm, LANES), lambda i, j, k: (i, 0)))
        else:
            in_specs.append(pl.BlockSpec((1, tn), lambda i, j, k: (0, j)))
    scratch = [pltpu.VMEM((tm, tn), F32) for _ in ws] if nk > 1 else []
    return pl.pallas_call(
        functools.partial(_mm_kernel, n_w=len(ws), n_e=len(extras), nk=nk, epilogue=epilogue),
        grid=(m // tm, n // tn, nk),
        in_specs=in_specs,
        out_specs=pl.BlockSpec((tm, tn), lambda i, j, k: (i, j)),
        out_shape=jax.ShapeDtypeStruct((m, n), out_dtype),
        scratch_shapes=scratch,
        compiler_params=_cparams(("parallel", "parallel", "arbitrary")),
        name=name,
    )(a, *ws, *extras)


def _epi_plain(parts, extras):
    return parts[0]


def _epi_residual(parts, extras):
    return extras[0] + parts[0]


def _epi_swiglu(parts, extras):
    g, u = parts
    return g * jax.nn.sigmoid(g) * u


def _merge_kernel(h_ref, wg_ref, b_ref, y_ref, wb_ref, o_ref, accm_ref, *accg, nk, nb):
    j = pl.program_id(2)
    k = pl.program_id(3)

    @pl.when((pl.program_id(0) == 0) & (pl.program_id(1) == 0) & (j == 0) & (k == 0))
    def _():
        accm_ref[...] = jnp.zeros(accm_ref.shape, F32)

    part = _dot(h_ref[...], wg_ref[0].astype(BF16))

    def finish(pre):
        term = jax.nn.sigmoid(pre + b_ref[0]) * _dot(y_ref[0], wb_ref[0].astype(BF16))
        total = jnp.where(j == 0, term, accm_ref[...] + term)
        accm_ref[...] = total
        o_ref[...] = total.astype(o_ref.dtype)

    if nk == 1:
        finish(part)
        return

    accg_ref, = accg

    @pl.when(k == 0)
    def _():
        accg_ref[...] = part

    @pl.when(k > 0)
    def _():
        accg_ref[...] += part

    @pl.when(k == nk - 1)
    def _():
        finish(accg_ref[...])


def gated_merge(h, w_gate, b_gate, ys, w_branch, layer, tm=1024, tn=512, tk=4096):
    t, d = h.shape
    nb, _, wdt = ys.shape
    tm, tn, tk = _tile(t, tm), _tile(d, tn), _tile(d, tk)
    nk = d // tk
    assert nb >= 2
    return pl.pallas_call(
        functools.partial(_merge_kernel, nk=nk, nb=nb),
        grid=(t // tm, d // tn, nb, nk),
        in_specs=[
            pl.BlockSpec((tm, tk), lambda i, n, j, k: (i, k)),
            pl.BlockSpec((None, 1, tk, tn), lambda i, n, j, k: (layer, j, k, n)),
            pl.BlockSpec((1, 1, tn), lambda i, n, j, k: (j, 0, n)),
            pl.BlockSpec((1, tm, wdt), lambda i, n, j, k: (j, i, 0)),
            pl.BlockSpec((None, 1, wdt, tn), lambda i, n, j, k: (layer, j, 0, n)),
        ],
        out_specs=pl.BlockSpec((tm, tn), lambda i, n, j, k: (i, n)),
        out_shape=jax.ShapeDtypeStruct((t, d), BF16),
        scratch_shapes=[pltpu.VMEM((tm, tn), F32)] * (1 if nk == 1 else 2),
        compiler_params=_cparams(("arbitrary", "arbitrary", "arbitrary", "arbitrary")),
        name="gated_merge",
    )(h, w_gate, b_gate, ys, w_branch)


def _softplus(x):
    return jnp.maximum(x, 0.0) + jnp.log1p(jnp.exp(-jnp.abs(x)))


def _ssd_kernel(xbc_ref, z_ref, dt_ref, dtt_ref, cw_ref, cb_ref, dtb_ref, dtbt_ref, alog_ref, alogt_ref,
                dskip_ref, ng_ref, expand_ref, ybuf_ref, o_ref, xe_ref, st_ref, *, n_heads, d_inner):
    del ybuf_ref
    q = SSD_CHUNK
    gn = SSD_N_GROUPS * SSD_D_STATE
    hpg = n_heads // SSD_N_GROUPS
    gw = hpg * SSD_HEAD_DIM
    c = pl.program_id(1)

    @pl.when(c == 0)
    def _():
        xe_ref[0:8, :] = jnp.zeros((8, xe_ref.shape[1]), F32)
        st_ref[...] = jnp.zeros(st_ref.shape, F32)

    xe_ref[8:8 + q, :] = xbc_ref[...]
    acc = cb_ref[...] + cw_ref[3:4, :] * xe_ref[8:8 + q, :]
    for kk in range(3):
        acc = acc + cw_ref[kk:kk + 1, :] * xe_ref[5 + kk:5 + kk + q, :]
    xe_ref[0:8, :] = xe_ref[q:q + 8, :]
    xc = acc * jax.nn.sigmoid(acc)
    xs = xc[:, :d_inner]
    bm = xc[:, d_inner:d_inner + gn].astype(BF16)
    cm = xc[:, d_inner + gn:].astype(BF16)

    dt = _softplus(dt_ref[...] + dtb_ref[...])
    da = dt * (-jnp.exp(alog_ref[...]))
    row = lax.broadcasted_iota(jnp.int32, (q, q), 0)
    col = lax.broadcasted_iota(jnp.int32, (q, q), 1)
    lower = row >= col
    tril = jnp.where(lower, 1.0, 0.0).astype(BF16)
    acs = _dot_exact_lhs01(tril, da)
    dtt = _softplus(dtt_ref[0] + dtbt_ref[...])
    dat = dtt * (-jnp.exp(alogt_ref[...]))
    triu = jnp.where(row <= col, 1.0, 0.0).astype(BF16)
    acst = _dot_exact_rhs01(dat, triu)

    expand = expand_ref[...]
    dt_x = _dot_exact_rhs01(dt, expand)
    acs_x = _dot_exact_rhs01(acs, expand)
    tot_x = acs_x[q - 1:q, :]
    xdt = xs * dt_x
    xdt_b = xdt.astype(BF16)
    xw_b = (xdt * jnp.exp(tot_x - acs_x)).astype(BF16)
    in_decay = jnp.exp(acs_x)
    chunk_decay = jnp.exp(tot_x)

    lane = lax.broadcasted_iota(jnp.int32, (1, gw), 1)
    ys = []
    for g in range(SSD_N_GROUPS):
        cg = cm[:, g * SSD_D_STATE:(g + 1) * SSD_D_STATE]
        bg = bm[:, g * SSD_D_STATE:(g + 1) * SSD_D_STATE]
        sl = slice(g * gw, (g + 1) * gw)
        gmat = lax.dot_general(cg, bg, (((1,), (1,)), ((), ())), preferred_element_type=F32)
        state = st_ref[g]
        y_g = _dot(cg, state.astype(BF16)) * in_decay[:, sl]
        xg = xdt_b[:, sl]
        for hh in range(hpg):
            hd = g * hpg + hh
            seg = acs[:, hd:hd + 1] - acst[hd:hd + 1, :]
            lmat = jnp.exp(jnp.where(lower, seg, -jnp.inf))
            mh = (gmat * lmat).astype(BF16)
            head_cols = (lane >= hh * SSD_HEAD_DIM) & (lane < (hh + 1) * SSD_HEAD_DIM)
            y_g = y_g + _dot(mh, jnp.where(head_cols, xg, jnp.zeros_like(xg)))
        upd = lax.dot_general(bg, xw_b[:, sl], (((0,), (0,)), ((), ())), preferred_element_type=F32)
        st_ref[g] = state * chunk_decay[:, sl] + upd
        ys.append(y_g)
    y = jnp.concatenate(ys, axis=-1) + xs * dskip_ref[...]
    zz = z_ref[...]
    y = y * (zz * jax.nn.sigmoid(zz))
    outs = []
    for g in range(SSD_N_GROUPS):
        yg = y[:, g * gw:(g + 1) * gw]
        ms = jnp.mean(yg * yg, axis=-1, keepdims=True)
        outs.append(yg * lax.rsqrt(ms + RMS_EPS))
    o_ref[0] = (jnp.concatenate(outs, axis=-1) * ng_ref[...]).astype(o_ref.dtype)


def ssd_mixer(proj, dt_t, cols, bsz, seq, conv_w, conv_b, dt_bias, a_log, d_skip, norm_g, ybuf, slot):
    n_heads = dt_bias.shape[0]
    d_inner = n_heads * SSD_HEAD_DIM
    conv_dim = conv_w.shape[1]
    q = SSD_CHUNK
    nc = seq // q
    pad = LANES - n_heads
    expand = jnp.repeat(jnp.eye(LANES, dtype=BF16)[:, :n_heads], SSD_HEAD_DIM, axis=1)
    dtb = jnp.pad(dt_bias, (0, pad)).reshape(1, LANES)
    alog = jnp.pad(a_log, (0, pad)).reshape(1, LANES)
    x_blk, z_blk, dt_blk = cols["xbc"] // conv_dim, cols["z"] // d_inner, cols["dt"] // LANES
    const = lambda b, c: (0, 0)
    return pl.pallas_call(
        functools.partial(_ssd_kernel, n_heads=n_heads, d_inner=d_inner),
        grid=(bsz, nc),
        in_specs=[
            pl.BlockSpec((q, conv_dim), lambda b, c: (b * nc + c, x_blk)),
            pl.BlockSpec((q, d_inner), lambda b, c: (b * nc + c, z_blk)),
            pl.BlockSpec((q, LANES), lambda b, c: (b * nc + c, dt_blk)),
            pl.BlockSpec((1, n_heads, q), lambda b, c: (b, 0, c)),
            pl.BlockSpec((4, conv_dim), const),
            pl.BlockSpec((1, conv_dim), const),
            pl.BlockSpec((1, LANES), const),
            pl.BlockSpec((n_heads, 1), const),
            pl.BlockSpec((1, LANES), const),
            pl.BlockSpec((n_heads, 1), const),
            pl.BlockSpec((1, d_inner), const),
            pl.BlockSpec((1, d_inner), const),
            pl.BlockSpec((LANES, d_inner), const),
            pl.BlockSpec(memory_space=pl.ANY),
        ],
        out_specs=pl.BlockSpec((1, q, d_inner), lambda b, c: (slot, b * nc + c, 0)),
        out_shape=jax.ShapeDtypeStruct(ybuf.shape, ybuf.dtype),
        scratch_shapes=[pltpu.VMEM((q + 8, conv_dim), F32),
                        pltpu.VMEM((SSD_N_GROUPS, SSD_D_STATE, d_inner // SSD_N_GROUPS), F32)],
        input_output_aliases={13: 0},
        compiler_params=_cparams(("parallel", "arbitrary")),
        name="ssd_mixer",
    )(proj, proj, proj, dt_t, conv_w, conv_b.reshape(1, conv_dim), dtb, dt_bias.reshape(n_heads, 1),
      alog, a_log.reshape(n_heads, 1), jnp.repeat(d_skip, SSD_HEAD_DIM).reshape(1, d_inner),
      norm_g.reshape(1, d_inner), expand, ybuf)


S5_ROWS = 8


def _s5_disc_kernel(are_ref, aim_ref, ldt_ref, brt_ref, bit_ref, pr_ref, pi_ref, bbr_ref, bbi_ref):
    dt = jnp.exp(ldt_ref[...])
    ar, ai = are_ref[...], aim_ref[...]
    mag = jnp.exp(ar * dt)
    lr, li = mag * jnp.cos(ai * dt), mag * jnp.sin(ai * dt)
    den = ar * ar + ai * ai
    fr = ((lr - 1.0) * ar + li * ai) / den
    fi = (li * ar - (lr - 1.0) * ai) / den
    pr, pi = lr, li
    for k in range(S5_ROWS):
        pr_ref[k] = pr
        pi_ref[k] = pi
        pr, pi = pr * lr - pi * li, pr * li + pi * lr
    br, bi = brt_ref[...], bit_ref[...]
    bbr_ref[...] = fr[:, None, :] * br - fi[:, None, :] * bi
    bbi_ref[...] = fr[:, None, :] * bi + fi[:, None, :] * br


def s5_discretise(a_re, a_im, log_dt, b_re, b_im):
    g, n = a_re.shape
    c = b_re.shape[2]
    brt, bit = jnp.swapaxes(b_re, 1, 2), jnp.swapaxes(b_im, 1, 2)
    return pl.pallas_call(
        _s5_disc_kernel,
        out_shape=[jax.ShapeDtypeStruct((S5_ROWS, g, n), F32), jax.ShapeDtypeStruct((S5_ROWS, g, n), F32),
                   jax.ShapeDtypeStruct((g, c, n), F32), jax.ShapeDtypeStruct((g, c, n), F32)],
        name="s5_discretise",
    )(a_re, a_im, log_dt.reshape(g, 1), brt, bit)


def _block_diag(blocks, per):
    g, r, c = blocks.shape
    nb = g // per
    eye = jnp.eye(per, dtype=blocks.dtype)
    out = blocks.reshape(nb, per, r, 1, c) * eye[None, :, None, :, None]
    return out.reshape(nb, per * r, per * c)


def _s5_kernel(u_ref, wr_ref, wi_ref, cr_ref, ci_ref, ctab_ref, d_ref, wglu_ref, bglu_ref, ybuf_ref, o_ref,
               xr_scr, xi_scr, carry_ref, y_scr, *, nblk, sw):
    del ybuf_ref
    tm = u_ref.shape[0]

    @pl.when(pl.program_id(1) == 0)
    def _():
        carry_ref[...] = jnp.zeros(carry_ref.shape, F32)

    for blk in range(nblk):
        cs = slice(blk * sw, (blk + 1) * sw)
        ls = slice(blk * LANES, (blk + 1) * LANES)
        u = u_ref[:, ls]
        ub = u.astype(BF16)
        xr_scr[...] = _dot(ub, wr_ref[blk])
        xi_scr[...] = _dot(ub, wi_ref[blk])

        def body(i, carry, cs=cs):
            c_re, c_im = carry
            r0 = pl.multiple_of(i * S5_ROWS, S5_ROWS)
            sr, si = xr_scr[pl.ds(r0, S5_ROWS), :], xi_scr[pl.ds(r0, S5_ROWS), :]
            for lvl, dist in enumerate((1, 2, 4)):
                lr, li = ctab_ref[2 * lvl, :, cs], ctab_ref[2 * lvl + 1, :, cs]
                tr, ti = pltpu.roll(sr, dist, axis=0), pltpu.roll(si, dist, axis=0)
                sr, si = sr + lr * tr - li * ti, si + lr * ti + li * tr
            pr, pi = ctab_ref[6, :, cs], ctab_ref[7, :, cs]
            sr, si = sr + pr * c_re - pi * c_im, si + pr * c_im + pi * c_re
            xr_scr[pl.ds(r0, S5_ROWS), :] = sr
            xi_scr[pl.ds(r0, S5_ROWS), :] = si
            return sr[S5_ROWS - 1:S5_ROWS, :], si[S5_ROWS - 1:S5_ROWS, :]

        c_re, c_im = lax.fori_loop(0, tm // S5_ROWS, body, (carry_ref[0:1, cs], carry_ref[1:2, cs]), unroll=2)
        carry_ref[0:1, cs] = c_re
        carry_ref[1:2, cs] = c_im
        y = _dot(xr_scr[...].astype(BF16), cr_ref[blk]) - _dot(xi_scr[...].astype(BF16), ci_ref[blk])
        y_scr[:, ls] = jax.nn.gelu(y + d_ref[:, ls] * u)
    y = y_scr[...]
    o_ref[0] = (y * jax.nn.sigmoid(_dot(y.astype(BF16), wglu_ref[...]) + bglu_ref[...])).astype(o_ref.dtype)


def s5_mixer(proj, u_col, bsz, seq, a_re, a_im, b_re, b_im, c_re, c_im, log_dt, d_skip, w_glu, b_glu,
             ybuf, slot):
    t = bsz * seq
    g, n = a_re.shape
    width = g * S5_GROUP
    per = S5_GROUPS_PER_BLOCK
    nblk = g // per
    sw = per * n
    ns = g * n
    pw_r, pw_i, bbr, bbi = s5_discretise(a_re, a_im, log_dt, b_re, b_im)
    w_r = _block_diag(bbr, per).astype(BF16)
    w_i = _block_diag(bbi, per).astype(BF16)
    c_r = _block_diag(jnp.swapaxes(c_re, 1, 2), per).astype(BF16)
    c_i = _block_diag(jnp.swapaxes(c_im, 1, 2), per).astype(BF16)
    pw_r, pw_i = pw_r.reshape(S5_ROWS, ns), pw_i.reshape(S5_ROWS, ns)
    sub = jnp.arange(S5_ROWS)[:, None]
    kinds = []
    for dist in (1, 2, 4):
        kinds += [jnp.where(sub >= dist, pw_r[dist - 1][None, :], 0.0),
                  jnp.where(sub >= dist, pw_i[dist - 1][None, :], 0.0)]
    ctab = jnp.stack(kinds + [pw_r, pw_i])
    tm = _tile(seq, 256)
    nt = seq // tm
    const2 = lambda b, c: (0, 0)
    const3 = lambda b, c: (0, 0, 0)
    return pl.pallas_call(
        functools.partial(_s5_kernel, nblk=nblk, sw=sw),
        grid=(bsz, nt),
        in_specs=[pl.BlockSpec((tm, width), lambda b, c: (b * nt + c, u_col // width)),
                  pl.BlockSpec((nblk, LANES, sw), const3), pl.BlockSpec((nblk, LANES, sw), const3),
                  pl.BlockSpec((nblk, sw, LANES), const3), pl.BlockSpec((nblk, sw, LANES), const3),
                  pl.BlockSpec((8, S5_ROWS, ns), const3),
                  pl.BlockSpec((1, width), const2), pl.BlockSpec((width, width), const2),
                  pl.BlockSpec((1, width), const2), pl.BlockSpec(memory_space=pl.ANY)],
        out_specs=pl.BlockSpec((1, tm, width), lambda b, c: (slot, b * nt + c, 0)),
        out_shape=jax.ShapeDtypeStruct(ybuf.shape, ybuf.dtype),
        scratch_shapes=[pltpu.VMEM((tm, sw), F32), pltpu.VMEM((tm, sw), F32), pltpu.VMEM((8, ns), F32),
                        pltpu.VMEM((tm, width), F32)],
        input_output_aliases={9: 0},
        compiler_params=_cparams(("parallel", "arbitrary")),
        name="s5_mixer",
    )(proj, w_r, w_i, c_r, c_i, ctab, d_skip.reshape(1, width), w_glu.astype(BF16), b_glu.reshape(1, width),
      ybuf)


def _flash_kernel(*refs, tq, n_maps, has_bias, finalize):
    if has_bias:
        q_ref, k_ref, v_ref, bias_ref, far_ref, *rest = refs
    else:
        q_ref, k_ref, v_ref, *rest = refs
    fin_refs = rest[:-2]
    o_ref = rest[-1]
    h = pl.program_id(1)
    qi = pl.program_id(2)
    rows = n_maps * tq
    dvo = v_ref.shape[1]
    qv = q_ref[...].reshape(rows, q_ref.shape[-1])

    def tiled(b):
        return jnp.concatenate([b] * n_maps, axis=0) if n_maps > 1 else b

    def block(kb, carry, add, shift):
        m, l, acc = carry
        off = pl.multiple_of(kb * tq, tq)
        s = lax.dot_general(qv, k_ref[pl.ds(off, tq), :], (((1,), (1,)), ((), ())), preferred_element_type=F32)
        sb = s.astype(BF16)
        if add is not None:
            sb = sb + add
        mx = jnp.max(sb, axis=-1, keepdims=True).astype(F32)
        m_hi = jnp.maximum(m, mx if shift is None else mx + shift)
        ref_b = (m_hi if shift is None else m_hi - shift).astype(BF16)
        m_new = ref_b.astype(F32) if shift is None else ref_b.astype(F32) + shift
        alpha = jnp.exp2(m - m_new)
        p = jnp.exp2(sb - ref_b)
        l_new = alpha * l + jnp.sum(p.astype(F32), axis=-1, keepdims=True)
        return m_new, l_new, alpha * acc + _dot(p, v_ref[pl.ds(off, tq), :])

    carry = (jnp.full((rows, 1), -jnp.inf, F32), jnp.zeros((rows, 1), F32), jnp.zeros((rows, dvo), F32))
    if has_bias:
        far = far_ref[h]
        n_far = jnp.maximum(qi - 1, 0)
        carry = lax.fori_loop(0, n_far, lambda kb, c: block(kb, c, None, far), carry)
        carry = lax.fori_loop(n_far, qi, lambda kb, c: block(kb, c, tiled(bias_ref[0, 1]), None), carry)
    else:
        carry = lax.fori_loop(0, qi, lambda kb, c: block(kb, c, None, None), carry)
    row = lax.broadcasted_iota(jnp.int32, (tq, tq), 0)
    col = lax.broadcasted_iota(jnp.int32, (tq, tq), 1)
    diag = jnp.where(row >= col, bias_ref[0, 0].astype(F32) if has_bias else 0.0, -jnp.inf).astype(BF16)
    m, l, acc = block(qi, carry, tiled(diag), None)
    o_ref[0] = finalize(acc / l, [r[...] for r in fin_refs]).astype(o_ref.dtype)


def flash_attention(q, k, v, bsz, seq, n_heads, dq, ybuf, slot, n_maps=1, bias=None, far=None, fin=(),
                    finalize=None, tq=512):
    dv = v.shape[1] // n_heads
    tq = _tile(seq, tq)
    nq = seq // tq
    has_bias = bias is not None
    if finalize is None:
        finalize = lambda o, extras: o
    in_specs = [pl.BlockSpec((n_maps, tq, dq), lambda b, h, i: (0, b * nq + i, h)),
                pl.BlockSpec((seq, dq), lambda b, h, i: (b, h)),
                pl.BlockSpec((seq, dv), lambda b, h, i: (b, h))]
    args = [q, k, v]
    if has_bias:
        in_specs += [pl.BlockSpec((1, 2, tq, tq), lambda b, h, i: (h, 0, 0, 0)),
                     pl.BlockSpec(memory_space=pltpu.SMEM)]
        args += [bias, far]
    for f in fin:
        in_specs.append(pl.BlockSpec(f.shape, lambda b, h, i: (0, 0)))
        args.append(f)
    in_specs.append(pl.BlockSpec(memory_space=pl.ANY))
    args.append(ybuf)
    return pl.pallas_call(
        functools.partial(_flash_kernel, tq=tq, n_maps=n_maps, has_bias=has_bias, finalize=finalize),
        grid=(bsz, n_heads, nq),
        in_specs=in_specs,
        out_specs=pl.BlockSpec((1, tq, dv), lambda b, h, i: (slot, b * nq + i, h)),
        out_shape=jax.ShapeDtypeStruct(ybuf.shape, ybuf.dtype),
        input_output_aliases={len(args) - 1: 0},
        compiler_params=_cparams(("parallel", "parallel", "arbitrary")),
        name="flash_bias" if has_bias else "flash",
    )(*args)


def _mla_proj_kernel(cq_ref, ckv_ref, kpe_ref, cos_ref, sin_ref, qng_ref, kvng_ref, wq_ref, wkv_ref,
                     qg_ref, kg_ref, q_ref, k_ref, v_ref, *, n_heads, scale):
    half = LANES // 2

    def norm(x, g):
        return (x * lax.rsqrt(jnp.mean(x * x, axis=-1, keepdims=True) + RMS_EPS) * g).astype(BF16)

    cq = norm(cq_ref[...], qng_ref[...])
    ckv = norm(ckv_ref[...], kvng_ref[...])
    cos, sin = cos_ref[...], sin_ref[...]
    kpe = kpe_ref[...]
    kpe_ss = jnp.sum(jnp.where(lax.broadcasted_iota(jnp.int32, kpe.shape, 1) < half, kpe * kpe, 0.0),
                     axis=-1, keepdims=True)
    n_qk = float(MLA_NOPE + MLA_ROPE)
    qg, kg = qg_ref[...], kg_ref[...]

    def rotary(hi, gains, inv):
        return inv * (hi * (gains[1:2] * cos) + pltpu.roll(hi, half, axis=1) * (gains[2:3] * sin))

    k_rot_raw = kpe * (kg[1:2] * cos) + pltpu.roll(kpe, half, axis=1) * (kg[2:3] * sin)
    for h in range(n_heads):
        xq = _dot(cq, wq_ref[:, h * MLA_PAD:(h + 1) * MLA_PAD])
        lo, hi = xq[:, :LANES], xq[:, LANES:]
        ss = jnp.sum(lo * lo, axis=-1, keepdims=True) + 0.5 * jnp.sum(hi * hi, axis=-1, keepdims=True)
        inv = lax.rsqrt(ss / n_qk + RMS_EPS) * scale
        q_ref[:, h * MLA_PAD:h * MLA_PAD + LANES] = (lo * inv * qg[0:1]).astype(BF16)
        q_ref[:, h * MLA_PAD + LANES:(h + 1) * MLA_PAD] = rotary(hi, qg, inv).astype(BF16)
        xkv = _dot(ckv, wkv_ref[:, h * MLA_PAD:(h + 1) * MLA_PAD])
        kn = xkv[:, :LANES]
        inv_k = lax.rsqrt((jnp.sum(kn * kn, axis=-1, keepdims=True) + kpe_ss) / n_qk + RMS_EPS)
        k_ref[:, h * MLA_PAD:h * MLA_PAD + LANES] = (kn * inv_k * kg[0:1]).astype(BF16)
        k_ref[:, h * MLA_PAD + LANES:(h + 1) * MLA_PAD] = (k_rot_raw * inv_k).astype(BF16)
        v_ref[:, h * LANES:(h + 1) * LANES] = xkv[:, LANES:].astype(BF16)


def _rope_gain_rows(gain):
    rot = gain[MLA_NOPE:]
    r = MLA_ROPE // 2
    swapped = jnp.concatenate([rot[r:], rot[:r]])
    z = jnp.zeros((LANES - MLA_ROPE,), F32)
    return jnp.stack([gain[:MLA_NOPE], jnp.concatenate([rot, z]), jnp.concatenate([swapped, z])])


def mla_mixer(proj, cols, bsz, seq, q_norm_g, w_q_up, kv_norm_g, w_kv_up, q_gain, k_gain, ybuf, slot):
    t = bsz * seq
    hds = MLA_N_HEADS
    q_rank, kv_rank = w_q_up.shape[0], w_kv_up.shape[0]
    r = MLA_ROPE // 2
    wq = w_q_up.reshape(q_rank, hds, MLA_NOPE + MLA_ROPE)
    wq = jnp.concatenate([wq, wq[:, :, MLA_NOPE + r:], wq[:, :, MLA_NOPE:MLA_NOPE + r]], axis=-1)
    wq = wq.reshape(q_rank, hds * MLA_PAD).astype(BF16)
    wkv = w_kv_up.astype(BF16)
    inv = 1.0 / (ROPE_THETA ** (jnp.arange(0, MLA_ROPE, 2, dtype=F32) / MLA_ROPE))
    ang = jnp.arange(seq, dtype=F32)[:, None] * inv[None, :]
    zer = jnp.zeros((seq, LANES - MLA_ROPE), F32)
    cos_t = jnp.concatenate([jnp.cos(ang), jnp.cos(ang), zer], axis=-1)
    sin_t = jnp.concatenate([-jnp.sin(ang), jnp.sin(ang), zer], axis=-1)
    tm = _tile(seq, 512)
    npos = seq // tm
    const = lambda i: (0, 0)
    q, k, v = pl.pallas_call(
        functools.partial(_mla_proj_kernel, n_heads=hds, scale=LOG2E * float(MLA_NOPE + MLA_ROPE) ** -0.5),
        grid=(t // tm,),
        in_specs=[pl.BlockSpec((tm, q_rank), lambda i: (i, cols["cq"] // q_rank)),
                  pl.BlockSpec((tm, kv_rank), lambda i: (i, cols["ckv"] // kv_rank)),
                  pl.BlockSpec((tm, LANES), lambda i: (i, cols["kpe"] // LANES)),
                  pl.BlockSpec((tm, LANES), lambda i: (i % npos, 0)),
                  pl.BlockSpec((tm, LANES), lambda i: (i % npos, 0)),
                  pl.BlockSpec((1, q_rank), const), pl.BlockSpec((1, kv_rank), const),
                  pl.BlockSpec((q_rank, hds * MLA_PAD), const), pl.BlockSpec((kv_rank, hds * MLA_PAD), const),
                  pl.BlockSpec((3, LANES), const), pl.BlockSpec((3, LANES), const)],
        out_specs=[pl.BlockSpec((tm, hds * MLA_PAD), lambda i: (i, 0)),
                   pl.BlockSpec((tm, hds * MLA_PAD), lambda i: (i, 0)),
                   pl.BlockSpec((tm, hds * LANES), lambda i: (i, 0))],
        out_shape=[jax.ShapeDtypeStruct((t, hds * MLA_PAD), BF16), jax.ShapeDtypeStruct((t, hds * MLA_PAD), BF16),
                   jax.ShapeDtypeStruct((t, hds * LANES), BF16)],
        compiler_params=_cparams(("parallel",)),
        name="mla_proj",
    )(proj, proj, proj, cos_t, sin_t, q_norm_g.reshape(1, q_rank), kv_norm_g.reshape(1, kv_rank), wq, wkv,
      _rope_gain_rows(q_gain), _rope_gain_rows(k_gain))
    return flash_attention(q.reshape(1, t, hds * MLA_PAD), k, v, bsz, seq, hds, MLA_PAD, ybuf, slot, tq=1024)


def _diff_prep_kernel(q_ref, k_ref, v_ref, qg_ref, kg_ref, seg_ref, qo_ref, ko_ref, vo_ref, *, hd, scale):
    seg = seg_ref[...]
    lane = lax.broadcasted_iota(jnp.int32, (1, LANES), 1)
    n_col = q_ref.shape[1] // LANES
    for c in range(n_col):
        sl = slice(c * LANES, (c + 1) * LANES)
        xq, xk = q_ref[:, sl], k_ref[:, sl]
        ssq = _dot_exact_rhs01(xq * xq, seg)
        ssk = _dot_exact_rhs01(xk * xk, seg)
        qn = xq * lax.rsqrt(ssq / hd + RMS_EPS) * (qg_ref[...] * scale)
        kn = xk * lax.rsqrt(ssk / hd + RMS_EPS) * kg_ref[...]
        qo_ref[0, :, sl] = jnp.where(lane < hd, qn, 0.0).astype(BF16)
        qo_ref[1, :, sl] = jnp.where(lane >= hd, qn, 0.0).astype(BF16)
        ko_ref[:, sl] = kn.astype(BF16)
    vo_ref[...] = v_ref[...].astype(BF16)


def _t5_bucket_np(n):
    max_exact = REL_BUCKETS // 2
    nf = np.maximum(n, 1).astype(np.float32)
    large = max_exact + (np.log(nf / np.float32(max_exact)) / np.float32(math.log(REL_MAX_DIST / max_exact))
                         * np.float32(REL_BUCKETS - max_exact)).astype(np.int32)
    large = np.minimum(large, REL_BUCKETS - 1)
    return np.where(n < max_exact, n, large).astype(np.int32)


def _bias_tiles_kernel(bkt_ref, rb_ref, o_ref):
    h = pl.program_id(0)
    for d in range(2):
        bkt = bkt_ref[d]
        tile = jnp.zeros(bkt.shape, F32)
        for b in range(REL_BUCKETS):
            tile = jnp.where(bkt == b, rb_ref[b, h] * LOG2E, tile)
        o_ref[0, d] = tile.astype(o_ref.dtype)


def _diff_finalize(lambda_init, tq):
    def fin(o, extras):
        lq1, lk1, lq2, lk2, sub_g = extras
        lam = (jnp.exp(jnp.sum(lq1 * lk1, axis=-1, keepdims=True))
               - jnp.exp(jnp.sum(lq2 * lk2, axis=-1, keepdims=True)) + lambda_init)
        d = o[:tq] - lam * o[tq:]
        ms = jnp.mean(d * d, axis=-1, keepdims=True)
        return d * lax.rsqrt(ms + RMS_EPS) * (sub_g * (1.0 - lambda_init))
    return fin


def t5_bias_tiles(rel_bias, seq):
    hds = rel_bias.shape[1]
    tq = _tile(seq, 512)
    assert tq + 1 > REL_MAX_DIST, "blocks two or more before the diagonal must all be at the far distance"
    idx = np.arange(tq)
    dist0 = np.maximum(idx[:, None] - idx[None, :], 0)
    dist1 = idx[:, None] - idx[None, :] + tq
    bkt = jnp.asarray(np.stack([_t5_bucket_np(dist0), _t5_bucket_np(dist1)]))
    bias = pl.pallas_call(
        _bias_tiles_kernel,
        grid=(hds,),
        in_specs=[pl.BlockSpec((2, tq, tq), lambda h: (0, 0, 0)), pl.BlockSpec(memory_space=pltpu.SMEM)],
        out_specs=pl.BlockSpec((1, 2, tq, tq), lambda h: (h, 0, 0, 0)),
        out_shape=jax.ShapeDtypeStruct((hds, 2, tq, tq), BF16),
        compiler_params=_cparams(("parallel",)),
        name="t5_bias_tiles",
    )(bkt, rel_bias)
    return bias, rel_bias[REL_BUCKETS - 1] * LOG2E, tq


def diff_mixer(proj, cols, bsz, seq, q_gain, k_gain, lq1, lk1, lq2, lk2, subln_g, bias_tiles, lambda_init,
               ybuf, slot):
    t = bsz * seq
    hds = DIFF_N_HEADS
    hd = q_gain.shape[0]
    width = hds * 2 * hd
    tm = _tile(t, 512)
    seg = (np.arange(LANES)[:, None] // hd == np.arange(LANES)[None, :] // hd)
    seg = jnp.asarray(seg, BF16)
    gq, gk = jnp.tile(q_gain, LANES // hd).reshape(1, LANES), jnp.tile(k_gain, LANES // hd).reshape(1, LANES)
    const = lambda i: (0, 0)
    qd, kd, vd = pl.pallas_call(
        functools.partial(_diff_prep_kernel, hd=hd, scale=LOG2E * float(hd) ** -0.5),
        grid=(t // tm,),
        in_specs=[pl.BlockSpec((tm, width), lambda i: (i, cols["dq"] // width)),
                  pl.BlockSpec((tm, width), lambda i: (i, cols["dk"] // width)),
                  pl.BlockSpec((tm, width), lambda i: (i, cols["dv"] // width)),
                  pl.BlockSpec((1, LANES), const), pl.BlockSpec((1, LANES), const),
                  pl.BlockSpec((LANES, LANES), const)],
        out_specs=[pl.BlockSpec((2, tm, width), lambda i: (0, i, 0)),
                   pl.BlockSpec((tm, width), lambda i: (i, 0)), pl.BlockSpec((tm, width), lambda i: (i, 0))],
        out_shape=[jax.ShapeDtypeStruct((2, t, width), BF16), jax.ShapeDtypeStruct((t, width), BF16),
                   jax.ShapeDtypeStruct((t, width), BF16)],
        compiler_params=_cparams(("parallel",)),
        name="diff_prep",
    )(proj, proj, proj, gq, gk, seg)

    bias, far, tq = bias_tiles
    fin = tuple(a.reshape(1, -1) for a in (lq1, lk1, lq2, lk2, subln_g))
    return flash_attention(qd, kd, vd, bsz, seq, hds, 2 * hd, ybuf, slot, n_maps=2, bias=bias, far=far, fin=fin,
                           finalize=_diff_finalize(lambda_init, tq), tq=tq)


def _pack_w_in(w_in_l, widths):
    names = ("z", "xbc", "dt", "u", "cq", "ckv", "kpe", "dq", "dk", "dv")
    starts = dict(zip(names, np.cumsum([0] + list(widths))[:-1]))
    width = dict(zip(names, widths))
    nl, d, n_in = w_in_l.shape
    kpe_w, dt_w, r = width["kpe"], width["dt"], width["kpe"] // 2
    order = ("xbc", "z", "u", "dq", "dk", "dv", "ckv", "cq", "kpe", "dt")
    cols, off = {}, 0
    for name in order:
        wd = LANES if name in ("kpe", "dt") else width[name]
        assert off % wd == 0, (name, off, wd)
        cols[name] = off
        off += wd
    n_out = off

    def pack_kernel(w_ref, o_ref):
        w = w_ref[0]
        rows = w.shape[0]
        for name in order:
            s0, c0 = int(starts[name]), cols[name]
            if name == "kpe":
                parts = [w[:, s0:s0 + kpe_w], w[:, s0 + r:s0 + kpe_w], w[:, s0:s0 + r]]
                if LANES > 2 * kpe_w:
                    parts.append(jnp.zeros((rows, LANES - 2 * kpe_w), F32))
                blk = jnp.concatenate(parts, axis=-1)
            elif name == "dt":
                blk = jnp.concatenate([w[:, s0:s0 + dt_w], jnp.zeros((rows, LANES - dt_w), F32)], axis=-1)
            else:
                blk = w[:, s0:s0 + width[name]]
            o_ref[0, :, c0:c0 + blk.shape[1]] = blk.astype(BF16)

    tk = _tile(d, 256)
    packed = pl.pallas_call(
        pack_kernel,
        grid=(nl, d // tk),
        in_specs=[pl.BlockSpec((1, tk, n_in), lambda l, k: (l, k, 0))],
        out_specs=pl.BlockSpec((1, tk, n_out), lambda l, k: (l, k, 0)),
        out_shape=jax.ShapeDtypeStruct((nl, d, n_out), BF16),
        compiler_params=_cparams(("parallel", "parallel")),
        name="pack_w_in",
    )(w_in_l)
    return packed, cols


def kernel(x, norm_mix_g, w_in, ssd_conv_w, ssd_conv_b, ssd_dt_bias, ssd_a_log, ssd_d, ssd_norm_g, s5_a_re, s5_a_im, s5_b_re, s5_b_im, s5_c_re, s5_c_im, s5_log_dt, s5_d, s5_w_glu, s5_b_glu, mla_q_norm_g, mla_w_q_up, mla_kv_norm_g, mla_w_kv_up, mla_q_gain, mla_k_gain, diff_q_gain, diff_k_gain, diff_lq1, diff_lk1, diff_lq2, diff_lk2, diff_subln_g, rel_bias, w_gate, b_gate, w_branch, w_out, norm_ffn_g, ffn_w_gate, ffn_w_up, ffn_w_down, moe_w_router, moe_b_router, moe_w_gate, moe_w_up, moe_w_down):
    bsz, seq, d = x.shape
    t = bsz * seq
    depth = w_in.shape[0]
    mix = d // 4
    n_ssd_heads = ssd_dt_bias.shape[1]
    widths = (mix, ssd_conv_w.shape[2], n_ssd_heads, mix, mla_w_q_up.shape[1], mla_w_kv_up.shape[1], MLA_ROPE,
              mix, mix, mix)
    xf = x.reshape(t, d)
    w_in_b, cols = _pack_w_in(w_in, widths)
    w_out_b = w_out.astype(BF16)
    bias_tiles = t5_bias_tiles(rel_bias, seq)
    ffn_gate_b, ffn_up_b, ffn_down_b = ffn_w_gate.astype(BF16), ffn_w_up.astype(BF16), ffn_w_down.astype(BF16)
    ys = jnp.zeros((N_BRANCH, t, mix), BF16)
    for i in range(depth):
        h = rmsnorm(xf, norm_mix_g[i])
        proj = matmul(h, [w_in_b], _epi_plain, out_dtype=F32, tn=512, w_prefix=(i,), name="in_proj")
        dt_t = jnp.swapaxes(proj[:, cols["dt"]:cols["dt"] + n_ssd_heads].reshape(bsz, seq, n_ssd_heads), 1, 2)
        ys = ssd_mixer(proj, dt_t, cols, bsz, seq, ssd_conv_w[i], ssd_conv_b[i], ssd_dt_bias[i], ssd_a_log[i],
                       ssd_d[i], ssd_norm_g[i], ys, 0)
        ys = s5_mixer(proj, cols["u"], bsz, seq, s5_a_re[i], s5_a_im[i], s5_b_re[i], s5_b_im[i], s5_c_re[i],
                      s5_c_im[i], s5_log_dt[i], s5_d[i], s5_w_glu[i], s5_b_glu[i], ys, 1)
        ys = mla_mixer(proj, cols, bsz, seq, mla_q_norm_g[i], mla_w_q_up[i], mla_kv_norm_g[i], mla_w_kv_up[i],
                       mla_q_gain[i], mla_k_gain[i], ys, 2)
        lambda_init = 0.8 - 0.6 * math.exp(-0.3 * i)
        ys = diff_mixer(proj, cols, bsz, seq, diff_q_gain[i], diff_k_gain[i], diff_lq1[i], diff_lk1[i],
                        diff_lq2[i], diff_lk2[i], diff_subln_g[i], bias_tiles, lambda_init, ys, 3)
        merged = gated_merge(h, w_gate, b_gate[i][:, None, :], ys, w_branch, i)
        xf = matmul(merged, [w_out_b], _epi_residual, extras=(xf,), extra_kinds=("mn",), tk=2048, w_prefix=(i,),
                    name="out_proj")
        e = i // 2
        if i % 2 == 0:
            h2 = rmsnorm(xf, norm_ffn_g[i])
            act = matmul(h2, [ffn_gate_b, ffn_up_b], _epi_swiglu, out_dtype=BF16, tn=512, w_prefix=(e,),
                         name="ffn_up")
            xf = matmul(act, [ffn_down_b], _epi_residual, extras=(xf,), extra_kinds=("mn",), tk=3584,
                        w_prefix=(e,), name="ffn_down")
        else:
            xf = moe_block(xf, norm_ffn_g[i], moe_w_router[e], moe_b_router[e], moe_w_gate[e], moe_w_up[e],
                           moe_w_down[e])
    return xf.reshape(bsz, seq, d)
```

```python
import functools
import math

import numpy as np
import jax
import jax.numpy as jnp
from jax import lax
from jax.experimental import pallas as pl
from jax.experimental.pallas import tpu as pltpu

F32 = jnp.float32
BF16 = jnp.bfloat16

V7X_VMEM_BYTES = 64 * 1024 * 1024
VMEM_LIMIT = V7X_VMEM_BYTES - 8 * 1024 * 1024
LANES = 128

RMS_EPS = 1e-6
SSD_HEAD_DIM = 64
SSD_N_GROUPS = 4
SSD_D_STATE = 128
SSD_CHUNK = 128
S5_GROUP = 16
S5_STATE = 64
S5_GROUPS_PER_BLOCK = 8
MLA_N_HEADS = 8
MLA_NOPE = 128
MLA_ROPE = 64
MLA_PAD = 256
ROPE_THETA = 10000.0
DIFF_N_HEADS = 8
REL_BUCKETS = 32
REL_MAX_DIST = 128
TOP_K = 2
N_BRANCH = 4
LOG2E = math.log2(math.e)


def _cparams(sem):
    return pltpu.CompilerParams(dimension_semantics=sem, vmem_limit_bytes=VMEM_LIMIT)


def _tile(n, pref):
    t = min(n, pref)
    while n % t:
        t //= 2
    return t


def _split3(a):
    hi = a.astype(BF16)
    r1 = a - hi.astype(F32)
    mid = r1.astype(BF16)
    lo = (r1 - mid.astype(F32)).astype(BF16)
    return hi, mid, lo


def _dot(a, b):
    return jnp.dot(a, b, preferred_element_type=F32)


def _dot_exact_rhs01(a, sel):
    hi, mid, lo = _split3(a)
    return _dot(hi, sel) + _dot(mid, sel) + _dot(lo, sel)


def _dot_exact_lhs01(sel, b):
    hi, mid, lo = _split3(b)
    return _dot(sel, hi) + _dot(sel, mid) + _dot(sel, lo)


def _rmsnorm_kernel(x_ref, g_ref, o_ref):
    x = x_ref[...]
    ms = jnp.mean(x * x, axis=-1, keepdims=True)
    o_ref[...] = (x * lax.rsqrt(ms + RMS_EPS) * g_ref[...]).astype(o_ref.dtype)


def rmsnorm(x, g, tm=256):
    t, d = x.shape
    tm = _tile(t, tm)
    return pl.pallas_call(
        _rmsnorm_kernel,
        grid=(t // tm,),
        in_specs=[pl.BlockSpec((tm, d), lambda i: (i, 0)), pl.BlockSpec((1, d), lambda i: (0, 0))],
        out_specs=pl.BlockSpec((tm, d), lambda i: (i, 0)),
        out_shape=jax.ShapeDtypeStruct((t, d), BF16),
        compiler_params=_cparams(("parallel",)),
        name="rmsnorm",
    )(x, g.reshape(1, d))


def _rmsnorm_router_kernel(x_ref, g_ref, wr_ref, br_ref, hp_ref, route_ref, cnt_ref, run_ref, *, n_experts):
    @pl.when(pl.program_id(0) == 0)
    def _():
        run_ref[...] = jnp.zeros(run_ref.shape, F32)

    x = x_ref[...]
    tm, d = x.shape
    ms = jnp.mean(x * x, axis=-1, keepdims=True)
    h = x * lax.rsqrt(ms + RMS_EPS) * g_ref[...]
    bits = lax.bitcast_convert_type(h.astype(BF16).astype(F32), jnp.uint32)
    hp_ref[...] = (bits[:, :d // 2] >> 16) | (bits[:, d // 2:] & jnp.uint32(0xFFFF0000))
    logits = jnp.dot(h, wr_ref[...], preferred_element_type=F32, precision=lax.Precision.HIGHEST) + br_ref[...]
    lane = lax.broadcasted_iota(jnp.int32, logits.shape, 1)
    neg = jnp.float32(-jnp.inf)
    logits = jnp.where(lane < n_experts, logits, neg)
    m1 = jnp.max(logits, axis=-1, keepdims=True)
    i1 = jnp.min(jnp.where(logits == m1, lane, LANES), axis=-1, keepdims=True)
    rest = jnp.where(lane == i1, neg, logits)
    m2 = jnp.max(rest, axis=-1, keepdims=True)
    i2 = jnp.min(jnp.where(rest == m2, lane, LANES), axis=-1, keepdims=True)
    e2 = jnp.exp(m2 - m1)
    w1 = 1.0 / (1.0 + e2)
    w2 = e2 / (1.0 + e2)
    sel = jnp.where(lane == i1, 1.0, 0.0) + jnp.where(lane == i2, 1.0, 0.0)
    row = lax.broadcasted_iota(jnp.int32, (tm, tm), 0)
    col = lax.broadcasted_iota(jnp.int32, (tm, tm), 1)
    before = jnp.where(col < row, 1.0, 0.0).astype(BF16)
    excl = _dot(before, sel.astype(BF16)) + run_ref[0:1, :]
    run_ref[...] = run_ref[...] + jnp.sum(sel, axis=0, keepdims=True)
    cnt_ref[...] = run_ref[...]
    r1 = jnp.sum(jnp.where(lane == i1, excl, 0.0), axis=-1, keepdims=True)
    r2 = jnp.sum(jnp.where(lane == i2, excl, 0.0), axis=-1, keepdims=True)
    route = jnp.zeros(logits.shape, F32)
    for pos, val in enumerate((i1.astype(F32), i2.astype(F32), r1, r2, w1, w2)):
        route = jnp.where(lane == pos, val, route)
    route_ref[...] = route


def rmsnorm_router(x, g, w_router, b_router, tm=256):
    t, d = x.shape
    ne = w_router.shape[1]
    tm = _tile(t, tm)
    wr = jnp.zeros((d, LANES), F32).at[:, :ne].set(w_router)
    br = jnp.zeros((1, LANES), F32).at[0, :ne].set(b_router)
    return pl.pallas_call(
        functools.partial(_rmsnorm_router_kernel, n_experts=ne),
        grid=(t // tm,),
        in_specs=[pl.BlockSpec((tm, d), lambda i: (i, 0)), pl.BlockSpec((1, d), lambda i: (0, 0)),
                  pl.BlockSpec((d, LANES), lambda i: (0, 0)), pl.BlockSpec((1, LANES), lambda i: (0, 0))],
        out_specs=[pl.BlockSpec((tm, d // 2), lambda i: (i, 0)), pl.BlockSpec((tm, LANES), lambda i: (i, 0)),
                   pl.BlockSpec((8, LANES), lambda i: (0, 0))],
        out_shape=[jax.ShapeDtypeStruct((t, d // 2), jnp.uint32), jax.ShapeDtypeStruct((t, LANES), F32),
                   jax.ShapeDtypeStruct((8, LANES), F32)],
        scratch_shapes=[pltpu.VMEM((8, LANES), F32)],
        compiler_params=_cparams(("arbitrary",)),
        name="rmsnorm_router",
    )(x, g.reshape(1, d), wr, br)


MOE_ROW_TILE = 512
MOE_TOKEN_BLOCK = 256


def _unpack_halves(xp):
    lo = lax.bitcast_convert_type(xp << 16, F32).astype(BF16)
    hi = lax.bitcast_convert_type(xp & jnp.uint32(0xFFFF0000), F32).astype(BF16)
    return lo, hi


def _dispatch_kernel(src_ref, h_hbm, o_ref, buf, sem, *, tmg):
    def start(r2, c):
        for p in range(2):
            r = 2 * r2 + p
            pltpu.make_async_copy(h_hbm.at[pl.ds(src_ref[0, 0, r], 1)], buf.at[pl.ds(r, 1)], sem).start(
                priority=p)
        return c

    lax.fori_loop(0, tmg // 2, start, 0)

    def wait(r, c):
        pltpu.make_async_copy(h_hbm.at[pl.ds(0, 1)], buf.at[pl.ds(0, 1)], sem).wait()
        return c

    lax.fori_loop(0, tmg, wait, 0)
    lo, hi = _unpack_halves(buf[...])
    half = lo.shape[1]
    o_ref[:, :half] = lo
    o_ref[:, half:] = hi


def moe_dispatch(hp, src, n_rows):
    dw = hp.shape[1]
    tmg = MOE_ROW_TILE
    return pl.pallas_call(
        functools.partial(_dispatch_kernel, tmg=tmg),
        grid=(n_rows // tmg,),
        in_specs=[pl.BlockSpec((1, 1, tmg), lambda i: (i, 0, 0), memory_space=pltpu.SMEM),
                  pl.BlockSpec(memory_space=pl.ANY)],
        out_specs=pl.BlockSpec((tmg, 2 * dw), lambda i: (i, 0)),
        out_shape=jax.ShapeDtypeStruct((n_rows, 2 * dw), BF16),
        scratch_shapes=[pltpu.VMEM((tmg, dw), hp.dtype), pltpu.SemaphoreType.DMA(())],
        compiler_params=_cparams(("arbitrary",)),
        name="moe_dispatch",
    )(src.reshape(n_rows // tmg, 1, tmg), hp)


def _grouped_kernel(e_tab, jw_tab, rx_tab, j_tab, r_tab, first_tab, valid_tab, x_ref, *rest, n_w, epilogue):
    del e_tab, jw_tab, rx_tab, j_tab, r_tab
    w_refs, o_ref, wb_refs = rest[:n_w], rest[n_w], rest[n_w + 1:]
    s = pl.program_id(0)

    @pl.when(first_tab[s] == 1)
    def _():
        for w, wb in zip(w_refs, wb_refs):
            wb[...] = w[0].astype(BF16)

    @pl.when(valid_tab[s] == 1)
    def _():
        x = x_ref[...]
        o_ref[...] = epilogue([_dot(x, wb[...]) for wb in wb_refs], []).astype(o_ref.dtype)

    @pl.when(valid_tab[s] == 0)
    def _():
        o_ref[...] = jnp.zeros(o_ref.shape, o_ref.dtype)


def grouped_matmul(x, ws, tabs, epilogue, out_dtype, tn, name):
    n_rows, kx = x.shape
    ne, kdim, n = ws[0].shape
    tmg = MOE_ROW_TILE
    tn = _tile(n, tn)
    n_steps = tabs[0].shape[0]
    grid_spec = pltpu.PrefetchScalarGridSpec(
        num_scalar_prefetch=7,
        grid=(n_steps,),
        in_specs=[pl.BlockSpec((tmg, kx), lambda s, e, jw, rx, j, r, f, v: (rx[s], 0))]
        + [pl.BlockSpec((1, kdim, tn), lambda s, e, jw, rx, j, r, f, v: (e[s], 0, jw[s])) for _ in ws],
        out_specs=pl.BlockSpec((tmg, tn), lambda s, e, jw, rx, j, r, f, v: (r[s], j[s])),
        scratch_shapes=[pltpu.VMEM((kdim, tn), BF16) for _ in ws],
    )
    return pl.pallas_call(
        functools.partial(_grouped_kernel, n_w=len(ws), epilogue=epilogue),
        grid_spec=grid_spec,
        out_shape=jax.ShapeDtypeStruct((n_rows, n), out_dtype),
        compiler_params=_cparams(("arbitrary",)),
        name=name,
    )(*tabs, x, *ws)


def _group_steps(padded, nj, n_tiles):
    tmg = MOE_ROW_TILE
    tiles = padded // tmg
    tile_start = jnp.cumsum(tiles) - tiles
    steps = tiles * nj
    step_end = jnp.cumsum(steps)
    step_start = step_end - steps
    total = step_end[-1]
    used_tiles = jnp.sum(tiles)
    s = jnp.arange(n_tiles * nj, dtype=jnp.int32)
    sc = jnp.minimum(s, total - 1)
    e = jnp.minimum(jnp.searchsorted(step_end, sc, side="right"), tiles.shape[0] - 1).astype(jnp.int32)
    local = sc - step_start[e]
    te = jnp.maximum(tiles[e], 1)
    jw = local // te
    rl = local % te
    rx = tile_start[e] + rl
    valid = s < total
    first = valid & (rl == 0)
    extra = jnp.maximum(s - total, 0)
    j = jnp.where(valid, jw, extra % nj)
    r = jnp.where(valid, rx, used_tiles + extra // nj)
    i32 = lambda a: a.astype(jnp.int32)
    return i32(e), i32(jw), i32(rx), i32(j), i32(r), i32(first), i32(valid)


def _combine_kernel(dest_ref, x_ref, route_ref, ys_hbm, o_ref, buf, sem, *, tb):
    def start(t, c):
        for k in range(TOP_K):
            pltpu.make_async_copy(ys_hbm.at[pl.ds(dest_ref[0, 0, TOP_K * t + k], 1)], buf.at[k, pl.ds(t, 1)],
                                  sem).start(priority=k % 2)
        return c

    lax.fori_loop(0, tb, start, 0)

    def wait(t, c):
        pltpu.make_async_copy(ys_hbm.at[pl.ds(0, 1)], buf.at[0, pl.ds(0, 1)], sem).wait()
        return c

    lax.fori_loop(0, TOP_K * tb, wait, 0)
    route = route_ref[...]
    o_ref[...] = x_ref[...] + route[:, 4:5] * buf[0] + route[:, 5:6] * buf[1]


def moe_combine(x, route, ys, dest):
    t, d = x.shape
    tb = _tile(t, MOE_TOKEN_BLOCK)
    return pl.pallas_call(
        functools.partial(_combine_kernel, tb=tb),
        grid=(t // tb,),
        in_specs=[pl.BlockSpec((1, 1, TOP_K * tb), lambda i: (i, 0, 0), memory_space=pltpu.SMEM),
                  pl.BlockSpec((tb, d), lambda i: (i, 0)), pl.BlockSpec((tb, LANES), lambda i: (i, 0)),
                  pl.BlockSpec(memory_space=pl.ANY)],
        out_specs=pl.BlockSpec((tb, d), lambda i: (i, 0)),
        out_shape=jax.ShapeDtypeStruct((t, d), F32),
        scratch_shapes=[pltpu.VMEM((TOP_K, tb, d), F32), pltpu.SemaphoreType.DMA(())],
        compiler_params=_cparams(("arbitrary",)),
        name="moe_combine",
    )(dest.reshape(t // tb, 1, TOP_K * tb), x, route, ys)


def moe_block(x, norm_g, w_router, b_router, w_gate, w_up, w_down):
    t, d = x.shape
    ne, _, f = w_gate.shape
    tmg = MOE_ROW_TILE
    hp, route, cnt = rmsnorm_router(x, norm_g, w_router, b_router)
    counts = cnt[0, :ne].astype(jnp.int32)
    padded = (counts + tmg - 1) // tmg * tmg
    starts = jnp.cumsum(padded) - padded
    ids = route[:, 0:2].astype(jnp.int32)
    dest = starts[ids] + route[:, 2:4].astype(jnp.int32)
    n_rows = t * TOP_K + ne * tmg
    n_tiles = n_rows // tmg
    src = jnp.zeros((n_rows,), jnp.int32).at[dest.reshape(-1)].set(
        jnp.repeat(jnp.arange(t, dtype=jnp.int32), TOP_K), unique_indices=True)
    xs = moe_dispatch(hp, src, n_rows)
    tn_up, tn_down = _tile(f, 512), _tile(d, 1024)
    act = grouped_matmul(xs, [w_gate, w_up], _group_steps(padded, f // tn_up, n_tiles), _epi_swiglu, BF16,
                         tn_up, "moe_up")
    ys = grouped_matmul(act, [w_down], _group_steps(padded, d // tn_down, n_tiles), _epi_plain, F32,
                        tn_down, "moe_down")
    return moe_combine(x, route, ys, dest)


def _mm_kernel(*refs, n_w, n_e, nk, epilogue):
    a_ref = refs[0]
    w_refs = refs[1:1 + n_w]
    e_refs = refs[1 + n_w:1 + n_w + n_e]
    o_ref = refs[1 + n_w + n_e]
    acc_refs = refs[2 + n_w + n_e:]
    a = a_ref[...]
    parts = [_dot(a, w[...].astype(BF16)) for w in w_refs]
    if nk == 1:
        o_ref[...] = epilogue(parts, [e[...] for e in e_refs]).astype(o_ref.dtype)
        return
    k = pl.program_id(2)

    @pl.when(k == 0)
    def _():
        for acc, p in zip(acc_refs, parts):
            acc[...] = p

    @pl.when(k > 0)
    def _():
        for acc, p in zip(acc_refs, parts):
            acc[...] += p

    @pl.when(k == nk - 1)
    def _():
        o_ref[...] = epilogue([acc[...] for acc in acc_refs], [e[...] for e in e_refs]).astype(o_ref.dtype)


def matmul(a, ws, epilogue, extras=(), extra_kinds=(), out_dtype=F32, tm=1024, tn=1024, tk=4096, w_prefix=(),
           name="matmul"):
    m, kdim = a.shape
    n = ws[0].shape[-1]
    tm, tn, tk = _tile(m, tm), _tile(n, tn), _tile(kdim, tk)
    nk = kdim // tk
    squeezed = (None,) * len(w_prefix)
    in_specs = [pl.BlockSpec((tm, tk), lambda i, j, k: (i, k))]
    in_specs += [pl.BlockSpec(squeezed + (tk, tn), lambda i, j, k: (*w_prefix, k, j)) for _ in ws]
    for kind in extra_kinds:
        if kind == "mn":
            in_specs.append(pl.BlockSpec((tm, tn), lambda i, j, k: (i, j)))
        elif kind == "m":
            in_specs.append(pl.BlockSpec((tm, LANES), lambda i, j, k: (i, 0)))
        else:
            in_specs.append(pl.BlockSpec((1, tn), lambda i, j, k: (0, j)))
    scratch = [pltpu.VMEM((tm, tn), F32) for _ in ws] if nk > 1 else []
    return pl.pallas_call(
        functools.partial(_mm_kernel, n_w=len(ws), n_e=len(extras), nk=nk, epilogue=epilogue),
        grid=(m // tm, n // tn, nk),
        in_specs=in_specs,
        out_specs=pl.BlockSpec((tm, tn), lambda i, j, k: (i, j)),
        out_shape=jax.ShapeDtypeStruct((m, n), out_dtype),
        scratch_shapes=scratch,
        compiler_params=_cparams(("parallel", "parallel", "arbitrary")),
        name=name,
    )(a, *ws, *extras)


def _epi_plain(parts, extras):
    return parts[0]


def _epi_residual(parts, extras):
    return extras[0] + parts[0]


def _epi_swiglu(parts, extras):
    g, u = parts
    return g * jax.nn.sigmoid(g) * u


def _merge_kernel(h_ref, wg_ref, b_ref, y_ref, wb_ref, o_ref, accm_ref, *accg, nk, nb):
    j = pl.program_id(2)
    k = pl.program_id(3)

    @pl.when((pl.program_id(0) == 0) & (pl.program_id(1) == 0) & (j == 0) & (k == 0))
    def _():
        accm_ref[...] = jnp.zeros(accm_ref.shape, F32)

    part = _dot(h_ref[...], wg_ref[0].astype(BF16))

    def finish(pre):
        term = jax.nn.sigmoid(pre + b_ref[0]) * _dot(y_ref[0], wb_ref[0].astype(BF16))
        total = jnp.where(j == 0, term, accm_ref[...] + term)
        accm_ref[...] = total
        o_ref[...] = total.astype(o_ref.dtype)

    if nk == 1:
        finish(part)
        return

    accg_ref, = accg

    @pl.when(k == 0)
    def _():
        accg_ref[...] = part

    @pl.when(k > 0)
    def _():
        accg_ref[...] += part

    @pl.when(k == nk - 1)
    def _():
        finish(accg_ref[...])


def gated_merge(h, w_gate, b_gate, ys, w_branch, layer, tm=1024, tn=512, tk=4096):
    t, d = h.shape
    nb, _, wdt = ys.shape
    tm, tn, tk = _tile(t, tm), _tile(d, tn), _tile(d, tk)
    nk = d // tk
    assert nb >= 2
    return pl.pallas_call(
        functools.partial(_merge_kernel, nk=nk, nb=nb),
        grid=(t // tm, d // tn, nb, nk),
        in_specs=[
            pl.BlockSpec((tm, tk), lambda i, n, j, k: (i, k)),
            pl.BlockSpec((None, 1, tk, tn), lambda i, n, j, k: (layer, j, k, n)),
            pl.BlockSpec((1, 1, tn), lambda i, n, j, k: (j, 0, n)),
            pl.BlockSpec((1, tm, wdt), lambda i, n, j, k: (j, i, 0)),
            pl.BlockSpec((None, 1, wdt, tn), lambda i, n, j, k: (layer, j, 0, n)),
        ],
        out_specs=pl.BlockSpec((tm, tn), lambda i, n, j, k: (i, n)),
        out_shape=jax.ShapeDtypeStruct((t, d), BF16),
        scratch_shapes=[pltpu.VMEM((tm, tn), F32)] * (1 if nk == 1 else 2),
        compiler_params=_cparams(("arbitrary", "arbitrary", "arbitrary", "arbitrary")),
        name="gated_merge",
    )(h, w_gate, b_gate, ys, w_branch)


def _softplus(x):
    return jnp.maximum(x, 0.0) + jnp.log1p(jnp.exp(-jnp.abs(x)))


def _ssd_kernel(xbc_ref, z_ref, dt_ref, dtt_ref, cw_ref, cb_ref, dtb_ref, dtbt_ref, alog_ref, alogt_ref,
                dskip_ref, ng_ref, expand_ref, ybuf_ref, o_ref, xe_ref, st_ref, *, n_heads, d_inner):
    del ybuf_ref
    q = SSD_CHUNK
    gn = SSD_N_GROUPS * SSD_D_STATE
    hpg = n_heads // SSD_N_GROUPS
    gw = hpg * SSD_HEAD_DIM
    c = pl.program_id(1)

    @pl.when(c == 0)
    def _():
        xe_ref[0:8, :] = jnp.zeros((8, xe_ref.shape[1]), F32)
        st_ref[...] = jnp.zeros(st_ref.shape, F32)

    xe_ref[8:8 + q, :] = xbc_ref[...]
    acc = cb_ref[...] + cw_ref[3:4, :] * xe_ref[8:8 + q, :]
    for kk in range(3):
        acc = acc + cw_ref[kk:kk + 1, :] * xe_ref[5 + kk:5 + kk + q, :]
    xe_ref[0:8, :] = xe_ref[q:q + 8, :]
    xc = acc * jax.nn.sigmoid(acc)
    xs = xc[:, :d_inner]
    bm = xc[:, d_inner:d_inner + gn].astype(BF16)
    cm = xc[:, d_inner + gn:].astype(BF16)

    dt = _softplus(dt_ref[...] + dtb_ref[...])
    da = dt * (-jnp.exp(alog_ref[...]))
    row = lax.broadcasted_iota(jnp.int32, (q, q), 0)
    col = lax.broadcasted_iota(jnp.int32, (q, q), 1)
    lower = row >= col
    tril = jnp.where(lower, 1.0, 0.0).astype(BF16)
    acs = _dot_exact_lhs01(tril, da)
    dtt = _softplus(dtt_ref[0] + dtbt_ref[...])
    dat = dtt * (-jnp.exp(alogt_ref[...]))
    triu = jnp.where(row <= col, 1.0, 0.0).astype(BF16)
    acst = _dot_exact_rhs01(dat, triu)

    expand = expand_ref[...]
    dt_x = _dot_exact_rhs01(dt, expand)
    acs_x = _dot_exact_rhs01(acs, expand)
    tot_x = acs_x[q - 1:q, :]
    xdt = xs * dt_x
    xdt_b = xdt.astype(BF16)
    xw_b = (xdt * jnp.exp(tot_x - acs_x)).astype(BF16)
    in_decay = jnp.exp(acs_x)
    chunk_decay = jnp.exp(tot_x)

    lane = lax.broadcasted_iota(jnp.int32, (1, gw), 1)
    ys = []
    for g in range(SSD_N_GROUPS):
        cg = cm[:, g * SSD_D_STATE:(g + 1) * SSD_D_STATE]
        bg = bm[:, g * SSD_D_STATE:(g + 1) * SSD_D_STATE]
        sl = slice(g * gw, (g + 1) * gw)
        gmat = lax.dot_general(cg, bg, (((1,), (1,)), ((), ())), preferred_element_type=F32)
        state = st_ref[g]
        y_g = _dot(cg, state.astype(BF16)) * in_decay[:, sl]
        xg = xdt_b[:, sl]
        for hh in range(hpg):
            hd = g * hpg + hh
            seg = acs[:, hd:hd + 1] - acst[hd:hd + 1, :]
            lmat = jnp.exp(jnp.where(lower, seg, -jnp.inf))
            mh = (gmat * lmat).astype(BF16)
            head_cols = (lane >= hh * SSD_HEAD_DIM) & (lane < (hh + 1) * SSD_HEAD_DIM)
            y_g = y_g + _dot(mh, jnp.where(head_cols, xg, jnp.zeros_like(xg)))
        upd = lax.dot_general(bg, xw_b[:, sl], (((0,), (0,)), ((), ())), preferred_element_type=F32)
        st_ref[g] = state * chunk_decay[:, sl] + upd
        ys.append(y_g)
    y = jnp.concatenate(ys, axis=-1) + xs * dskip_ref[...]
    zz = z_ref[...]
    y = y * (zz * jax.nn.sigmoid(zz))
    outs = []
    for g in range(SSD_N_GROUPS):
        yg = y[:, g * gw:(g + 1) * gw]
        ms = jnp.mean(yg * yg, axis=-1, keepdims=True)
        outs.append(yg * lax.rsqrt(ms + RMS_EPS))
    o_ref[0] = (jnp.concatenate(outs, axis=-1) * ng_ref[...]).astype(o_ref.dtype)


def ssd_mixer(proj, dt_t, cols, bsz, seq, conv_w, conv_b, dt_bias, a_log, d_skip, norm_g, ybuf, slot):
    n_heads = dt_bias.shape[0]
    d_inner = n_heads * SSD_HEAD_DIM
    conv_dim = conv_w.shape[1]
    q = SSD_CHUNK
    nc = seq // q
    pad = LANES - n_heads
    expand = jnp.repeat(jnp.eye(LANES, dtype=BF16)[:, :n_heads], SSD_HEAD_DIM, axis=1)
    dtb = jnp.pad(dt_bias, (0, pad)).reshape(1, LANES)
    alog = jnp.pad(a_log, (0, pad)).reshape(1, LANES)
    x_blk, z_blk, dt_blk = cols["xbc"] // conv_dim, cols["z"] // d_inner, cols["dt"] // LANES
    const = lambda b, c: (0, 0)
    return pl.pallas_call(
        functools.partial(_ssd_kernel, n_heads=n_heads, d_inner=d_inner),
        grid=(bsz, nc),
        in_specs=[
            pl.BlockSpec((q, conv_dim), lambda b, c: (b * nc + c, x_blk)),
            pl.BlockSpec((q, d_inner), lambda b, c: (b * nc + c, z_blk)),
            pl.BlockSpec((q, LANES), lambda b, c: (b * nc + c, dt_blk)),
            pl.BlockSpec((1, n_heads, q), lambda b, c: (b, 0, c)),
            pl.BlockSpec((4, conv_dim), const),
            pl.BlockSpec((1, conv_dim), const),
            pl.BlockSpec((1, LANES), const),
            pl.BlockSpec((n_heads, 1), const),
            pl.BlockSpec((1, LANES), const),
            pl.BlockSpec((n_heads, 1), const),
            pl.BlockSpec((1, d_inner), const),
            pl.BlockSpec((1, d_inner), const),
            pl.BlockSpec((LANES, d_inner), const),
            pl.BlockSpec(memory_space=pl.ANY),
        ],
        out_specs=pl.BlockSpec((1, q, d_inner), lambda b, c: (slot, b * nc + c, 0)),
        out_shape=jax.ShapeDtypeStruct(ybuf.shape, ybuf.dtype),
        scratch_shapes=[pltpu.VMEM((q + 8, conv_dim), F32),
                        pltpu.VMEM((SSD_N_GROUPS, SSD_D_STATE, d_inner // SSD_N_GROUPS), F32)],
        input_output_aliases={13: 0},
        compiler_params=_cparams(("parallel", "arbitrary")),
        name="ssd_mixer",
    )(proj, proj, proj, dt_t, conv_w, conv_b.reshape(1, conv_dim), dtb, dt_bias.reshape(n_heads, 1),
      alog, a_log.reshape(n_heads, 1), jnp.repeat(d_skip, SSD_HEAD_DIM).reshape(1, d_inner),
      norm_g.reshape(1, d_inner), expand, ybuf)


S5_ROWS = 8


def _s5_disc_kernel(are_ref, aim_ref, ldt_ref, brt_ref, bit_ref, pr_ref, pi_ref, bbr_ref, bbi_ref):
    dt = jnp.exp(ldt_ref[...])
    ar, ai = are_ref[...], aim_ref[...]
    mag = jnp.exp(ar * dt)
    lr, li = mag * jnp.cos(ai * dt), mag * jnp.sin(ai * dt)
    den = ar * ar + ai * ai
    fr = ((lr - 1.0) * ar + li * ai) / den
    fi = (li * ar - (lr - 1.0) * ai) / den
    pr, pi = lr, li
    for k in range(S5_ROWS):
        pr_ref[k] = pr
        pi_ref[k] = pi
        pr, pi = pr * lr - pi * li, pr * li + pi * lr
    br, bi = brt_ref[...], bit_ref[...]
    bbr_ref[...] = fr[:, None, :] * br - fi[:, None, :] * bi
    bbi_ref[...] = fr[:, None, :] * bi + fi[:, None, :] * br


def s5_discretise(a_re, a_im, log_dt, b_re, b_im):
    g, n = a_re.shape
    c = b_re.shape[2]
    brt, bit = jnp.swapaxes(b_re, 1, 2), jnp.swapaxes(b_im, 1, 2)
    return pl.pallas_call(
        _s5_disc_kernel,
        out_shape=[jax.ShapeDtypeStruct((S5_ROWS, g, n), F32), jax.ShapeDtypeStruct((S5_ROWS, g, n), F32),
                   jax.ShapeDtypeStruct((g, c, n), F32), jax.ShapeDtypeStruct((g, c, n), F32)],
        name="s5_discretise",
    )(a_re, a_im, log_dt.reshape(g, 1), brt, bit)


def _block_diag(blocks, per):
    g, r, c = blocks.shape
    nb = g // per
    eye = jnp.eye(per, dtype=blocks.dtype)
    out = blocks.reshape(nb, per, r, 1, c) * eye[None, :, None, :, None]
    return out.reshape(nb, per * r, per * c)


def _s5_kernel(u_ref, wr_ref, wi_ref, cr_ref, ci_ref, ctab_ref, d_ref, wglu_ref, bglu_ref, ybuf_ref, o_ref,
               xr_scr, xi_scr, carry_ref, y_scr, *, nblk, sw):
    del ybuf_ref
    tm = u_ref.shape[0]

    @pl.when(pl.program_id(1) == 0)
    def _():
        carry_ref[...] = jnp.zeros(carry_ref.shape, F32)

    for blk in range(nblk):
        cs = slice(blk * sw, (blk + 1) * sw)
        ls = slice(blk * LANES, (blk + 1) * LANES)
        u = u_ref[:, ls]
        ub = u.astype(BF16)
        xr_scr[...] = _dot(ub, wr_ref[blk])
        xi_scr[...] = _dot(ub, wi_ref[blk])

        def body(i, carry, cs=cs):
            c_re, c_im = carry
            r0 = pl.multiple_of(i * S5_ROWS, S5_ROWS)
            sr, si = xr_scr[pl.ds(r0, S5_ROWS), :], xi_scr[pl.ds(r0, S5_ROWS), :]
            for lvl, dist in enumerate((1, 2, 4)):
                lr, li = ctab_ref[2 * lvl, :, cs], ctab_ref[2 * lvl + 1, :, cs]
                tr, ti = pltpu.roll(sr, dist, axis=0), pltpu.roll(si, dist, axis=0)
                sr, si = sr + lr * tr - li * ti, si + lr * ti + li * tr
            pr, pi = ctab_ref[6, :, cs], ctab_ref[7, :, cs]
            sr, si = sr + pr * c_re - pi * c_im, si + pr * c_im + pi * c_re
            xr_scr[pl.ds(r0, S5_ROWS), :] = sr
            xi_scr[pl.ds(r0, S5_ROWS), :] = si
            return sr[S5_ROWS - 1:S5_ROWS, :], si[S5_ROWS - 1:S5_ROWS, :]

        c_re, c_im = lax.fori_loop(0, tm // S5_ROWS, body, (carry_ref[0:1, cs], carry_ref[1:2, cs]), unroll=2)
        carry_ref[0:1, cs] = c_re
        carry_ref[1:2, cs] = c_im
        y = _dot(xr_scr[...].astype(BF16), cr_ref[blk]) - _dot(xi_scr[...].astype(BF16), ci_ref[blk])
        y_scr[:, ls] = jax.nn.gelu(y + d_ref[:, ls] * u)
    y = y_scr[...]
    o_ref[0] = (y * jax.nn.sigmoid(_dot(y.astype(BF16), wglu_ref[...]) + bglu_ref[...])).astype(o_ref.dtype)


def s5_mixer(proj, u_col, bsz, seq, a_re, a_im, b_re, b_im, c_re, c_im, log_dt, d_skip, w_glu, b_glu,
             ybuf, slot):
    t = bsz * seq
    g, n = a_re.shape
    width = g * S5_GROUP
    per = S5_GROUPS_PER_BLOCK
    nblk = g // per
    sw = per * n
    ns = g * n
    pw_r, pw_i, bbr, bbi = s5_discretise(a_re, a_im, log_dt, b_re, b_im)
    w_r = _block_diag(bbr, per).astype(BF16)
    w_i = _block_diag(bbi, per).astype(BF16)
    c_r = _block_diag(jnp.swapaxes(c_re, 1, 2), per).astype(BF16)
    c_i = _block_diag(jnp.swapaxes(c_im, 1, 2), per).astype(BF16)
    pw_r, pw_i = pw_r.reshape(S5_ROWS, ns), pw_i.reshape(S5_ROWS, ns)
    sub = jnp.arange(S5_ROWS)[:, None]
    kinds = []
    for dist in (1, 2, 4):
        kinds += [jnp.where(sub >= dist, pw_r[dist - 1][None, :], 0.0),
                  jnp.where(sub >= dist, pw_i[dist - 1][None, :], 0.0)]
    ctab = jnp.stack(kinds + [pw_r, pw_i])
    tm = _tile(seq, 256)
    nt = seq // tm
    const2 = lambda b, c: (0, 0)
    const3 = lambda b, c: (0, 0, 0)
    return pl.pallas_call(
        functools.partial(_s5_kernel, nblk=nblk, sw=sw),
        grid=(bsz, nt),
        in_specs=[pl.BlockSpec((tm, width), lambda b, c: (b * nt + c, u_col // width)),
                  pl.BlockSpec((nblk, LANES, sw), const3), pl.BlockSpec((nblk, LANES, sw), const3),
                  pl.BlockSpec((nblk, sw, LANES), const3), pl.BlockSpec((nblk, sw, LANES), const3),
                  pl.BlockSpec((8, S5_ROWS, ns), const3),
                  pl.BlockSpec((1, width), const2), pl.BlockSpec((width, width), const2),
                  pl.BlockSpec((1, width), const2), pl.BlockSpec(memory_space=pl.ANY)],
        out_specs=pl.BlockSpec((1, tm, width), lambda b, c: (slot, b * nt + c, 0)),
        out_shape=jax.ShapeDtypeStruct(ybuf.shape, ybuf.dtype),
        scratch_shapes=[pltpu.VMEM((tm, sw), F32), pltpu.VMEM((tm, sw), F32), pltpu.VMEM((8, ns), F32),
                        pltpu.VMEM((tm, width), F32)],
        input_output_aliases={9: 0},
        compiler_params=_cparams(("parallel", "arbitrary")),
        name="s5_mixer",
    )(proj, w_r, w_i, c_r, c_i, ctab, d_skip.reshape(1, width), w_glu.astype(BF16), b_glu.reshape(1, width),
      ybuf)


def _flash_kernel(*refs, tq, n_maps, has_bias, finalize):
    if has_bias:
        q_ref, k_ref, v_ref, bias_ref, far_ref, *rest = refs
    else:
        q_ref, k_ref, v_ref, *rest = refs
    fin_refs = rest[:-2]
    o_ref = rest[-1]
    h = pl.program_id(1)
    qi = pl.program_id(2)
    rows = n_maps * tq
    dvo = v_ref.shape[1]
    qv = q_ref[...].reshape(rows, q_ref.shape[-1])

    def tiled(b):
        return jnp.concatenate([b] * n_maps, axis=0) if n_maps > 1 else b

    def block(kb, carry, add, shift):
        m, l, acc = carry
        off = pl.multiple_of(kb * tq, tq)
        s = lax.dot_general(qv, k_ref[pl.ds(off, tq), :], (((1,), (1,)), ((), ())), preferred_element_type=F32)
        sb = s.astype(BF16)
        if add is not None:
            sb = sb + add
        mx = jnp.max(sb, axis=-1, keepdims=True).astype(F32)
        m_hi = jnp.maximum(m, mx if shift is None else mx + shift)
        ref_b = (m_hi if shift is None else m_hi - shift).astype(BF16)
        m_new = ref_b.astype(F32) if shift is None else ref_b.astype(F32) + shift
        alpha = jnp.exp2(m - m_new)
        p = jnp.exp2(sb - ref_b)
        l_new = alpha * l + jnp.sum(p.astype(F32), axis=-1, keepdims=True)
        return m_new, l_new, alpha * acc + _dot(p, v_ref[pl.ds(off, tq), :])

    carry = (jnp.full((rows, 1), -jnp.inf, F32), jnp.zeros((rows, 1), F32), jnp.zeros((rows, dvo), F32))
    if has_bias:
        far = far_ref[h]
        n_far = jnp.maximum(qi - 1, 0)
        carry = lax.fori_loop(0, n_far, lambda kb, c: block(kb, c, None, far), carry)
        carry = lax.fori_loop(n_far, qi, lambda kb, c: block(kb, c, tiled(bias_ref[0, 1]), None), carry)
    else:
        carry = lax.fori_loop(0, qi, lambda kb, c: block(kb, c, None, None), carry)
    row = lax.broadcasted_iota(jnp.int32, (tq, tq), 0)
    col = lax.broadcasted_iota(jnp.int32, (tq, tq), 1)
    diag = jnp.where(row >= col, bias_ref[0, 0].astype(F32) if has_bias else 0.0, -jnp.inf).astype(BF16)
    m, l, acc = block(qi, carry, tiled(diag), None)
    o_ref[0] = finalize(acc / l, [r[...] for r in fin_refs]).astype(o_ref.dtype)


def flash_attention(q, k, v, bsz, seq, n_heads, dq, ybuf, slot, n_maps=1, bias=None, far=None, fin=(),
                    finalize=None, tq=512):
    dv = v.shape[1] // n_heads
    tq = _tile(seq, tq)
    nq = seq // tq
    has_bias = bias is not None
    if finalize is None:
        finalize = lambda o, extras: o
    in_specs = [pl.BlockSpec((n_maps, tq, dq), lambda b, h, i: (0, b * nq + i, h)),
                pl.BlockSpec((seq, dq), lambda b, h, i: (b, h)),
                pl.BlockSpec((seq, dv), lambda b, h, i: (b, h))]
    args = [q, k, v]
    if has_bias:
        in_specs += [pl.BlockSpec((1, 2, tq, tq), lambda b, h, i: (h, 0, 0, 0)),
                     pl.BlockSpec(memory_space=pltpu.SMEM)]
        args += [bias, far]
    for f in fin:
        in_specs.append(pl.BlockSpec(f.shape, lambda b, h, i: (0, 0)))
        args.append(f)
    in_specs.append(pl.BlockSpec(memory_space=pl.ANY))
    args.append(ybuf)
    return pl.pallas_call(
        functools.partial(_flash_kernel, tq=tq, n_maps=n_maps, has_bias=has_bias, finalize=finalize),
        grid=(bsz, n_heads, nq),
        in_specs=in_specs,
        out_specs=pl.BlockSpec((1, tq, dv), lambda b, h, i: (slot, b * nq + i, h)),
        out_shape=jax.ShapeDtypeStruct(ybuf.shape, ybuf.dtype),
        input_output_aliases={len(args) - 1: 0},
        compiler_params=_cparams(("parallel", "parallel", "arbitrary")),
        name="flash_bias" if has_bias else "flash",
    )(*args)


def _mla_proj_kernel(cq_ref, ckv_ref, kpe_ref, cos_ref, sin_ref, qng_ref, kvng_ref, wq_ref, wkv_ref,
                     qg_ref, kg_ref, q_ref, k_ref, v_ref, *, n_heads, scale):
    half = LANES // 2

    def norm(x, g):
        return (x * lax.rsqrt(jnp.mean(x * x, axis=-1, keepdims=True) + RMS_EPS) * g).astype(BF16)

    cq = norm(cq_ref[...], qng_ref[...])
    ckv = norm(ckv_ref[...], kvng_ref[...])
    cos, sin = cos_ref[...], sin_ref[...]
    kpe = kpe_ref[...]
    kpe_ss = jnp.sum(jnp.where(lax.broadcasted_iota(jnp.int32, kpe.shape, 1) < half, kpe * kpe, 0.0),
                     axis=-1, keepdims=True)
    n_qk = float(MLA_NOPE + MLA_ROPE)
    qg, kg = qg_ref[...], kg_ref[...]

    def rotary(hi, gains, inv):
        return inv * (hi * (gains[1:2] * cos) + pltpu.roll(hi, half, axis=1) * (gains[2:3] * sin))

    k_rot_raw = kpe * (kg[1:2] * cos) + pltpu.roll(kpe, half, axis=1) * (kg[2:3] * sin)
    for h in range(n_heads):
        xq = _dot(cq, wq_ref[:, h * MLA_PAD:(h + 1) * MLA_PAD])
        lo, hi = xq[:, :LANES], xq[:, LANES:]
        ss = jnp.sum(lo * lo, axis=-1, keepdims=True) + 0.5 * jnp.sum(hi * hi, axis=-1, keepdims=True)
        inv = lax.rsqrt(ss / n_qk + RMS_EPS) * scale
        q_ref[:, h * MLA_PAD:h * MLA_PAD + LANES] = (lo * inv * qg[0:1]).astype(BF16)
        q_ref[:, h * MLA_PAD + LANES:(h + 1) * MLA_PAD] = rotary(hi, qg, inv).astype(BF16)
        xkv = _dot(ckv, wkv_ref[:, h * MLA_PAD:(h + 1) * MLA_PAD])
        kn = xkv[:, :LANES]
        inv_k = lax.rsqrt((jnp.sum(kn * kn, axis=-1, keepdims=True) + kpe_ss) / n_qk + RMS_EPS)
        k_ref[:, h * MLA_PAD:h * MLA_PAD + LANES] = (kn * inv_k * kg[0:1]).astype(BF16)
        k_ref[:, h * MLA_PAD + LANES:(h + 1) * MLA_PAD] = (k_rot_raw * inv_k).astype(BF16)
        v_ref[:, h * LANES:(h + 1) * LANES] = xkv[:, LANES:].astype(BF16)


def _rope_gain_rows(gain):
    rot = gain[MLA_NOPE:]
    r = MLA_ROPE // 2
    swapped = jnp.concatenate([rot[r:], rot[:r]])
    z = jnp.zeros((LANES - MLA_ROPE,), F32)
    return jnp.stack([gain[:MLA_NOPE], jnp.concatenate([rot, z]), jnp.concatenate([swapped, z])])


def mla_mixer(proj, cols, bsz, seq, q_norm_g, w_q_up, kv_norm_g, w_kv_up, q_gain, k_gain, ybuf, slot):
    t = bsz * seq
    hds = MLA_N_HEADS
    q_rank, kv_rank = w_q_up.shape[0], w_kv_up.shape[0]
    r = MLA_ROPE // 2
    wq = w_q_up.reshape(q_rank, hds, MLA_NOPE + MLA_ROPE)
    wq = jnp.concatenate([wq, wq[:, :, MLA_NOPE + r:], wq[:, :, MLA_NOPE:MLA_NOPE + r]], axis=-1)
    wq = wq.reshape(q_rank, hds * MLA_PAD).astype(BF16)
    wkv = w_kv_up.astype(BF16)
    inv = 1.0 / (ROPE_THETA ** (jnp.arange(0, MLA_ROPE, 2, dtype=F32) / MLA_ROPE))
    ang = jnp.arange(seq, dtype=F32)[:, None] * inv[None, :]
    zer = jnp.zeros((seq, LANES - MLA_ROPE), F32)
    cos_t = jnp.concatenate([jnp.cos(ang), jnp.cos(ang), zer], axis=-1)
    sin_t = jnp.concatenate([-jnp.sin(ang), jnp.sin(ang), zer], axis=-1)
    tm = _tile(seq, 512)
    npos = seq // tm
    const = lambda i: (0, 0)
    q, k, v = pl.pallas_call(
        functools.partial(_mla_proj_kernel, n_heads=hds, scale=LOG2E * float(MLA_NOPE + MLA_ROPE) ** -0.5),
        grid=(t // tm,),
        in_specs=[pl.BlockSpec((tm, q_rank), lambda i: (i, cols["cq"] // q_rank)),
                  pl.BlockSpec((tm, kv_rank), lambda i: (i, cols["ckv"] // kv_rank)),
                  pl.BlockSpec((tm, LANES), lambda i: (i, cols["kpe"] // LANES)),
                  pl.BlockSpec((tm, LANES), lambda i: (i % npos, 0)),
                  pl.BlockSpec((tm, LANES), lambda i: (i % npos, 0)),
                  pl.BlockSpec((1, q_rank), const), pl.BlockSpec((1, kv_rank), const),
                  pl.BlockSpec((q_rank, hds * MLA_PAD), const), pl.BlockSpec((kv_rank, hds * MLA_PAD), const),
                  pl.BlockSpec((3, LANES), const), pl.BlockSpec((3, LANES), const)],
        out_specs=[pl.BlockSpec((tm, hds * MLA_PAD), lambda i: (i, 0)),
                   pl.BlockSpec((tm, hds * MLA_PAD), lambda i: (i, 0)),
                   pl.BlockSpec((tm, hds * LANES), lambda i: (i, 0))],
        out_shape=[jax.ShapeDtypeStruct((t, hds * MLA_PAD), BF16), jax.ShapeDtypeStruct((t, hds * MLA_PAD), BF16),
                   jax.ShapeDtypeStruct((t, hds * LANES), BF16)],
        compiler_params=_cparams(("parallel",)),
        name="mla_proj",
    )(proj, proj, proj, cos_t, sin_t, q_norm_g.reshape(1, q_rank), kv_norm_g.reshape(1, kv_rank), wq, wkv,
      _rope_gain_rows(q_gain), _rope_gain_rows(k_gain))
    return flash_attention(q.reshape(1, t, hds * MLA_PAD), k, v, bsz, seq, hds, MLA_PAD, ybuf, slot, tq=1024)


def _diff_prep_kernel(q_ref, k_ref, v_ref, qg_ref, kg_ref, seg_ref, qo_ref, ko_ref, vo_ref, *, hd, scale):
    seg = seg_ref[...]
    lane = lax.broadcasted_iota(jnp.int32, (1, LANES), 1)
    n_col = q_ref.shape[1] // LANES
    for c in range(n_col):
        sl = slice(c * LANES, (c + 1) * LANES)
        xq, xk = q_ref[:, sl], k_ref[:, sl]
        ssq = _dot_exact_rhs01(xq * xq, seg)
        ssk = _dot_exact_rhs01(xk * xk, seg)
        qn = xq * lax.rsqrt(ssq / hd + RMS_EPS) * (qg_ref[...] * scale)
        kn = xk * lax.rsqrt(ssk / hd + RMS_EPS) * kg_ref[...]
        qo_ref[0, :, sl] = jnp.where(lane < hd, qn, 0.0).astype(BF16)
        qo_ref[1, :, sl] = jnp.where(lane >= hd, qn, 0.0).astype(BF16)
        ko_ref[:, sl] = kn.astype(BF16)
    vo_ref[...] = v_ref[...].astype(BF16)


def _t5_bucket_np(n):
    max_exact = REL_BUCKETS // 2
    nf = np.maximum(n, 1).astype(np.float32)
    large = max_exact + (np.log(nf / np.float32(max_exact)) / np.float32(math.log(REL_MAX_DIST / max_exact))
                         * np.float32(REL_BUCKETS - max_exact)).astype(np.int32)
    large = np.minimum(large, REL_BUCKETS - 1)
    return np.where(n < max_exact, n, large).astype(np.int32)


def _bias_tiles_kernel(bkt_ref, rb_ref, o_ref):
    h = pl.program_id(0)
    for d in range(2):
        bkt = bkt_ref[d]
        tile = jnp.zeros(bkt.shape, F32)
        for b in range(REL_BUCKETS):
            tile = jnp.where(bkt == b, rb_ref[b, h] * LOG2E, tile)
        o_ref[0, d] = tile.astype(o_ref.dtype)


def _diff_finalize(lambda_init, tq):
    def fin(o, extras):
        lq1, lk1, lq2, lk2, sub_g = extras
        lam = (jnp.exp(jnp.sum(lq1 * lk1, axis=-1, keepdims=True))
               - jnp.exp(jnp.sum(lq2 * lk2, axis=-1, keepdims=True)) + lambda_init)
        d = o[:tq] - lam * o[tq:]
        ms = jnp.mean(d * d, axis=-1, keepdims=True)
        return d * lax.rsqrt(ms + RMS_EPS) * (sub_g * (1.0 - lambda_init))
    return fin


def t5_bias_tiles(rel_bias, seq):
    hds = rel_bias.shape[1]
    tq = _tile(seq, 512)
    assert tq + 1 > REL_MAX_DIST, "blocks two or more before the diagonal must all be at the far distance"
    idx = np.arange(tq)
    dist0 = np.maximum(idx[:, None] - idx[None, :], 0)
    dist1 = idx[:, None] - idx[None, :] + tq
    bkt = jnp.asarray(np.stack([_t5_bucket_np(dist0), _t5_bucket_np(dist1)]))
    bias = pl.pallas_call(
        _bias_tiles_kernel,
        grid=(hds,),
        in_specs=[pl.BlockSpec((2, tq, tq), lambda h: (0, 0, 0)), pl.BlockSpec(memory_space=pltpu.SMEM)],
        out_specs=pl.BlockSpec((1, 2, tq, tq), lambda h: (h, 0, 0, 0)),
        out_shape=jax.ShapeDtypeStruct((hds, 2, tq, tq), BF16),
        compiler_params=_cparams(("parallel",)),
        name="t5_bias_tiles",
    )(bkt, rel_bias)
    return bias, rel_bias[REL_BUCKETS - 1] * LOG2E, tq


def diff_mixer(proj, cols, bsz, seq, q_gain, k_gain, lq1, lk1, lq2, lk2, subln_g, bias_tiles, lambda_init,
               ybuf, slot):
    t = bsz * seq
    hds = DIFF_N_HEADS
    hd = q_gain.shape[0]
    width = hds * 2 * hd
    tm = _tile(t, 512)
    seg = (np.arange(LANES)[:, None] // hd == np.arange(LANES)[None, :] // hd)
    seg = jnp.asarray(seg, BF16)
    gq, gk = jnp.tile(q_gain, LANES // hd).reshape(1, LANES), jnp.tile(k_gain, LANES // hd).reshape(1, LANES)
    const = lambda i: (0, 0)
    qd, kd, vd = pl.pallas_call(
        functools.partial(_diff_prep_kernel, hd=hd, scale=LOG2E * float(hd) ** -0.5),
        grid=(t // tm,),
        in_specs=[pl.BlockSpec((tm, width), lambda i: (i, cols["dq"] // width)),
                  pl.BlockSpec((tm, width), lambda i: (i, cols["dk"] // width)),
                  pl.BlockSpec((tm, width), lambda i: (i, cols["dv"] // width)),
                  pl.BlockSpec((1, LANES), const), pl.BlockSpec((1, LANES), const),
                  pl.BlockSpec((LANES, LANES), const)],
        out_specs=[pl.BlockSpec((2, tm, width), lambda i: (0, i, 0)),
                   pl.BlockSpec((tm, width), lambda i: (i, 0)), pl.BlockSpec((tm, width), lambda i: (i, 0))],
        out_shape=[jax.ShapeDtypeStruct((2, t, width), BF16), jax.ShapeDtypeStruct((t, width), BF16),
                   jax.ShapeDtypeStruct((t, width), BF16)],
        compiler_params=_cparams(("parallel",)),
        name="diff_prep",
    )(proj, proj, proj, gq, gk, seg)

    bias, far, tq = bias_tiles
    fin = tuple(a.reshape(1, -1) for a in (lq1, lk1, lq2, lk2, subln_g))
    return flash_attention(qd, kd, vd, bsz, seq, hds, 2 * hd, ybuf, slot, n_maps=2, bias=bias, far=far, fin=fin,
                           finalize=_diff_finalize(lambda_init, tq), tq=tq)


def _pack_w_in(w_in_l, widths):
    names = ("z", "xbc", "dt", "u", "cq", "ckv", "kpe", "dq", "dk", "dv")
    starts = dict(zip(names, np.cumsum([0] + list(widths))[:-1]))
    width = dict(zip(names, widths))
    nl, d, n_in = w_in_l.shape
    kpe_w, dt_w, r = width["kpe"], width["dt"], width["kpe"] // 2
    order = ("xbc", "z", "u", "dq", "dk", "dv", "ckv", "cq", "kpe", "dt")
    cols, off = {}, 0
    for name in order:
        wd = LANES if name in ("kpe", "dt") else width[name]
        assert off % wd == 0, (name, off, wd)
        cols[name] = off
        off += wd
    n_out = off

    def pack_kernel(w_ref, o_ref):
        w = w_ref[0]
        rows = w.shape[0]
        for name in order:
            s0, c0 = int(starts[name]), cols[name]
            if name == "kpe":
                parts = [w[:, s0:s0 + kpe_w], w[:, s0 + r:s0 + kpe_w], w[:, s0:s0 + r]]
                if LANES > 2 * kpe_w:
                    parts.append(jnp.zeros((rows, LANES - 2 * kpe_w), F32))
                blk = jnp.concatenate(parts, axis=-1)
            elif name == "dt":
                blk = jnp.concatenate([w[:, s0:s0 + dt_w], jnp.zeros((rows, LANES - dt_w), F32)], axis=-1)
            else:
                blk = w[:, s0:s0 + width[name]]
            o_ref[0, :, c0:c0 + blk.shape[1]] = blk.astype(BF16)

    tk = _tile(d, 256)
    packed = pl.pallas_call(
        pack_kernel,
        grid=(nl, d // tk),
        in_specs=[pl.BlockSpec((1, tk, n_in), lambda l, k: (l, k, 0))],
        out_specs=pl.BlockSpec((1, tk, n_out), lambda l, k: (l, k, 0)),
        out_shape=jax.ShapeDtypeStruct((nl, d, n_out), BF16),
        compiler_params=_cparams(("parallel", "parallel")),
        name="pack_w_in",
    )(w_in_l)
    return packed, cols


def kernel(x, norm_mix_g, w_in, ssd_conv_w, ssd_conv_b, ssd_dt_bias, ssd_a_log, ssd_d, ssd_norm_g, s5_a_re, s5_a_im, s5_b_re, s5_b_im, s5_c_re, s5_c_im, s5_log_dt, s5_d, s5_w_glu, s5_b_glu, mla_q_norm_g, mla_w_q_up, mla_kv_norm_g, mla_w_kv_up, mla_q_gain, mla_k_gain, diff_q_gain, diff_k_gain, diff_lq1, diff_lk1, diff_lq2, diff_lk2, diff_subln_g, rel_bias, w_gate, b_gate, w_branch, w_out, norm_ffn_g, ffn_w_gate, ffn_w_up, ffn_w_down, moe_w_router, moe_b_router, moe_w_gate, moe_w_up, moe_w_down):
    bsz, seq, d = x.shape
    t = bsz * seq
    depth = w_in.shape[0]
    mix = d // 4
    n_ssd_heads = ssd_dt_bias.shape[1]
    widths = (mix, ssd_conv_w.shape[2], n_ssd_heads, mix, mla_w_q_up.shape[1], mla_w_kv_up.shape[1], MLA_ROPE,
              mix, mix, mix)
    xf = x.reshape(t, d)
    w_in_b, cols = _pack_w_in(w_in, widths)
    w_out_b = w_out.astype(BF16)
    bias_tiles = t5_bias_tiles(rel_bias, seq)
    ys = jnp.zeros((N_BRANCH, t, mix), BF16)
    for i in range(depth):
        h = rmsnorm(xf, norm_mix_g[i])
        proj = matmul(h, [w_in_b], _epi_plain, out_dtype=F32, tn=512, w_prefix=(i,), name="in_proj")
        dt_t = jnp.swapaxes(proj[:, cols["dt"]:cols["dt"] + n_ssd_heads].reshape(bsz, seq, n_ssd_heads), 1, 2)
        ys = ssd_mixer(proj, dt_t, cols, bsz, seq, ssd_conv_w[i], ssd_conv_b[i], ssd_dt_bias[i], ssd_a_log[i],
                       ssd_d[i], ssd_norm_g[i], ys, 0)
        ys = s5_mixer(proj, cols["u"], bsz, seq, s5_a_re[i], s5_a_im[i], s5_b_re[i], s5_b_im[i], s5_c_re[i],
                      s5_c_im[i], s5_log_dt[i], s5_d[i], s5_w_glu[i], s5_b_glu[i], ys, 1)
        ys = mla_mixer(proj, cols, bsz, seq, mla_q_norm_g[i], mla_w_q_up[i], mla_kv_norm_g[i], mla_w_kv_up[i],
                       mla_q_gain[i], mla_k_gain[i], ys, 2)
        lambda_init = 0.8 - 0.6 * math.exp(-0.3 * i)
        ys = diff_mixer(proj, cols, bsz, seq, diff_q_gain[i], diff_k_gain[i], diff_lq1[i], diff_lk1[i],
                        diff_lq2[i], diff_lk2[i], diff_subln_g[i], bias_tiles, lambda_init, ys, 3)
        merged = gated_merge(h, w_gate, b_gate[i][:, None, :], ys, w_branch, i)
        xf = matmul(merged, [w_out_b], _epi_residual, extras=(xf,), extra_kinds=("mn",), tk=2048, w_prefix=(i,),
                    name="out_proj")
        e = i // 2
        if i % 2 == 0:
            h2 = rmsnorm(xf, norm_ffn_g[i])
            act = matmul(h2, [ffn_w_gate, ffn_w_up], _epi_swiglu, out_dtype=BF16, tn=256, w_prefix=(e,),
                         name="ffn_up")
            xf = matmul(act, [ffn_w_down], _epi_residual, extras=(xf,), extra_kinds=("mn",), tn=512, tk=3584,
                        w_prefix=(e,), name="ffn_down")
        else:
            xf = moe_block(xf, norm_ffn_g[i], moe_w_router[e], moe_b_router[e], moe_w_gate[e], moe_w_up[e],
                           moe_w_down[e])
    return xf.reshape(bsz, seq, d)
```

```python
import functools
import math

import numpy as np
import jax
import jax.numpy as jnp
from jax import lax
from jax.experimental import pallas as pl
from jax.experimental.pallas import tpu as pltpu

F32 = jnp.float32
BF16 = jnp.bfloat16

V7X_VMEM_BYTES = 64 * 1024 * 1024
VMEM_LIMIT = V7X_VMEM_BYTES - 8 * 1024 * 1024
LANES = 128

RMS_EPS = 1e-6
SSD_HEAD_DIM = 64
SSD_N_GROUPS = 4
SSD_D_STATE = 128
SSD_CHUNK = 128
S5_GROUP = 16
S5_STATE = 64
S5_GROUPS_PER_BLOCK = 8
MLA_N_HEADS = 8
MLA_NOPE = 128
MLA_ROPE = 64
MLA_PAD = 256
ROPE_THETA = 10000.0
DIFF_N_HEADS = 8
REL_BUCKETS = 32
REL_MAX_DIST = 128
TOP_K = 2
N_BRANCH = 4
LOG2E = math.log2(math.e)


def _cparams(sem):
    return pltpu.CompilerParams(dimension_semantics=sem, vmem_limit_bytes=VMEM_LIMIT)


def _tile(n, pref):
    t = min(n, pref)
    while n % t:
        t //= 2
    return t


def _split3(a):
    hi = a.astype(BF16)
    r1 = a - hi.astype(F32)
    mid = r1.astype(BF16)
    lo = (r1 - mid.astype(F32)).astype(BF16)
    return hi, mid, lo


def _dot(a, b):
    return jnp.dot(a, b, preferred_element_type=F32)


def _dot_exact_rhs01(a, sel):
    hi, mid, lo = _split3(a)
    return _dot(hi, sel) + _dot(mid, sel) + _dot(lo, sel)


def _dot_exact_lhs01(sel, b):
    hi, mid, lo = _split3(b)
    return _dot(sel, hi) + _dot(sel, mid) + _dot(sel, lo)


def _rmsnorm_kernel(x_ref, g_ref, o_ref):
    x = x_ref[...]
    ms = jnp.mean(x * x, axis=-1, keepdims=True)
    o_ref[...] = (x * lax.rsqrt(ms + RMS_EPS) * g_ref[...]).astype(o_ref.dtype)


def rmsnorm(x, g, tm=256):
    t, d = x.shape
    tm = _tile(t, tm)
    return pl.pallas_call(
        _rmsnorm_kernel,
        grid=(t // tm,),
        in_specs=[pl.BlockSpec((tm, d), lambda i: (i, 0)), pl.BlockSpec((1, d), lambda i: (0, 0))],
        out_specs=pl.BlockSpec((tm, d), lambda i: (i, 0)),
        out_shape=jax.ShapeDtypeStruct((t, d), BF16),
        compiler_params=_cparams(("parallel",)),
        name="rmsnorm",
    )(x, g.reshape(1, d))


def _rmsnorm_router_kernel(x_ref, g_ref, wr_ref, br_ref, hp_ref, route_ref, cnt_ref, run_ref, *, n_experts):
    @pl.when(pl.program_id(0) == 0)
    def _():
        run_ref[...] = jnp.zeros(run_ref.shape, F32)

    x = x_ref[...]
    tm, d = x.shape
    ms = jnp.mean(x * x, axis=-1, keepdims=True)
    h = x * lax.rsqrt(ms + RMS_EPS) * g_ref[...]
    bits = lax.bitcast_convert_type(h.astype(BF16).astype(F32), jnp.uint32)
    hp_ref[...] = (bits[:, :d // 2] >> 16) | (bits[:, d // 2:] & jnp.uint32(0xFFFF0000))
    logits = jnp.dot(h, wr_ref[...], preferred_element_type=F32, precision=lax.Precision.HIGHEST) + br_ref[...]
    lane = lax.broadcasted_iota(jnp.int32, logits.shape, 1)
    neg = jnp.float32(-jnp.inf)
    logits = jnp.where(lane < n_experts, logits, neg)
    m1 = jnp.max(logits, axis=-1, keepdims=True)
    i1 = jnp.min(jnp.where(logits == m1, lane, LANES), axis=-1, keepdims=True)
    rest = jnp.where(lane == i1, neg, logits)
    m2 = jnp.max(rest, axis=-1, keepdims=True)
    i2 = jnp.min(jnp.where(rest == m2, lane, LANES), axis=-1, keepdims=True)
    e2 = jnp.exp(m2 - m1)
    w1 = 1.0 / (1.0 + e2)
    w2 = e2 / (1.0 + e2)
    sel = jnp.where(lane == i1, 1.0, 0.0) + jnp.where(lane == i2, 1.0, 0.0)
    row = lax.broadcasted_iota(jnp.int32, (tm, tm), 0)
    col = lax.broadcasted_iota(jnp.int32, (tm, tm), 1)
    before = jnp.where(col < row, 1.0, 0.0).astype(BF16)
    excl = _dot(before, sel.astype(BF16)) + run_ref[0:1, :]
    run_ref[...] = run_ref[...] + jnp.sum(sel, axis=0, keepdims=True)
    cnt_ref[...] = run_ref[...]
    r1 = jnp.sum(jnp.where(lane == i1, excl, 0.0), axis=-1, keepdims=True)
    r2 = jnp.sum(jnp.where(lane == i2, excl, 0.0), axis=-1, keepdims=True)
    route = jnp.zeros(logits.shape, F32)
    for pos, val in enumerate((i1.astype(F32), i2.astype(F32), r1, r2, w1, w2)):
        route = jnp.where(lane == pos, val, route)
    route_ref[...] = route


def rmsnorm_router(x, g, w_router, b_router, tm=256):
    t, d = x.shape
    ne = w_router.shape[1]
    tm = _tile(t, tm)
    wr = jnp.zeros((d, LANES), F32).at[:, :ne].set(w_router)
    br = jnp.zeros((1, LANES), F32).at[0, :ne].set(b_router)
    return pl.pallas_call(
        functools.partial(_rmsnorm_router_kernel, n_experts=ne),
        grid=(t // tm,),
        in_specs=[pl.BlockSpec((tm, d), lambda i: (i, 0)), pl.BlockSpec((1, d), lambda i: (0, 0)),
                  pl.BlockSpec((d, LANES), lambda i: (0, 0)), pl.BlockSpec((1, LANES), lambda i: (0, 0))],
        out_specs=[pl.BlockSpec((tm, d // 2), lambda i: (i, 0)), pl.BlockSpec((tm, LANES), lambda i: (i, 0)),
                   pl.BlockSpec((8, LANES), lambda i: (0, 0))],
        out_shape=[jax.ShapeDtypeStruct((t, d // 2), jnp.uint32), jax.ShapeDtypeStruct((t, LANES), F32),
                   jax.ShapeDtypeStruct((8, LANES), F32)],
        scratch_shapes=[pltpu.VMEM((8, LANES), F32)],
        compiler_params=_cparams(("arbitrary",)),
        name="rmsnorm_router",
    )(x, g.reshape(1, d), wr, br)


MOE_ROW_TILE = 512
MOE_TOKEN_BLOCK = 256


def _unpack_halves(xp):
    lo = lax.bitcast_convert_type(xp << 16, F32).astype(BF16)
    hi = lax.bitcast_convert_type(xp & jnp.uint32(0xFFFF0000), F32).astype(BF16)
    return lo, hi


def _dispatch_kernel(src_ref, h_hbm, o_ref, buf, sem, *, tmg):
    def start(r2, c):
        for p in range(2):
            r = 2 * r2 + p
            pltpu.make_async_copy(h_hbm.at[pl.ds(src_ref[0, 0, r], 1)], buf.at[pl.ds(r, 1)], sem).start(
                priority=p)
        return c

    lax.fori_loop(0, tmg // 2, start, 0)

    def wait(r, c):
        pltpu.make_async_copy(h_hbm.at[pl.ds(0, 1)], buf.at[pl.ds(0, 1)], sem).wait()
        return c

    lax.fori_loop(0, tmg, wait, 0)
    lo, hi = _unpack_halves(buf[...])
    half = lo.shape[1]
    o_ref[:, :half] = lo
    o_ref[:, half:] = hi


def moe_dispatch(hp, src, n_rows):
    dw = hp.shape[1]
    tmg = MOE_ROW_TILE
    return pl.pallas_call(
        functools.partial(_dispatch_kernel, tmg=tmg),
        grid=(n_rows // tmg,),
        in_specs=[pl.BlockSpec((1, 1, tmg), lambda i: (i, 0, 0), memory_space=pltpu.SMEM),
                  pl.BlockSpec(memory_space=pl.ANY)],
        out_specs=pl.BlockSpec((tmg, 2 * dw), lambda i: (i, 0)),
        out_shape=jax.ShapeDtypeStruct((n_rows, 2 * dw), BF16),
        scratch_shapes=[pltpu.VMEM((tmg, dw), hp.dtype), pltpu.SemaphoreType.DMA(())],
        compiler_params=_cparams(("arbitrary",)),
        name="moe_dispatch",
    )(src.reshape(n_rows // tmg, 1, tmg), hp)


def _grouped_kernel(e_tab, jw_tab, rx_tab, j_tab, r_tab, first_tab, valid_tab, x_ref, *rest, n_w, epilogue):
    del e_tab, jw_tab, rx_tab, j_tab, r_tab
    w_refs, o_ref, wb_refs = rest[:n_w], rest[n_w], rest[n_w + 1:]
    s = pl.program_id(0)

    @pl.when(first_tab[s] == 1)
    def _():
        for w, wb in zip(w_refs, wb_refs):
            wb[...] = w[0].astype(BF16)

    @pl.when(valid_tab[s] == 1)
    def _():
        x = x_ref[...]
        o_ref[...] = epilogue([_dot(x, wb[...]) for wb in wb_refs], []).astype(o_ref.dtype)

    @pl.when(valid_tab[s] == 0)
    def _():
        o_ref[...] = jnp.zeros(o_ref.shape, o_ref.dtype)


def grouped_matmul(x, ws, tabs, epilogue, out_dtype, tn, name):
    n_rows, kx = x.shape
    ne, kdim, n = ws[0].shape
    tmg = MOE_ROW_TILE
    tn = _tile(n, tn)
    n_steps = tabs[0].shape[0]
    grid_spec = pltpu.PrefetchScalarGridSpec(
        num_scalar_prefetch=7,
        grid=(n_steps,),
        in_specs=[pl.BlockSpec((tmg, kx), lambda s, e, jw, rx, j, r, f, v: (rx[s], 0))]
        + [pl.BlockSpec((1, kdim, tn), lambda s, e, jw, rx, j, r, f, v: (e[s], 0, jw[s])) for _ in ws],
        out_specs=pl.BlockSpec((tmg, tn), lambda s, e, jw, rx, j, r, f, v: (r[s], j[s])),
        scratch_shapes=[pltpu.VMEM((kdim, tn), BF16) for _ in ws],
    )
    return pl.pallas_call(
        functools.partial(_grouped_kernel, n_w=len(ws), epilogue=epilogue),
        grid_spec=grid_spec,
        out_shape=jax.ShapeDtypeStruct((n_rows, n), out_dtype),
        compiler_params=_cparams(("arbitrary",)),
        name=name,
    )(*tabs, x, *ws)


def _group_steps(padded, nj, n_tiles):
    tmg = MOE_ROW_TILE
    tiles = padded // tmg
    tile_start = jnp.cumsum(tiles) - tiles
    steps = tiles * nj
    step_end = jnp.cumsum(steps)
    step_start = step_end - steps
    total = step_end[-1]
    used_tiles = jnp.sum(tiles)
    s = jnp.arange(n_tiles * nj, dtype=jnp.int32)
    sc = jnp.minimum(s, total - 1)
    e = jnp.minimum(jnp.searchsorted(step_end, sc, side="right"), tiles.shape[0] - 1).astype(jnp.int32)
    local = sc - step_start[e]
    te = jnp.maximum(tiles[e], 1)
    jw = local // te
    rl = local % te
    rx = tile_start[e] + rl
    valid = s < total
    first = valid & (rl == 0)
    extra = jnp.maximum(s - total, 0)
    j = jnp.where(valid, jw, extra % nj)
    r = jnp.where(valid, rx, used_tiles + extra // nj)
    i32 = lambda a: a.astype(jnp.int32)
    return i32(e), i32(jw), i32(rx), i32(j), i32(r), i32(first), i32(valid)


def _combine_kernel(dest_ref, x_ref, route_ref, ys_hbm, o_ref, buf, sem, *, tb):
    def start(t, c):
        for k in range(TOP_K):
            pltpu.make_async_copy(ys_hbm.at[pl.ds(dest_ref[0, 0, TOP_K * t + k], 1)], buf.at[k, pl.ds(t, 1)],
                                  sem).start(priority=k % 2)
        return c

    lax.fori_loop(0, tb, start, 0)

    def wait(t, c):
        pltpu.make_async_copy(ys_hbm.at[pl.ds(0, 1)], buf.at[0, pl.ds(0, 1)], sem).wait()
        return c

    lax.fori_loop(0, TOP_K * tb, wait, 0)
    route = route_ref[...]
    o_ref[...] = x_ref[...] + route[:, 4:5] * buf[0] + route[:, 5:6] * buf[1]


def moe_combine(x, route, ys, dest):
    t, d = x.shape
    tb = _tile(t, MOE_TOKEN_BLOCK)
    return pl.pallas_call(
        functools.partial(_combine_kernel, tb=tb),
        grid=(t // tb,),
        in_specs=[pl.BlockSpec((1, 1, TOP_K * tb), lambda i: (i, 0, 0), memory_space=pltpu.SMEM),
                  pl.BlockSpec((tb, d), lambda i: (i, 0)), pl.BlockSpec((tb, LANES), lambda i: (i, 0)),
                  pl.BlockSpec(memory_space=pl.ANY)],
        out_specs=pl.BlockSpec((tb, d), lambda i: (i, 0)),
        out_shape=jax.ShapeDtypeStruct((t, d), F32),
        scratch_shapes=[pltpu.VMEM((TOP_K, tb, d), F32), pltpu.SemaphoreType.DMA(())],
        compiler_params=_cparams(("arbitrary",)),
        name="moe_combine",
    )(dest.reshape(t // tb, 1, TOP_K * tb), x, route, ys)


def moe_block(x, norm_g, w_router, b_router, w_gate, w_up, w_down):
    t, d = x.shape
    ne, _, f = w_gate.shape
    tmg = MOE_ROW_TILE
    hp, route, cnt = rmsnorm_router(x, norm_g, w_router, b_router)
    counts = cnt[0, :ne].astype(jnp.int32)
    padded = (counts + tmg - 1) // tmg * tmg
    starts = jnp.cumsum(padded) - padded
    ids = route[:, 0:2].astype(jnp.int32)
    dest = starts[ids] + route[:, 2:4].astype(jnp.int32)
    n_rows = t * TOP_K + ne * tmg
    n_tiles = n_rows // tmg
    src = jnp.zeros((n_rows,), jnp.int32).at[dest.reshape(-1)].set(
        jnp.repeat(jnp.arange(t, dtype=jnp.int32), TOP_K), unique_indices=True)
    xs = moe_dispatch(hp, src, n_rows)
    tn_up, tn_down = _tile(f, 512), _tile(d, 1024)
    act = grouped_matmul(xs, [w_gate, w_up], _group_steps(padded, f // tn_up, n_tiles), _epi_swiglu, BF16,
                         tn_up, "moe_up")
    ys = grouped_matmul(act, [w_down], _group_steps(padded, d // tn_down, n_tiles), _epi_plain, F32,
                        tn_down, "moe_down")
    return moe_combine(x, route, ys, dest)


def _mm_kernel(*refs, n_w, n_e, nk, epilogue):
    a_ref = refs[0]
    w_refs = refs[1:1 + n_w]
    e_refs = refs[1 + n_w:1 + n_w + n_e]
    o_ref = refs[1 + n_w + n_e]
    acc_refs = refs[2 + n_w + n_e:]
    a = a_ref[...]
    parts = [_dot(a, w[...]) for w in w_refs]
    if nk == 1:
        o_ref[...] = epilogue(parts, [e[...] for e in e_refs]).astype(o_ref.dtype)
        return
    k = pl.program_id(2)

    @pl.when(k == 0)
    def _():
        for acc, p in zip(acc_refs, parts):
            acc[...] = p

    @pl.when(k > 0)
    def _():
        for acc, p in zip(acc_refs, parts):
            acc[...] += p

    @pl.when(k == nk - 1)
    def _():
        o_ref[...] = epilogue([acc[...] for acc in acc_refs], [e[...] for e in e_refs]).astype(o_ref.dtype)


def matmul(a, ws, epilogue, extras=(), extra_kinds=(), out_dtype=F32, tm=1024, tn=1024, tk=4096, w_prefix=(),
           name="matmul"):
    m, kdim = a.shape
    n = ws[0].shape[-1]
    tm, tn, tk = _tile(m, tm), _tile(n, tn), _tile(kdim, tk)
    nk = kdim // tk
    squeezed = (None,) * len(w_prefix)
    in_specs = [pl.BlockSpec((tm, tk), lambda i, j, k: (i, k))]
    in_specs += [pl.BlockSpec(squeezed + (tk, tn), lambda i, j, k: (*w_prefix, k, j)) for _ in ws]
    for kind in extra_kinds:
        if kind == "mn":
            in_specs.append(pl.BlockSpec((tm, tn), lambda i, j, k: (i, j)))
        elif kind == "m":
            in_specs.append(pl.BlockSpec((tm, LANES), lambda i, j, k: (i, 0)))
        else:
            in_specs.append(pl.BlockSpec((1, tn), lambda i, j, k: (0, j)))
    scratch = [pltpu.VMEM((tm, tn), F32) for _ in ws] if nk > 1 else []
    return pl.pallas_call(
        functools.partial(_mm_kernel, n_w=len(ws), n_e=len(extras), nk=nk, epilogue=epilogue),
        grid=(m // tm, n // tn, nk),
        in_specs=in_specs,
        out_specs=pl.BlockSpec((tm, tn), lambda i, j, k: (i, j)),
        out_shape=jax.ShapeDtypeStruct((m, n), out_dtype),
        scratch_shapes=scratch,
        compiler_params=_cparams(("parallel", "parallel", "arbitrary")),
        name=name,
    )(a, *ws, *extras)


def _epi_plain(parts, extras):
    return parts[0]


def _epi_residual(parts, extras):
    return extras[0] + parts[0]


def _epi_swiglu(parts, extras):
    g, u = parts
    return g * jax.nn.sigmoid(g) * u


def _merge_kernel(h_ref, wg_ref, b_ref, y_ref, wb_ref, o_ref, accm_ref, *accg, nk, nb):
    j = pl.program_id(2)
    k = pl.program_id(3)

    @pl.when((pl.program_id(0) == 0) & (pl.program_id(1) == 0) & (j == 0) & (k == 0))
    def _():
        accm_ref[...] = jnp.zeros(accm_ref.shape, F32)

    part = _dot(h_ref[...], wg_ref[0].astype(BF16))

    def finish(pre):
        term = jax.nn.sigmoid(pre + b_ref[0]) * _dot(y_ref[0], wb_ref[0].astype(BF16))
        total = jnp.where(j == 0, term, accm_ref[...] + term)
        accm_ref[...] = total
        o_ref[...] = total.astype(o_ref.dtype)

    if nk == 1:
        finish(part)
        return

    accg_ref, = accg

    @pl.when(k == 0)
    def _():
        accg_ref[...] = part

    @pl.when(k > 0)
    def _():
        accg_ref[...] += part

    @pl.when(k == nk - 1)
    def _():
        finish(accg_ref[...])


def gated_merge(h, w_gate, b_gate, ys, w_branch, layer, tm=1024, tn=512, tk=4096):
    t, d = h.shape
    nb, _, wdt = ys.shape
    tm, tn, tk = _tile(t, tm), _tile(d, tn), _tile(d, tk)
    nk = d // tk
    assert nb >= 2
    return pl.pallas_call(
        functools.partial(_merge_kernel, nk=nk, nb=nb),
        grid=(t // tm, d // tn, nb, nk),
        in_specs=[
            pl.BlockSpec((tm, tk), lambda i, n, j, k: (i, k)),
            pl.BlockSpec((None, 1, tk, tn), lambda i, n, j, k: (layer, j, k, n)),
            pl.BlockSpec((1, 1, tn), lambda i, n, j, k: (j, 0, n)),
            pl.BlockSpec((1, tm, wdt), lambda i, n, j, k: (j, i, 0)),
            pl.BlockSpec((None, 1, wdt, tn), lambda i, n, j, k: (layer, j, 0, n)),
        ],
        out_specs=pl.BlockSpec((tm, tn), lambda i, n, j, k: (i, n)),
        out_shape=jax.ShapeDtypeStruct((t, d), BF16),
        scratch_shapes=[pltpu.VMEM((tm, tn), F32)] * (1 if nk == 1 else 2),
        compiler_params=_cparams(("arbitrary", "arbitrary", "arbitrary", "arbitrary")),
        name="gated_merge",
    )(h, w_gate, b_gate, ys, w_branch)


def _softplus(x):
    return jnp.maximum(x, 0.0) + jnp.log1p(jnp.exp(-jnp.abs(x)))


def _ssd_kernel(xbc_ref, z_ref, dt_ref, dtt_ref, cw_ref, cb_ref, dtb_ref, dtbt_ref, alog_ref, alogt_ref,
                dskip_ref, ng_ref, expand_ref, ybuf_ref, o_ref, xe_ref, st_ref, *, n_heads, d_inner):
    del ybuf_ref
    q = SSD_CHUNK
    gn = SSD_N_GROUPS * SSD_D_STATE
    hpg = n_heads // SSD_N_GROUPS
    gw = hpg * SSD_HEAD_DIM
    c = pl.program_id(1)

    @pl.when(c == 0)
    def _():
        xe_ref[0:8, :] = jnp.zeros((8, xe_ref.shape[1]), F32)
        st_ref[...] = jnp.zeros(st_ref.shape, F32)

    xe_ref[8:8 + q, :] = xbc_ref[...]
    acc = cb_ref[...] + cw_ref[3:4, :] * xe_ref[8:8 + q, :]
    for kk in range(3):
        acc = acc + cw_ref[kk:kk + 1, :] * xe_ref[5 + kk:5 + kk + q, :]
    xe_ref[0:8, :] = xe_ref[q:q + 8, :]
    xc = acc * jax.nn.sigmoid(acc)
    xs = xc[:, :d_inner]
    bm = xc[:, d_inner:d_inner + gn].astype(BF16)
    cm = xc[:, d_inner + gn:].astype(BF16)

    dt = _softplus(dt_ref[...] + dtb_ref[...])
    da = dt * (-jnp.exp(alog_ref[...]))
    row = lax.broadcasted_iota(jnp.int32, (q, q), 0)
    col = lax.broadcasted_iota(jnp.int32, (q, q), 1)
    lower = row >= col
    tril = jnp.where(lower, 1.0, 0.0).astype(BF16)
    acs = _dot_exact_lhs01(tril, da)
    dtt = _softplus(dtt_ref[0] + dtbt_ref[...])
    dat = dtt * (-jnp.exp(alogt_ref[...]))
    triu = jnp.where(row <= col, 1.0, 0.0).astype(BF16)
    acst = _dot_exact_rhs01(dat, triu)

    expand = expand_ref[...]
    dt_x = _dot_exact_rhs01(dt, expand)
    acs_x = _dot_exact_rhs01(acs, expand)
    tot_x = acs_x[q - 1:q, :]
    xdt = xs * dt_x
    xdt_b = xdt.astype(BF16)
    xw_b = (xdt * jnp.exp(tot_x - acs_x)).astype(BF16)
    in_decay = jnp.exp(acs_x)
    chunk_decay = jnp.exp(tot_x)

    lane = lax.broadcasted_iota(jnp.int32, (1, gw), 1)
    ys = []
    for g in range(SSD_N_GROUPS):
        cg = cm[:, g * SSD_D_STATE:(g + 1) * SSD_D_STATE]
        bg = bm[:, g * SSD_D_STATE:(g + 1) * SSD_D_STATE]
        sl = slice(g * gw, (g + 1) * gw)
        gmat = lax.dot_general(cg, bg, (((1,), (1,)), ((), ())), preferred_element_type=F32)
        state = st_ref[g]
        y_g = _dot(cg, state.astype(BF16)) * in_decay[:, sl]
        xg = xdt_b[:, sl]
        for hh in range(hpg):
            hd = g * hpg + hh
            seg = acs[:, hd:hd + 1] - acst[hd:hd + 1, :]
            lmat = jnp.exp(jnp.where(lower, seg, -jnp.inf))
            mh = (gmat * lmat).astype(BF16)
            head_cols = (lane >= hh * SSD_HEAD_DIM) & (lane < (hh + 1) * SSD_HEAD_DIM)
            y_g = y_g + _dot(mh, jnp.where(head_cols, xg, jnp.zeros_like(xg)))
        upd = lax.dot_general(bg, xw_b[:, sl], (((0,), (0,)), ((), ())), preferred_element_type=F32)
        st_ref[g] = state * chunk_decay[:, sl] + upd
        ys.append(y_g)
    y = jnp.concatenate(ys, axis=-1) + xs * dskip_ref[...]
    zz = z_ref[...]
    y = y * (zz * jax.nn.sigmoid(zz))
    outs = []
    for g in range(SSD_N_GROUPS):
        yg = y[:, g * gw:(g + 1) * gw]
        ms = jnp.mean(yg * yg, axis=-1, keepdims=True)
        outs.append(yg * lax.rsqrt(ms + RMS_EPS))
    o_ref[0] = (jnp.concatenate(outs, axis=-1) * ng_ref[...]).astype(o_ref.dtype)


def ssd_mixer(proj, dt_t, cols, bsz, seq, conv_w, conv_b, dt_bias, a_log, d_skip, norm_g, ybuf, slot):
    n_heads = dt_bias.shape[0]
    d_inner = n_heads * SSD_HEAD_DIM
    conv_dim = conv_w.shape[1]
    q = SSD_CHUNK
    nc = seq // q
    pad = LANES - n_heads
    expand = jnp.repeat(jnp.eye(LANES, dtype=BF16)[:, :n_heads], SSD_HEAD_DIM, axis=1)
    dtb = jnp.pad(dt_bias, (0, pad)).reshape(1, LANES)
    alog = jnp.pad(a_log, (0, pad)).reshape(1, LANES)
    x_blk, z_blk, dt_blk = cols["xbc"] // conv_dim, cols["z"] // d_inner, cols["dt"] // LANES
    const = lambda b, c: (0, 0)
    return pl.pallas_call(
        functools.partial(_ssd_kernel, n_heads=n_heads, d_inner=d_inner),
        grid=(bsz, nc),
        in_specs=[
            pl.BlockSpec((q, conv_dim), lambda b, c: (b * nc + c, x_blk)),
            pl.BlockSpec((q, d_inner), lambda b, c: (b * nc + c, z_blk)),
            pl.BlockSpec((q, LANES), lambda b, c: (b * nc + c, dt_blk)),
            pl.BlockSpec((1, n_heads, q), lambda b, c: (b, 0, c)),
            pl.BlockSpec((4, conv_dim), const),
            pl.BlockSpec((1, conv_dim), const),
            pl.BlockSpec((1, LANES), const),
            pl.BlockSpec((n_heads, 1), const),
            pl.BlockSpec((1, LANES), const),
            pl.BlockSpec((n_heads, 1), const),
            pl.BlockSpec((1, d_inner), const),
            pl.BlockSpec((1, d_inner), const),
            pl.BlockSpec((LANES, d_inner), const),
            pl.BlockSpec(memory_space=pl.ANY),
        ],
        out_specs=pl.BlockSpec((1, q, d_inner), lambda b, c: (slot, b * nc + c, 0)),
        out_shape=jax.ShapeDtypeStruct(ybuf.shape, ybuf.dtype),
        scratch_shapes=[pltpu.VMEM((q + 8, conv_dim), F32),
                        pltpu.VMEM((SSD_N_GROUPS, SSD_D_STATE, d_inner // SSD_N_GROUPS), F32)],
        input_output_aliases={13: 0},
        compiler_params=_cparams(("parallel", "arbitrary")),
        name="ssd_mixer",
    )(proj, proj, proj, dt_t, conv_w, conv_b.reshape(1, conv_dim), dtb, dt_bias.reshape(n_heads, 1),
      alog, a_log.reshape(n_heads, 1), jnp.repeat(d_skip, SSD_HEAD_DIM).reshape(1, d_inner),
      norm_g.reshape(1, d_inner), expand, ybuf)


S5_ROWS = 8


def _s5_disc_kernel(are_ref, aim_ref, ldt_ref, brt_ref, bit_ref, pr_ref, pi_ref, bbr_ref, bbi_ref):
    dt = jnp.exp(ldt_ref[...])
    ar, ai = are_ref[...], aim_ref[...]
    mag = jnp.exp(ar * dt)
    lr, li = mag * jnp.cos(ai * dt), mag * jnp.sin(ai * dt)
    den = ar * ar + ai * ai
    fr = ((lr - 1.0) * ar + li * ai) / den
    fi = (li * ar - (lr - 1.0) * ai) / den
    pr, pi = lr, li
    for k in range(S5_ROWS):
        pr_ref[k] = pr
        pi_ref[k] = pi
        pr, pi = pr * lr - pi * li, pr * li + pi * lr
    br, bi = brt_ref[...], bit_ref[...]
    bbr_ref[...] = fr[:, None, :] * br - fi[:, None, :] * bi
    bbi_ref[...] = fr[:, None, :] * bi + fi[:, None, :] * br


def s5_discretise(a_re, a_im, log_dt, b_re, b_im):
    g, n = a_re.shape
    c = b_re.shape[2]
    brt, bit = jnp.swapaxes(b_re, 1, 2), jnp.swapaxes(b_im, 1, 2)
    return pl.pallas_call(
        _s5_disc_kernel,
        out_shape=[jax.ShapeDtypeStruct((S5_ROWS, g, n), F32), jax.ShapeDtypeStruct((S5_ROWS, g, n), F32),
                   jax.ShapeDtypeStruct((g, c, n), F32), jax.ShapeDtypeStruct((g, c, n), F32)],
        name="s5_discretise",
    )(a_re, a_im, log_dt.reshape(g, 1), brt, bit)


def _block_diag(blocks, per):
    g, r, c = blocks.shape
    nb = g // per
    eye = jnp.eye(per, dtype=blocks.dtype)
    out = blocks.reshape(nb, per, r, 1, c) * eye[None, :, None, :, None]
    return out.reshape(nb, per * r, per * c)


def _s5_kernel(u_ref, wr_ref, wi_ref, cr_ref, ci_ref, ctab_ref, d_ref, wglu_ref, bglu_ref, ybuf_ref, o_ref,
               xr_scr, xi_scr, carry_ref, y_scr, *, nblk, sw):
    del ybuf_ref
    tm = u_ref.shape[0]

    @pl.when(pl.program_id(1) == 0)
    def _():
        carry_ref[...] = jnp.zeros(carry_ref.shape, F32)

    for blk in range(nblk):
        cs = slice(blk * sw, (blk + 1) * sw)
        ls = slice(blk * LANES, (blk + 1) * LANES)
        u = u_ref[:, ls]
        ub = u.astype(BF16)
        xr_scr[...] = _dot(ub, wr_ref[blk])
        xi_scr[...] = _dot(ub, wi_ref[blk])

        def body(i, carry, cs=cs):
            c_re, c_im = carry
            r0 = pl.multiple_of(i * S5_ROWS, S5_ROWS)
            sr, si = xr_scr[pl.ds(r0, S5_ROWS), :], xi_scr[pl.ds(r0, S5_ROWS), :]
            for lvl, dist in enumerate((1, 2, 4)):
                lr, li = ctab_ref[2 * lvl, :, cs], ctab_ref[2 * lvl + 1, :, cs]
                tr, ti = pltpu.roll(sr, dist, axis=0), pltpu.roll(si, dist, axis=0)
                sr, si = sr + lr * tr - li * ti, si + lr * ti + li * tr
            pr, pi = ctab_ref[6, :, cs], ctab_ref[7, :, cs]
            sr, si = sr + pr * c_re - pi * c_im, si + pr * c_im + pi * c_re
            xr_scr[pl.ds(r0, S5_ROWS), :] = sr
            xi_scr[pl.ds(r0, S5_ROWS), :] = si
            return sr[S5_ROWS - 1:S5_ROWS, :], si[S5_ROWS - 1:S5_ROWS, :]

        c_re, c_im = lax.fori_loop(0, tm // S5_ROWS, body, (carry_ref[0:1, cs], carry_ref[1:2, cs]), unroll=2)
        carry_ref[0:1, cs] = c_re
        carry_ref[1:2, cs] = c_im
        y = _dot(xr_scr[...].astype(BF16), cr_ref[blk]) - _dot(xi_scr[...].astype(BF16), ci_ref[blk])
        y_scr[:, ls] = jax.nn.gelu(y + d_ref[:, ls] * u)
    y = y_scr[...]
    o_ref[0] = (y * jax.nn.sigmoid(_dot(y.astype(BF16), wglu_ref[...]) + bglu_ref[...])).astype(o_ref.dtype)


def s5_mixer(proj, u_col, bsz, seq, a_re, a_im, b_re, b_im, c_re, c_im, log_dt, d_skip, w_glu, b_glu,
             ybuf, slot):
    t = bsz * seq
    g, n = a_re.shape
    width = g * S5_GROUP
    per = S5_GROUPS_PER_BLOCK
    nblk = g // per
    sw = per * n
    ns = g * n
    pw_r, pw_i, bbr, bbi = s5_discretise(a_re, a_im, log_dt, b_re, b_im)
    w_r = _block_diag(bbr, per).astype(BF16)
    w_i = _block_diag(bbi, per).astype(BF16)
    c_r = _block_diag(jnp.swapaxes(c_re, 1, 2), per).astype(BF16)
    c_i = _block_diag(jnp.swapaxes(c_im, 1, 2), per).astype(BF16)
    pw_r, pw_i = pw_r.reshape(S5_ROWS, ns), pw_i.reshape(S5_ROWS, ns)
    sub = jnp.arange(S5_ROWS)[:, None]
    kinds = []
    for dist in (1, 2, 4):
        kinds += [jnp.where(sub >= dist, pw_r[dist - 1][None, :], 0.0),
                  jnp.where(sub >= dist, pw_i[dist - 1][None, :], 0.0)]
    ctab = jnp.stack(kinds + [pw_r, pw_i])
    tm = _tile(seq, 256)
    nt = seq // tm
    const2 = lambda b, c: (0, 0)
    const3 = lambda b, c: (0, 0, 0)
    return pl.pallas_call(
        functools.partial(_s5_kernel, nblk=nblk, sw=sw),
        grid=(bsz, nt),
        in_specs=[pl.BlockSpec((tm, width), lambda b, c: (b * nt + c, u_col // width)),
                  pl.BlockSpec((nblk, LANES, sw), const3), pl.BlockSpec((nblk, LANES, sw), const3),
                  pl.BlockSpec((nblk, sw, LANES), const3), pl.BlockSpec((nblk, sw, LANES), const3),
                  pl.BlockSpec((8, S5_ROWS, ns), const3),
                  pl.BlockSpec((1, width), const2), pl.BlockSpec((width, width), const2),
                  pl.BlockSpec((1, width), const2), pl.BlockSpec(memory_space=pl.ANY)],
        out_specs=pl.BlockSpec((1, tm, width), lambda b, c: (slot, b * nt + c, 0)),
        out_shape=jax.ShapeDtypeStruct(ybuf.shape, ybuf.dtype),
        scratch_shapes=[pltpu.VMEM((tm, sw), F32), pltpu.VMEM((tm, sw), F32), pltpu.VMEM((8, ns), F32),
                        pltpu.VMEM((tm, width), F32)],
        input_output_aliases={9: 0},
        compiler_params=_cparams(("parallel", "arbitrary")),
        name="s5_mixer",
    )(proj, w_r, w_i, c_r, c_i, ctab, d_skip.reshape(1, width), w_glu.astype(BF16), b_glu.reshape(1, width),
      ybuf)


def _flash_kernel(*refs, tq, n_maps, has_bias, finalize):
    if has_bias:
        q_ref, k_ref, v_ref, bias_ref, far_ref, *rest = refs
    else:
        q_ref, k_ref, v_ref, *rest = refs
    fin_refs = rest[:-2]
    o_ref = rest[-1]
    h = pl.program_id(1)
    qi = pl.program_id(2)
    rows = n_maps * tq
    dvo = v_ref.shape[1]
    qv = q_ref[...].reshape(rows, q_ref.shape[-1])

    def tiled(b):
        return jnp.concatenate([b] * n_maps, axis=0) if n_maps > 1 else b

    def block(kb, carry, add, shift):
        m, l, acc = carry
        off = pl.multiple_of(kb * tq, tq)
        s = lax.dot_general(qv, k_ref[pl.ds(off, tq), :], (((1,), (1,)), ((), ())), preferred_element_type=F32)
        sb = s.astype(BF16)
        if add is not None:
            sb = sb + add
        mx = jnp.max(sb, axis=-1, keepdims=True).astype(F32)
        m_hi = jnp.maximum(m, mx if shift is None else mx + shift)
        ref_b = (m_hi if shift is None else m_hi - shift).astype(BF16)
        m_new = ref_b.astype(F32) if shift is None else ref_b.astype(F32) + shift
        alpha = jnp.exp2(m - m_new)
        p = jnp.exp2(sb - ref_b)
        l_new = alpha * l + jnp.sum(p.astype(F32), axis=-1, keepdims=True)
        return m_new, l_new, alpha * acc + _dot(p, v_ref[pl.ds(off, tq), :])

    carry = (jnp.full((rows, 1), -jnp.inf, F32), jnp.zeros((rows, 1), F32), jnp.zeros((rows, dvo), F32))
    if has_bias:
        far = far_ref[h]
        n_far = jnp.maximum(qi - 1, 0)
        carry = lax.fori_loop(0, n_far, lambda kb, c: block(kb, c, None, far), carry)
        carry = lax.fori_loop(n_far, qi, lambda kb, c: block(kb, c, tiled(bias_ref[0, 1]), None), carry)
    else:
        carry = lax.fori_loop(0, qi, lambda kb, c: block(kb, c, None, None), carry)
    row = lax.broadcasted_iota(jnp.int32, (tq, tq), 0)
    col = lax.broadcasted_iota(jnp.int32, (tq, tq), 1)
    diag = jnp.where(row >= col, bias_ref[0, 0].astype(F32) if has_bias else 0.0, -jnp.inf).astype(BF16)
    m, l, acc = block(qi, carry, tiled(diag), None)
    o_ref[0] = finalize(acc / l, [r[...] for r in fin_refs]).astype(o_ref.dtype)


def flash_attention(q, k, v, bsz, seq, n_heads, dq, ybuf, slot, n_maps=1, bias=None, far=None, fin=(),
                    finalize=None, tq=512):
    dv = v.shape[1] // n_heads
    tq = _tile(seq, tq)
    nq = seq // tq
    has_bias = bias is not None
    if finalize is None:
        finalize = lambda o, extras: o
    in_specs = [pl.BlockSpec((n_maps, tq, dq), lambda b, h, i: (0, b * nq + i, h)),
                pl.BlockSpec((seq, dq), lambda b, h, i: (b, h)),
                pl.BlockSpec((seq, dv), lambda b, h, i: (b, h))]
    args = [q, k, v]
    if has_bias:
        in_specs += [pl.BlockSpec((1, 2, tq, tq), lambda b, h, i: (h, 0, 0, 0)),
                     pl.BlockSpec(memory_space=pltpu.SMEM)]
        args += [bias, far]
    for f in fin:
        in_specs.append(pl.BlockSpec(f.shape, lambda b, h, i: (0, 0)))
        args.append(f)
    in_specs.append(pl.BlockSpec(memory_space=pl.ANY))
    args.append(ybuf)
    return pl.pallas_call(
        functools.partial(_flash_kernel, tq=tq, n_maps=n_maps, has_bias=has_bias, finalize=finalize),
        grid=(bsz, n_heads, nq),
        in_specs=in_specs,
        out_specs=pl.BlockSpec((1, tq, dv), lambda b, h, i: (slot, b * nq + i, h)),
        out_shape=jax.ShapeDtypeStruct(ybuf.shape, ybuf.dtype),
        input_output_aliases={len(args) - 1: 0},
        compiler_params=_cparams(("parallel", "parallel", "arbitrary")),
        name="flash_bias" if has_bias else "flash",
    )(*args)


def _mla_proj_kernel(cq_ref, ckv_ref, kpe_ref, cos_ref, sin_ref, qng_ref, kvng_ref, wq_ref, wkv_ref,
                     qg_ref, kg_ref, q_ref, k_ref, v_ref, *, n_heads, scale):
    half = LANES // 2

    def norm(x, g):
        return (x * lax.rsqrt(jnp.mean(x * x, axis=-1, keepdims=True) + RMS_EPS) * g).astype(BF16)

    cq = norm(cq_ref[...], qng_ref[...])
    ckv = norm(ckv_ref[...], kvng_ref[...])
    cos, sin = cos_ref[...], sin_ref[...]
    kpe = kpe_ref[...]
    kpe_ss = jnp.sum(jnp.where(lax.broadcasted_iota(jnp.int32, kpe.shape, 1) < half, kpe * kpe, 0.0),
                     axis=-1, keepdims=True)
    n_qk = float(MLA_NOPE + MLA_ROPE)
    qg, kg = qg_ref[...], kg_ref[...]

    def rotary(hi, gains, inv):
        return inv * (hi * (gains[1:2] * cos) + pltpu.roll(hi, half, axis=1) * (gains[2:3] * sin))

    k_rot_raw = kpe * (kg[1:2] * cos) + pltpu.roll(kpe, half, axis=1) * (kg[2:3] * sin)
    for h in range(n_heads):
        xq = _dot(cq, wq_ref[:, h * MLA_PAD:(h + 1) * MLA_PAD])
        lo, hi = xq[:, :LANES], xq[:, LANES:]
        ss = jnp.sum(lo * lo, axis=-1, keepdims=True) + 0.5 * jnp.sum(hi * hi, axis=-1, keepdims=True)
        inv = lax.rsqrt(ss / n_qk + RMS_EPS) * scale
        q_ref[:, h * MLA_PAD:h * MLA_PAD + LANES] = (lo * inv * qg[0:1]).astype(BF16)
        q_ref[:, h * MLA_PAD + LANES:(h + 1) * MLA_PAD] = rotary(hi, qg, inv).astype(BF16)
        xkv = _dot(ckv, wkv_ref[:, h * MLA_PAD:(h + 1) * MLA_PAD])
        kn = xkv[:, :LANES]
        inv_k = lax.rsqrt((jnp.sum(kn * kn, axis=-1, keepdims=True) + kpe_ss) / n_qk + RMS_EPS)
        k_ref[:, h * MLA_PAD:h * MLA_PAD + LANES] = (kn * inv_k * kg[0:1]).astype(BF16)
        k_ref[:, h * MLA_PAD + LANES:(h + 1) * MLA_PAD] = (k_rot_raw * inv_k).astype(BF16)
        v_ref[:, h * LANES:(h + 1) * LANES] = xkv[:, LANES:].astype(BF16)


def _rope_gain_rows(gain):
    rot = gain[MLA_NOPE:]
    r = MLA_ROPE // 2
    swapped = jnp.concatenate([rot[r:], rot[:r]])
    z = jnp.zeros((LANES - MLA_ROPE,), F32)
    return jnp.stack([gain[:MLA_NOPE], jnp.concatenate([rot, z]), jnp.concatenate([swapped, z])])


def mla_mixer(proj, cols, bsz, seq, q_norm_g, w_q_up, kv_norm_g, w_kv_up, q_gain, k_gain, ybuf, slot):
    t = bsz * seq
    hds = MLA_N_HEADS
    q_rank, kv_rank = w_q_up.shape[0], w_kv_up.shape[0]
    r = MLA_ROPE // 2
    wq = w_q_up.reshape(q_rank, hds, MLA_NOPE + MLA_ROPE)
    wq = jnp.concatenate([wq, wq[:, :, MLA_NOPE + r:], wq[:, :, MLA_NOPE:MLA_NOPE + r]], axis=-1)
    wq = wq.reshape(q_rank, hds * MLA_PAD).astype(BF16)
    wkv = w_kv_up.astype(BF16)
    inv = 1.0 / (ROPE_THETA ** (jnp.arange(0, MLA_ROPE, 2, dtype=F32) / MLA_ROPE))
    ang = jnp.arange(seq, dtype=F32)[:, None] * inv[None, :]
    zer = jnp.zeros((seq, LANES - MLA_ROPE), F32)
    cos_t = jnp.concatenate([jnp.cos(ang), jnp.cos(ang), zer], axis=-1)
    sin_t = jnp.concatenate([-jnp.sin(ang), jnp.sin(ang), zer], axis=-1)
    tm = _tile(seq, 512)
    npos = seq // tm
    const = lambda i: (0, 0)
    q, k, v = pl.pallas_call(
        functools.partial(_mla_proj_kernel, n_heads=hds, scale=LOG2E * float(MLA_NOPE + MLA_ROPE) ** -0.5),
        grid=(t // tm,),
        in_specs=[pl.BlockSpec((tm, q_rank), lambda i: (i, cols["cq"] // q_rank)),
                  pl.BlockSpec((tm, kv_rank), lambda i: (i, cols["ckv"] // kv_rank)),
                  pl.BlockSpec((tm, LANES), lambda i: (i, cols["kpe"] // LANES)),
                  pl.BlockSpec((tm, LANES), lambda i: (i % npos, 0)),
                  pl.BlockSpec((tm, LANES), lambda i: (i % npos, 0)),
                  pl.BlockSpec((1, q_rank), const), pl.BlockSpec((1, kv_rank), const),
                  pl.BlockSpec((q_rank, hds * MLA_PAD), const), pl.BlockSpec((kv_rank, hds * MLA_PAD), const),
                  pl.BlockSpec((3, LANES), const), pl.BlockSpec((3, LANES), const)],
        out_specs=[pl.BlockSpec((tm, hds * MLA_PAD), lambda i: (i, 0)),
                   pl.BlockSpec((tm, hds * MLA_PAD), lambda i: (i, 0)),
                   pl.BlockSpec((tm, hds * LANES), lambda i: (i, 0))],
        out_shape=[jax.ShapeDtypeStruct((t, hds * MLA_PAD), BF16), jax.ShapeDtypeStruct((t, hds * MLA_PAD), BF16),
                   jax.ShapeDtypeStruct((t, hds * LANES), BF16)],
        compiler_params=_cparams(("parallel",)),
        name="mla_proj",
    )(proj, proj, proj, cos_t, sin_t, q_norm_g.reshape(1, q_rank), kv_norm_g.reshape(1, kv_rank), wq, wkv,
      _rope_gain_rows(q_gain), _rope_gain_rows(k_gain))
    return flash_attention(q.reshape(1, t, hds * MLA_PAD), k, v, bsz, seq, hds, MLA_PAD, ybuf, slot, tq=1024)


def _diff_prep_kernel(q_ref, k_ref, v_ref, qg_ref, kg_ref, seg_ref, qo_ref, ko_ref, vo_ref, *, hd, scale):
    seg = seg_ref[...]
    lane = lax.broadcasted_iota(jnp.int32, (1, LANES), 1)
    n_col = q_ref.shape[1] // LANES
    for c in range(n_col):
        sl = slice(c * LANES, (c + 1) * LANES)
        xq, xk = q_ref[:, sl], k_ref[:, sl]
        ssq = _dot_exact_rhs01(xq * xq, seg)
        ssk = _dot_exact_rhs01(xk * xk, seg)
        qn = xq * lax.rsqrt(ssq / hd + RMS_EPS) * (qg_ref[...] * scale)
        kn = xk * lax.rsqrt(ssk / hd + RMS_EPS) * kg_ref[...]
        qo_ref[0, :, sl] = jnp.where(lane < hd, qn, 0.0).astype(BF16)
        qo_ref[1, :, sl] = jnp.where(lane >= hd, qn, 0.0).astype(BF16)
        ko_ref[:, sl] = kn.astype(BF16)
    vo_ref[...] = v_ref[...].astype(BF16)


def _t5_bucket_np(n):
    max_exact = REL_BUCKETS // 2
    nf = np.maximum(n, 1).astype(np.float32)
    large = max_exact + (np.log(nf / np.float32(max_exact)) / np.float32(math.log(REL_MAX_DIST / max_exact))
                         * np.float32(REL_BUCKETS - max_exact)).astype(np.int32)
    large = np.minimum(large, REL_BUCKETS - 1)
    return np.where(n < max_exact, n, large).astype(np.int32)


def _bias_tiles_kernel(bkt_ref, rb_ref, o_ref):
    h = pl.program_id(0)
    for d in range(2):
        bkt = bkt_ref[d]
        tile = jnp.zeros(bkt.shape, F32)
        for b in range(REL_BUCKETS):
            tile = jnp.where(bkt == b, rb_ref[b, h] * LOG2E, tile)
        o_ref[0, d] = tile.astype(o_ref.dtype)


def _diff_finalize(lambda_init, tq):
    def fin(o, extras):
        lq1, lk1, lq2, lk2, sub_g = extras
        lam = (jnp.exp(jnp.sum(lq1 * lk1, axis=-1, keepdims=True))
               - jnp.exp(jnp.sum(lq2 * lk2, axis=-1, keepdims=True)) + lambda_init)
        d = o[:tq] - lam * o[tq:]
        ms = jnp.mean(d * d, axis=-1, keepdims=True)
        return d * lax.rsqrt(ms + RMS_EPS) * (sub_g * (1.0 - lambda_init))
    return fin


def t5_bias_tiles(rel_bias, seq):
    hds = rel_bias.shape[1]
    tq = _tile(seq, 1024)
    assert tq + 1 > REL_MAX_DIST, "blocks two or more before the diagonal must all be at the far distance"
    idx = np.arange(tq)
    dist0 = np.maximum(idx[:, None] - idx[None, :], 0)
    dist1 = idx[:, None] - idx[None, :] + tq
    bkt = jnp.asarray(np.stack([_t5_bucket_np(dist0), _t5_bucket_np(dist1)]))
    bias = pl.pallas_call(
        _bias_tiles_kernel,
        grid=(hds,),
        in_specs=[pl.BlockSpec((2, tq, tq), lambda h: (0, 0, 0)), pl.BlockSpec(memory_space=pltpu.SMEM)],
        out_specs=pl.BlockSpec((1, 2, tq, tq), lambda h: (h, 0, 0, 0)),
        out_shape=jax.ShapeDtypeStruct((hds, 2, tq, tq), BF16),
        compiler_params=_cparams(("parallel",)),
        name="t5_bias_tiles",
    )(bkt, rel_bias)
    return bias, rel_bias[REL_BUCKETS - 1] * LOG2E, tq


def diff_mixer(proj, cols, bsz, seq, q_gain, k_gain, lq1, lk1, lq2, lk2, subln_g, bias_tiles, lambda_init,
               ybuf, slot):
    t = bsz * seq
    hds = DIFF_N_HEADS
    hd = q_gain.shape[0]
    width = hds * 2 * hd
    tm = _tile(t, 512)
    seg = (np.arange(LANES)[:, None] // hd == np.arange(LANES)[None, :] // hd)
    seg = jnp.asarray(seg, BF16)
    gq, gk = jnp.tile(q_gain, LANES // hd).reshape(1, LANES), jnp.tile(k_gain, LANES // hd).reshape(1, LANES)
    const = lambda i: (0, 0)
    qd, kd, vd = pl.pallas_call(
        functools.partial(_diff_prep_kernel, hd=hd, scale=LOG2E * float(hd) ** -0.5),
        grid=(t // tm,),
        in_specs=[pl.BlockSpec((tm, width), lambda i: (i, cols["dq"] // width)),
                  pl.BlockSpec((tm, width), lambda i: (i, cols["dk"] // width)),
                  pl.BlockSpec((tm, width), lambda i: (i, cols["dv"] // width)),
                  pl.BlockSpec((1, LANES), const), pl.BlockSpec((1, LANES), const),
                  pl.BlockSpec((LANES, LANES), const)],
        out_specs=[pl.BlockSpec((2, tm, width), lambda i: (0, i, 0)),
                   pl.BlockSpec((tm, width), lambda i: (i, 0)), pl.BlockSpec((tm, width), lambda i: (i, 0))],
        out_shape=[jax.ShapeDtypeStruct((2, t, width), BF16), jax.ShapeDtypeStruct((t, width), BF16),
                   jax.ShapeDtypeStruct((t, width), BF16)],
        compiler_params=_cparams(("parallel",)),
        name="diff_prep",
    )(proj, proj, proj, gq, gk, seg)

    bias, far, tq = bias_tiles
    fin = tuple(a.reshape(1, -1) for a in (lq1, lk1, lq2, lk2, subln_g))
    return flash_attention(qd, kd, vd, bsz, seq, hds, 2 * hd, ybuf, slot, n_maps=2, bias=bias, far=far, fin=fin,
                           finalize=_diff_finalize(lambda_init, tq), tq=tq)


def _pack_w_in(w_in_l, widths):
    names = ("z", "xbc", "dt", "u", "cq", "ckv", "kpe", "dq", "dk", "dv")
    starts = dict(zip(names, np.cumsum([0] + list(widths))[:-1]))
    width = dict(zip(names, widths))
    nl, d, n_in = w_in_l.shape
    kpe_w, dt_w, r = width["kpe"], width["dt"], width["kpe"] // 2
    order = ("xbc", "z", "u", "dq", "dk", "dv", "ckv", "cq", "kpe", "dt")
    cols, off = {}, 0
    for name in order:
        wd = LANES if name in ("kpe", "dt") else width[name]
        assert off % wd == 0, (name, off, wd)
        cols[name] = off
        off += wd
    n_out = off

    def pack_kernel(w_ref, o_ref):
        w = w_ref[0]
        rows = w.shape[0]
        for name in order:
            s0, c0 = int(starts[name]), cols[name]
            if name == "kpe":
                parts = [w[:, s0:s0 + kpe_w], w[:, s0 + r:s0 + kpe_w], w[:, s0:s0 + r]]
                if LANES > 2 * kpe_w:
                    parts.append(jnp.zeros((rows, LANES - 2 * kpe_w), F32))
                blk = jnp.concatenate(parts, axis=-1)
            elif name == "dt":
                blk = jnp.concatenate([w[:, s0:s0 + dt_w], jnp.zeros((rows, LANES - dt_w), F32)], axis=-1)
            else:
                blk = w[:, s0:s0 + width[name]]
            o_ref[0, :, c0:c0 + blk.shape[1]] = blk.astype(BF16)

    tk = _tile(d, 256)
    packed = pl.pallas_call(
        pack_kernel,
        grid=(nl, d // tk),
        in_specs=[pl.BlockSpec((1, tk, n_in), lambda l, k: (l, k, 0))],
        out_specs=pl.BlockSpec((1, tk, n_out), lambda l, k: (l, k, 0)),
        out_shape=jax.ShapeDtypeStruct((nl, d, n_out), BF16),
        compiler_params=_cparams(("parallel", "parallel")),
        name="pack_w_in",
    )(w_in_l)
    return packed, cols


def kernel(x, norm_mix_g, w_in, ssd_conv_w, ssd_conv_b, ssd_dt_bias, ssd_a_log, ssd_d, ssd_norm_g, s5_a_re, s5_a_im, s5_b_re, s5_b_im, s5_c_re, s5_c_im, s5_log_dt, s5_d, s5_w_glu, s5_b_glu, mla_q_norm_g, mla_w_q_up, mla_kv_norm_g, mla_w_kv_up, mla_q_gain, mla_k_gain, diff_q_gain, diff_k_gain, diff_lq1, diff_lk1, diff_lq2, diff_lk2, diff_subln_g, rel_bias, w_gate, b_gate, w_branch, w_out, norm_ffn_g, ffn_w_gate, ffn_w_up, ffn_w_down, moe_w_router, moe_b_router, moe_w_gate, moe_w_up, moe_w_down):
    bsz, seq, d = x.shape
    t = bsz * seq
    depth = w_in.shape[0]
    mix = d // 4
    n_ssd_heads = ssd_dt_bias.shape[1]
    widths = (mix, ssd_conv_w.shape[2], n_ssd_heads, mix, mla_w_q_up.shape[1], mla_w_kv_up.shape[1], MLA_ROPE,
              mix, mix, mix)
    xf = x.reshape(t, d)
    w_in_b, cols = _pack_w_in(w_in, widths)
    w_out_b = w_out.astype(BF16)
    bias_tiles = t5_bias_tiles(rel_bias, seq)
    ffn_gate_b, ffn_up_b, ffn_down_b = ffn_w_gate.astype(BF16), ffn_w_up.astype(BF16), ffn_w_down.astype(BF16)
    ys = jnp.zeros((N_BRANCH, t, mix), BF16)
    for i in range(depth):
        h = rmsnorm(xf, norm_mix_g[i])
        proj = matmul(h, [w_in_b], _epi_plain, out_dtype=F32, tn=512, w_prefix=(i,), name="in_proj")
        dt_t = jnp.swapaxes(proj[:, cols["dt"]:cols["dt"] + n_ssd_heads].reshape(bsz, seq, n_ssd_heads), 1, 2)
        ys = ssd_mixer(proj, dt_t, cols, bsz, seq, ssd_conv_w[i], ssd_conv_b[i], ssd_dt_bias[i], ssd_a_log[i],
                       ssd_d[i], ssd_norm_g[i], ys, 0)
        ys = s5_mixer(proj, cols["u"], bsz, seq, s5_a_re[i], s5_a_im[i], s5_b_re[i], s5_b_im[i], s5_c_re[i],
                      s5_c_im[i], s5_log_dt[i], s5_d[i], s5_w_glu[i], s5_b_glu[i], ys, 1)
        ys = mla_mixer(proj, cols, bsz, seq, mla_q_norm_g[i], mla_w_q_up[i], mla_kv_norm_g[i], mla_w_kv_up[i],
                       mla_q_gain[i], mla_k_gain[i], ys, 2)
        lambda_init = 0.8 - 0.6 * math.exp(-0.3 * i)
        ys = diff_mixer(proj, cols, bsz, seq, diff_q_gain[i], diff_k_gain[i], diff_lq1[i], diff_lk1[i],
                        diff_lq2[i], diff_lk2[i], diff_subln_g[i], bias_tiles, lambda_init, ys, 3)
        merged = gated_merge(h, w_gate, b_gate[i][:, None, :], ys, w_branch, i)
        xf = matmul(merged, [w_out_b], _epi_residual, extras=(xf,), extra_kinds=("mn",), tk=2048, w_prefix=(i,),
                    name="out_proj")
        e = i // 2
        if i % 2 == 0:
            h2 = rmsnorm(xf, norm_ffn_g[i])
            act = matmul(h2, [ffn_gate_b, ffn_up_b], _epi_swiglu, out_dtype=BF16, tn=512, w_prefix=(e,),
                         name="ffn_up")
            xf = matmul(act, [ffn_down_b], _epi_residual, extras=(xf,), extra_kinds=("mn",), tk=3584,
                        w_prefix=(e,), name="ffn_down")
        else:
            xf = moe_block(xf, norm_ffn_g[i], moe_w_router[e], moe_b_router[e], moe_w_gate[e], moe_w_up[e],
                           moe_w_down[e])
    return xf.reshape(bsz, seq, d)
```
